```python
import jax, jax.numpy as jnp
from jax import lax
import numpy as np

D_MODEL = 1024
BATCH = 8
SEQ = 4096
DEPTH = 4

HEAD_DIM = 64
GROUP_WIDTH = D_MODEL // 4
GROUP_HEADS = GROUP_WIDTH // HEAD_DIM
MIX_WIDTH = 4 * GROUP_WIDTH
CMP_BLOCK = 32
CMP_STRIDE = 16
CMP_HIDDEN = 128
SLC_BLOCK = 64
N_SEL_BLOCKS = 16
NSA_WINDOW = 512
DILATED_PAIRS = ((128, 1), (512, 4), (2048, 16))
BAND_BLOCK = 128
Q_CHUNK = 128
RWKV_W_LORA = 64
RWKV_A_LORA = 64
RWKV_G_LORA = 128
RWKV_GN_EPS = 64e-5
CONV_K = 31
FFN_CONV_K = 3
D_FF = 11 * D_MODEL // 4
RMS_EPS = 1e-6
LN_EPS = 1e-5
NEG = -1e30
FORCE_SCORE = 1e9
NSA_COLS = GROUP_WIDTH + 6 * HEAD_DIM + 3 * GROUP_HEADS
DIL_COLS = 3 * GROUP_WIDTH
RWKV_COLS = 3 * GROUP_WIDTH + RWKV_W_LORA + RWKV_A_LORA + RWKV_G_LORA
CONV_COLS = 2 * GROUP_WIDTH
IN_COLS = NSA_COLS + DIL_COLS + RWKV_COLS + CONV_COLS

kernel_name = 'hymba_nsa_dilated_rwkv7_conformer_trunk'


def split_cols(x, sizes):
    return jnp.split(x, [int(s) for s in np.cumsum(sizes)[:-1]], axis=-1)


def rmsnorm(x, g):
    xf = x.astype(jnp.float32)
    y = xf * lax.rsqrt(jnp.mean(xf * xf, axis=-1, keepdims=True) + RMS_EPS)
    return (y * g).astype(x.dtype)


def layernorm(x, g, b, eps):
    xf = x.astype(jnp.float32)
    mu = jnp.mean(xf, axis=-1, keepdims=True)
    var = jnp.mean((xf - mu) ** 2, axis=-1, keepdims=True)
    return ((xf - mu) * lax.rsqrt(var + eps) * g + b).astype(x.dtype)


def masked_softmax(s, mask):
    p = jax.nn.softmax(jnp.where(mask, s.astype(jnp.float32), NEG), axis=-1)
    return jnp.where(mask, p, 0.0)


def causal_dwconv(x, w, b):
    K, C = w.shape
    y = lax.conv_general_dilated(x, w[:, None, :].astype(x.dtype), window_strides=(1,),
                                 padding=[(K - 1, 0)], dimension_numbers=('NWC', 'WIO', 'NWC'),
                                 feature_group_count=C)
    return y + b


def banded_causal_attention(q, k, v, window, block):
    lead = q.shape[:-2]
    L, dh = q.shape[-2:]
    k = jnp.broadcast_to(k, lead + k.shape[-2:])
    v = jnp.broadcast_to(v, lead + v.shape[-2:])
    n_blk = -(-L // block)
    Lp = n_blk * block
    n_prev = -(-window // block)
    nz = [(0, 0)] * len(lead)
    qb = jnp.pad(q, nz + [(0, Lp - L), (0, 0)]).reshape(lead + (n_blk, block, dh))
    kp = jnp.pad(k, nz + [(n_prev * block, Lp - L), (0, 0)])
    vp = jnp.pad(v, nz + [(n_prev * block, Lp - L), (0, 0)])

    def windows(a):
        return jnp.concatenate([a[..., o * block:o * block + Lp, :].reshape(lead + (n_blk, block, dh))
                                for o in range(n_prev + 1)], axis=-2)

    kb, vb = windows(kp), windows(vp)
    s = jnp.einsum('...nqd,...nkd->...nqk', qb, kb).astype(jnp.float32) * (dh ** -0.5)
    qpos = jnp.arange(Lp).reshape(n_blk, block, 1)
    kpos = (jnp.arange(n_blk)[:, None, None] * block
            + jnp.arange((n_prev + 1) * block)[None, None, :] - n_prev * block)
    dist = qpos - kpos
    mask = (dist >= 0) & (dist <= window) & (kpos >= 0)
    s = jnp.where(mask, s, NEG)
    m = jnp.max(s, axis=-1, keepdims=True)
    e = jnp.exp(s - m)
    den = jnp.sum(e, axis=-1, keepdims=True)
    lse = (m + jnp.log(den))[..., 0]
    o = jnp.einsum('...nqk,...nkd->...nqd', (e / den).astype(v.dtype), vb)
    return (o.reshape(lead + (Lp, dh))[..., :L, :], lse.reshape(lead + (Lp,))[..., :L])


def compress_blocks(k, pos, w1, w2):
    B, S, dh = k.shape
    half = k.reshape(B, S // CMP_STRIDE, CMP_STRIDE, dh)
    blocks = jnp.concatenate([half[:, :-1], half[:, 1:]], axis=2)
    flat = (blocks + pos).reshape(B, blocks.shape[1], CMP_BLOCK * dh)
    return jax.nn.silu(flat @ w1) @ w2


def selected_block_attention(q, ks, vs, idx):
    B, S, H, dh = q.shape
    n_sel = idx.shape[-1]
    nq = S // Q_CHUNK
    k_blocks = ks.reshape(B, S // SLC_BLOCK, SLC_BLOCK, dh)
    v_blocks = vs.reshape(B, S // SLC_BLOCK, SLC_BLOCK, dh)
    b_idx = jnp.arange(B)[:, None, None]
    offs = jnp.arange(SLC_BLOCK)

    def chunk_fn(args):
        q_c, idx_c, t_c = args
        k_sel = k_blocks[b_idx, idx_c].reshape(B, Q_CHUNK, n_sel * SLC_BLOCK, dh)
        v_sel = v_blocks[b_idx, idx_c].reshape(B, Q_CHUNK, n_sel * SLC_BLOCK, dh)
        kpos = (idx_c[..., None] * SLC_BLOCK + offs).reshape(B, Q_CHUNK, n_sel * SLC_BLOCK)
        mask = (kpos <= t_c[None, :, None])[:, :, None, :]
        s = jnp.einsum('bqhd,bqnd->bqhn', q_c, k_sel) * (dh ** -0.5)
        p = masked_softmax(s, mask)
        return jnp.einsum('bqhn,bqnd->bqhd', p.astype(v_sel.dtype), v_sel)

    xs = (q.reshape(B, nq, Q_CHUNK, H, dh).swapaxes(0, 1),
          idx.reshape(B, nq, Q_CHUNK, n_sel).swapaxes(0, 1),
          jnp.arange(S).reshape(nq, Q_CHUNK))
    o = lax.map(chunk_fn, xs)
    return o.swapaxes(0, 1).reshape(B, S, H, dh)


def nsa_mixer(pa, cmp_pos, ck_w1, ck_w2, cv_w1, cv_w2):
    B, S, _ = pa.shape
    q, kc, vc, ks, vs, kw, vw, gates = split_cols(pa, [GROUP_WIDTH] + [HEAD_DIM] * 6 + [3 * GROUP_HEADS])
    q = q.reshape(B, S, GROUP_HEADS, HEAD_DIM)
    gates = jax.nn.sigmoid(gates.reshape(B, S, GROUP_HEADS, 3))
    t = jnp.arange(S)
    k_cmp = compress_blocks(kc, cmp_pos, ck_w1, ck_w2)
    v_cmp = compress_blocks(vc, cmp_pos, cv_w1, cv_w2)
    nC = k_cmp.shape[1]
    c_end = jnp.arange(nC) * CMP_STRIDE + CMP_BLOCK - 1
    s_cmp = jnp.einsum('bshd,bcd->bhsc', q, k_cmp) * (HEAD_DIM ** -0.5)
    p_cmp = masked_softmax(s_cmp, c_end[None, :] <= t[:, None])
    o_cmp = jnp.einsum('bhsc,bcd->bshd', p_cmp.astype(v_cmp.dtype), v_cmp)
    imp = jnp.pad(jnp.sum(p_cmp, axis=1), ((0, 0), (0, 0), (1, 1)))
    chunk = imp[..., :-1] + imp[..., 1:]
    nS = S // SLC_BLOCK
    blk = chunk.reshape(B, S, nS, SLC_BLOCK // CMP_STRIDE).sum(-1)
    j = jnp.arange(nS)[None, :]
    cur = (t // SLC_BLOCK)[:, None]
    forced = (j == 0) | (j == cur) | (j == cur - 1)
    valid = j * SLC_BLOCK <= t[:, None]
    score = jnp.where(valid, jnp.where(forced, FORCE_SCORE, blk), NEG)
    _, idx = lax.top_k(score, min(N_SEL_BLOCKS, nS))
    o_slc = selected_block_attention(q, ks, vs, idx)
    o_win, _ = banded_causal_attention(q.transpose(0, 2, 1, 3), kw[:, None], vw[:, None], NSA_WINDOW, BAND_BLOCK)
    o_win = o_win.transpose(0, 2, 1, 3)
    out = gates[..., 0:1] * o_cmp + gates[..., 1:2] * o_slc + gates[..., 2:3] * o_win
    return out.reshape(B, S, GROUP_WIDTH)


def dilated_mixer(pb):
    B, S, _ = pb.shape
    H, dh = GROUP_HEADS, HEAD_DIM
    q, k, v = [a.reshape(B, S, H, dh).transpose(0, 2, 1, 3) for a in split_cols(pb, [GROUP_WIDTH] * 3)]
    outs, lses = [], []
    for window, dil in DILATED_PAIRS:
        def by_residue(a):
            return a.reshape(B, H, S // dil, dil, dh).swapaxes(2, 3)
        o, lse = banded_causal_attention(by_residue(q), by_residue(k), by_residue(v), window // dil, BAND_BLOCK)
        outs.append(o.swapaxes(2, 3).reshape(B, H, S, dh))
        lses.append(lse.swapaxes(2, 3).reshape(B, H, S))
    wts = jax.nn.softmax(jnp.stack(lses), axis=0)
    out = jnp.sum(wts[..., None].astype(outs[0].dtype) * jnp.stack(outs), axis=0)
    return out.transpose(0, 2, 1, 3).reshape(B, S, GROUP_WIDTH)


def rwkv7_scan(r, w, k, v, kk, a):
    def step(state, inp):
        r_t, w_t, k_t, v_t, kk_t, a_t = inp
        sa = jnp.einsum('bhvk,bhk->bhv', state, -kk_t)
        state = (state * w_t[:, :, None, :] + sa[..., None] * (kk_t * a_t)[:, :, None, :]
                 + v_t[..., None] * k_t[:, :, None, :])
        return state, jnp.einsum('bhvk,bhk->bhv', state, r_t)

    B, S, H, dh = r.shape
    xs = tuple(jnp.moveaxis(t, 1, 0) for t in (r, w, k, v, kk, a))
    _, y = lax.scan(step, jnp.zeros((B, H, dh, dh), jnp.float32), xs)
    return jnp.moveaxis(y, 0, 1)


def rwkv7_mixer(pc, mu, w0, w_up, a0, a_up, g_up, k_k, k_a, r_k, ln_g, ln_b):
    B, S, _ = pc.shape
    H, dh = GROUP_HEADS, HEAD_DIM
    prev = jnp.pad(pc, ((0, 0), (1, 0), (0, 0)))[:, :-1]
    pc = pc + (prev - pc) * mu
    r, k, v, wd, ad, gd = split_cols(pc, [GROUP_WIDTH] * 3 + [RWKV_W_LORA, RWKV_A_LORA, RWKV_G_LORA])
    log_w = -jax.nn.softplus(-(w0 + jnp.tanh(wd) @ w_up)) - 0.5
    decay = jnp.exp(-jnp.exp(log_w.astype(jnp.float32)))
    a = jax.nn.sigmoid(a0 + ad @ a_up)
    g = jax.nn.sigmoid(gd) @ g_up

    def heads(t):
        return t.reshape(B, S, H, dh).astype(jnp.float32)

    kk = heads(k * k_k)
    kk = kk * lax.rsqrt(jnp.sum(kk * kk, axis=-1, keepdims=True) + 1e-12)
    k = k * (1 + (a - 1) * k_a)
    r_h, k_h, v_h, a_h, w_h = heads(r), heads(k), heads(v), heads(a), heads(decay)
    y = rwkv7_scan(r_h, w_h, k_h, v_h, kk, a_h)
    y = y + jnp.sum(r_h * k_h * r_k, axis=-1, keepdims=True) * v_h
    mu_y = jnp.mean(y, axis=-1, keepdims=True)
    var_y = jnp.mean((y - mu_y) ** 2, axis=-1, keepdims=True)
    y = ((y - mu_y) * lax.rsqrt(var_y + RWKV_GN_EPS)).reshape(B, S, GROUP_WIDTH) * ln_g + ln_b
    return y.astype(pc.dtype) * g


def conformer_conv(pd, dw, dw_b, ln_g, ln_b):
    a, b = split_cols(pd, [GROUP_WIDTH, GROUP_WIDTH])
    h = causal_dwconv(a * jax.nn.sigmoid(b), dw, dw_b)
    return jax.nn.silu(layernorm(h, ln_g, ln_b, LN_EPS))


def conv_ffn(h, up, dw, dw_b, down):
    u = causal_dwconv(h @ up, dw, dw_b)
    gate, val = split_cols(u, [D_FF, D_FF])
    return (jax.nn.silu(gate) * val) @ down


def setup_inputs(seed: int = 0) -> dict:
    key = jax.random.key(seed)
    ks = jax.random.split(key, 32)
    f32 = jnp.float32
    L = DEPTH

    def nrm(k, shape, scale):
        return jax.random.normal(k, shape, f32) * scale

    return {
        'x': nrm(ks[0], (BATCH, SEQ, D_MODEL), 1.0),
        'norm_mix': 1 + nrm(ks[1], (L, D_MODEL), 0.02),
        'w_in': nrm(ks[2], (L, D_MODEL, IN_COLS), D_MODEL ** -0.5),
        'cmp_pos': nrm(ks[3], (L, CMP_BLOCK, HEAD_DIM), 0.1),
        'cmp_k_w1': nrm(ks[4], (L, CMP_BLOCK * HEAD_DIM, CMP_HIDDEN), (CMP_BLOCK * HEAD_DIM) ** -0.5),
        'cmp_k_w2': nrm(ks[5], (L, CMP_HIDDEN, HEAD_DIM), CMP_HIDDEN ** -0.5),
        'cmp_v_w1': nrm(ks[6], (L, CMP_BLOCK * HEAD_DIM, CMP_HIDDEN), (CMP_BLOCK * HEAD_DIM) ** -0.5),
        'cmp_v_w2': nrm(ks[7], (L, CMP_HIDDEN, HEAD_DIM), CMP_HIDDEN ** -0.5),
        'beta_nsa': 1 + nrm(ks[8], (L, GROUP_WIDTH), 0.02),
        'beta_dil': 1 + nrm(ks[9], (L, GROUP_WIDTH), 0.02),
        'rwkv_mu': jax.random.uniform(ks[10], (L, RWKV_COLS), f32),
        'rwkv_w0': nrm(ks[11], (L, GROUP_WIDTH), 0.5),
        'rwkv_w_up': nrm(ks[12], (L, RWKV_W_LORA, GROUP_WIDTH), 0.5 * RWKV_W_LORA ** -0.5),
        'rwkv_a0': nrm(ks[13], (L, GROUP_WIDTH), 0.1),
        'rwkv_a_up': nrm(ks[14], (L, RWKV_A_LORA, GROUP_WIDTH), RWKV_A_LORA ** -0.5),
        'rwkv_g_up': nrm(ks[15], (L, RWKV_G_LORA, GROUP_WIDTH), RWKV_G_LORA ** -0.5),
        'rwkv_k_k': 0.85 + nrm(ks[16], (L, GROUP_WIDTH), 0.02),
        'rwkv_k_a': 1 + nrm(ks[17], (L, GROUP_WIDTH), 0.02),
        'rwkv_r_k': nrm(ks[18], (L, GROUP_HEADS, HEAD_DIM), 0.1),
        'rwkv_ln_g': 1 + nrm(ks[19], (L, GROUP_WIDTH), 0.02),
        'rwkv_ln_b': nrm(ks[20], (L, GROUP_WIDTH), 0.02),
        'conv_dw': nrm(ks[21], (L, CONV_K, GROUP_WIDTH), CONV_K ** -0.5),
        'conv_dw_b': nrm(ks[22], (L, GROUP_WIDTH), 0.02),
        'conv_ln_g': 1 + nrm(ks[23], (L, GROUP_WIDTH), 0.02),
        'conv_ln_b': nrm(ks[24], (L, GROUP_WIDTH), 0.02),
        'w_out': nrm(ks[25], (L, MIX_WIDTH, D_MODEL), MIX_WIDTH ** -0.5),
        'norm_ffn': 1 + nrm(ks[26], (L, D_MODEL), 0.02),
        'ffn_up': nrm(ks[27], (L, D_MODEL, 2 * D_FF), D_MODEL ** -0.5),
        'ffn_dw': nrm(ks[28], (L, FFN_CONV_K, 2 * D_FF), FFN_CONV_K ** -0.5),
        'ffn_dw_b': nrm(ks[29], (L, 2 * D_FF), 0.02),
        'ffn_down': nrm(ks[30], (L, D_FF, D_MODEL), D_FF ** -0.5),
        'norm_final': 1 + nrm(ks[31], (D_MODEL,), 0.02),
    }


def reference(x, norm_mix, w_in, cmp_pos, cmp_k_w1, cmp_k_w2, cmp_v_w1, cmp_v_w2, beta_nsa, beta_dil,
              rwkv_mu, rwkv_w0, rwkv_w_up, rwkv_a0, rwkv_a_up, rwkv_g_up, rwkv_k_k, rwkv_k_a, rwkv_r_k,
              rwkv_ln_g, rwkv_ln_b, conv_dw, conv_dw_b, conv_ln_g, conv_ln_b, w_out, norm_ffn,
              ffn_up, ffn_dw, ffn_dw_b, ffn_down, norm_final):
    for l in range(DEPTH):
        h = rmsnorm(x, norm_mix[l])
        proj = h @ w_in[l]
        pa, pb, pc, pd = split_cols(proj, [NSA_COLS, DIL_COLS, RWKV_COLS, CONV_COLS])
        y_a = rmsnorm(nsa_mixer(pa, cmp_pos[l], cmp_k_w1[l], cmp_k_w2[l], cmp_v_w1[l], cmp_v_w2[l]), beta_nsa[l])
        y_b = rmsnorm(dilated_mixer(pb), beta_dil[l])
        y_c = rwkv7_mixer(pc, rwkv_mu[l], rwkv_w0[l], rwkv_w_up[l], rwkv_a0[l], rwkv_a_up[l], rwkv_g_up[l],
                          rwkv_k_k[l], rwkv_k_a[l], rwkv_r_k[l], rwkv_ln_g[l], rwkv_ln_b[l])
        y_d = conformer_conv(pd, conv_dw[l], conv_dw_b[l], conv_ln_g[l], conv_ln_b[l])
        x = x + jnp.concatenate([y_a, y_b, y_c, y_d], axis=-1) @ w_out[l]
        h = rmsnorm(x, norm_ffn[l])
        x = x + conv_ffn(h, ffn_up[l], ffn_dw[l], ffn_dw_b[l], ffn_down[l])
    return rmsnorm(x, norm_final)
```

```python
import functools

import jax
import jax.numpy as jnp
import numpy as np
from jax import lax
from jax.experimental import pallas as pl
from jax.experimental.pallas import tpu as pltpu

F32 = jnp.float32
BF16 = jnp.bfloat16

HEAD_DIM = 64
GROUP_HEADS = 4
GROUP_WIDTH = HEAD_DIM * GROUP_HEADS
CMP_STRIDE = 16
CMP_BLOCK = 32
SLC_BLOCK = 64
N_SEL_BLOCKS = 16
NSA_WINDOW = 512
DILATIONS = (1, 4, 16)
TILE = 128
CONV_K = 31
FFN_CONV_K = 3
RMS_EPS = 1e-6
LN_EPS = 1e-5
RWKV_GN_EPS = 64e-5
NEG = -1e30
FORCE_SCORE = 1e9
SCALE = HEAD_DIM ** -0.5

C_PC, C_PD, C_QN, C_G0, C_G1, C_G2 = 0, 1024, 1536, 1792, 2048, 2304
C_QD, C_KD, C_VD, C_KCVC, C_KSVS, C_KWVW = 2560, 2816, 3072, 3328, 3456, 3584
N_PROJ = 3840

VMEM_LIMIT = 48 * 1024 * 1024


def _cparams(*sem):
    return pltpu.CompilerParams(dimension_semantics=sem, vmem_limit_bytes=VMEM_LIMIT)


def _proj_columns():
    gw, hd = GROUP_WIDTH, HEAD_DIM
    a0 = 0
    q = np.arange(a0, a0 + gw)
    kc, vc, ks, vs, kw, vw = [np.arange(a0 + gw + i * hd, a0 + gw + (i + 1) * hd) for i in range(6)]
    gate0 = a0 + gw + 6 * hd
    gates = [np.repeat(gate0 + 3 * np.arange(GROUP_HEADS) + j, hd) for j in range(3)]
    b0 = gate0 + 3 * GROUP_HEADS
    qd, kd, vd = [np.arange(b0 + i * gw, b0 + (i + 1) * gw) for i in range(3)]
    c0 = b0 + 3 * gw
    pc = np.arange(c0, c0 + 4 * gw)
    d0 = c0 + 4 * gw
    pd = np.arange(d0, d0 + 2 * gw)
    cols = np.concatenate([pc, pd, q, gates[0], gates[1], gates[2], qd, kd, vd, kc, vc, ks, vs, kw, vw,
                           np.full(N_PROJ - C_KWVW - 2 * hd, -1)])
    assert cols.shape[0] == N_PROJ
    return cols


def _dot_nt(a, b):
    return lax.dot_general(a, b, (((1,), (1,)), ((), ())), preferred_element_type=F32)


def _dot(a, b):
    return jnp.dot(a, b, preferred_element_type=F32)


def _seg_sum(x, ones_bf16):
    hi = x.astype(BF16)
    lo = (x - hi.astype(F32)).astype(BF16)
    return _dot(hi, ones_bf16) + _dot(lo, ones_bf16)


def _sigmoid(x):
    return 1.0 / (1.0 + jnp.exp(-x))


def _silu(x):
    return x * _sigmoid(x)


def _proj_in_kernel(x_ref, g_ref, w_ref, o_ref, h_ref):
    @pl.when(pl.program_id(1) == 0)
    def _():
        x = x_ref[...]
        ms = jnp.mean(x * x, axis=-1, keepdims=True)
        h_ref[...] = (x * lax.rsqrt(ms + RMS_EPS) * g_ref[...]).astype(BF16)

    o_ref[...] = _dot(h_ref[...], w_ref[...])


def _proj_in(x2, gain, w, tm=512, tn=768):
    m, d = x2.shape
    n = w.shape[1]
    return pl.pallas_call(
        _proj_in_kernel,
        grid=(m // tm, n // tn),
        in_specs=[pl.BlockSpec((tm, d), lambda i, j: (i, 0)),
                  pl.BlockSpec((1, d), lambda i, j: (0, 0)),
                  pl.BlockSpec((d, tn), lambda i, j: (0, j))],
        out_specs=pl.BlockSpec((tm, tn), lambda i, j: (i, j)),
        out_shape=jax.ShapeDtypeStruct((m, n), F32),
        scratch_shapes=[pltpu.VMEM((tm, d), BF16)],
        compiler_params=_cparams("parallel", "arbitrary"),
        name="proj_in",
    )(x2, gain, w)


def _compress_kernel(kh_ref, vh_ref, plo_ref, phi_ref, kw1_ref, kw2_ref, vw1_ref, vw2_ref, ko_ref, vo_ref):
    half = kh_ref.shape[2]
    n = kh_ref.shape[1]

    def mlp(xh, w1_ref, w2_ref):
        a = _dot((xh + plo_ref[...]).astype(BF16), w1_ref[0:half, :])
        bm = _dot((xh + phi_ref[...]).astype(BF16), w1_ref[half:2 * half, :])
        hid = a + pltpu.roll(bm, n - 1, axis=0)
        return _dot(_silu(hid).astype(BF16), w2_ref[...])

    ko_ref[0] = mlp(kh_ref[0], kw1_ref, kw2_ref)
    vo_ref[0] = mlp(vh_ref[0], vw1_ref, vw2_ref)


def _compress(kh, vh, pos_lo, pos_hi, kw1, kw2, vw1, vw2):
    b, n, half = kh.shape
    hid = kw1.shape[1]
    full = lambda shape: pl.BlockSpec(shape, lambda i: (0,) * len(shape))
    return pl.pallas_call(
        _compress_kernel,
        grid=(b,),
        in_specs=[pl.BlockSpec((1, n, half), lambda i: (i, 0, 0)),
                  pl.BlockSpec((1, n, half), lambda i: (i, 0, 0)),
                  full((1, half)), full((1, half)),
                  full((2 * half, hid)), full((hid, HEAD_DIM)),
                  full((2 * half, hid)), full((hid, HEAD_DIM))],
        out_specs=[pl.BlockSpec((1, n, HEAD_DIM), lambda i: (i, 0, 0))] * 2,
        out_shape=[jax.ShapeDtypeStruct((b, n, HEAD_DIM), F32)] * 2,
        compiler_params=_cparams("parallel"),
        name="nsa_compress",
    )(kh, vh, pos_lo, pos_hi, kw1, kw2, vw1, vw2)


def _nsa_cmp_kernel(q_ref, kc_ref, vc_ref, gt_ref, ocmp_ref, sel_ref):
    t0 = pl.program_id(1) * TILE
    n_cmp = kc_ref.shape[1]
    n_blk = gt_ref.shape[0]
    q = q_ref[0]
    kc = kc_ref[0].astype(BF16)
    vc = vc_ref[0].astype(BF16)
    trow = t0 + lax.broadcasted_iota(jnp.int32, (TILE, n_cmp), 0)
    ci = lax.broadcasted_iota(jnp.int32, (TILE, n_cmp), 1)
    cmask = jnp.where(ci < n_cmp - 1, CMP_STRIDE * ci + CMP_BLOCK - 1, 1 << 30) <= trow
    imp = jnp.zeros((TILE, n_cmp), F32)
    outs = []
    for h in range(GROUP_HEADS):
        qh = (q[:, HEAD_DIM * h:HEAD_DIM * (h + 1)] * SCALE).astype(BF16)
        s = jnp.where(cmask, _dot_nt(qh, kc), NEG)
        e = jnp.exp(s - jnp.max(s, axis=-1, keepdims=True))
        p = jnp.where(cmask, e / jnp.sum(e, axis=-1, keepdims=True), 0.0)
        imp = imp + p
        outs.append(_dot(p.astype(BF16), vc))
    ocmp_ref[0] = jnp.concatenate(outs, axis=1)

    imp_t = imp.T
    hi = imp_t.astype(BF16)
    lo = (imp_t - hi.astype(F32)).astype(BF16)
    blk = _dot(gt_ref[...], hi) + _dot(gt_ref[...], lo)
    j = lax.broadcasted_iota(jnp.int32, (n_blk, TILE), 0)
    t = t0 + lax.broadcasted_iota(jnp.int32, (n_blk, TILE), 1)
    cur = lax.shift_right_logical(t, 6)
    forced = jnp.where(j == 0, 1, jnp.where(j == cur, 1, jnp.where(j == cur - 1, 1, 0)))
    score = jnp.where(j * SLC_BLOCK <= t, jnp.where(forced == 1, FORCE_SCORE, blk), NEG)
    rank = jnp.zeros((n_blk, TILE), jnp.int32)
    for jp in range(n_blk):
        row = score[jp:jp + 1, :]
        tie = jnp.where(j > jp, 1, 0)
        rank = rank + jnp.where(row > score, 1, jnp.where(row == score, tie, 0))
    sel_neg = jnp.where(rank < N_SEL_BLOCKS, 0.0, NEG)
    full = jnp.concatenate([sel_neg, jnp.zeros((TILE - n_blk, TILE), F32)], axis=0)
    sel_ref[0] = full.T


def _nsa_cmp(proj, k_cmp, v_cmp, gt):
    b, s, _ = proj.shape
    n_cmp = k_cmp.shape[1]
    return pl.pallas_call(
        _nsa_cmp_kernel,
        grid=(b, s // TILE),
        in_specs=[pl.BlockSpec((1, TILE, GROUP_WIDTH), lambda i, j: (i, j, C_QN // GROUP_WIDTH)),
                  pl.BlockSpec((1, n_cmp, HEAD_DIM), lambda i, j: (i, 0, 0)),
                  pl.BlockSpec((1, n_cmp, HEAD_DIM), lambda i, j: (i, 0, 0)),
                  pl.BlockSpec(gt.shape, lambda i, j: (0, 0))],
        out_specs=[pl.BlockSpec((1, TILE, GROUP_WIDTH), lambda i, j: (i, j, 0)),
                   pl.BlockSpec((1, TILE, TILE), lambda i, j: (i, j, 0))],
        out_shape=[jax.ShapeDtypeStruct((b, s, GROUP_WIDTH), F32),
                   jax.ShapeDtypeStruct((b, s, TILE), F32)],
        compiler_params=_cparams("parallel", "parallel"),
        name="nsa_cmp",
    )(proj, k_cmp, v_cmp, gt)


def _nsa_flash_kernel(q_ref, ocmp_ref, sel_ref, g0_ref, g1_ref, g2_ref, ksvs_ref, kwvw_ref, beta_ref,
                      o_ref, m_ref, l_ref, acc_ref):
    qi = pl.program_id(1)
    rows = GROUP_HEADS * TILE
    q = q_ref[0]
    qs = jnp.concatenate([(q[:, HEAD_DIM * h:HEAD_DIM * (h + 1)] * SCALE).astype(BF16)
                          for h in range(GROUP_HEADS)], axis=0)
    sel = sel_ref[0][:, 0:SLC_BLOCK].astype(BF16)
    r_i = lax.broadcasted_iota(jnp.int32, (TILE, TILE), 0)
    c_i = lax.broadcasted_iota(jnp.int32, (TILE, TILE), 1)
    causal = jnp.where(c_i <= r_i, 0.0, NEG)
    anti = jnp.where(c_i >= r_i, 0.0, NEG)
    tile4 = lambda x: jnp.concatenate([x] * GROUP_HEADS, axis=0)

    def reset():
        m_ref[...] = jnp.full((rows, 1), NEG, F32)
        l_ref[...] = jnp.zeros((rows, 1), F32)
        acc_ref[...] = jnp.zeros((rows, HEAD_DIM), F32)

    def step(kv, bias):
        k = kv[:, 0:HEAD_DIM].astype(BF16)
        v = kv[:, HEAD_DIM:2 * HEAD_DIM].astype(BF16)
        s = _dot_nt(qs, k)
        if bias is not None:
            s = s + tile4(bias)
        m_prev = m_ref[...]
        m_new = jnp.maximum(m_prev, jnp.max(s, axis=-1, keepdims=True))
        alpha = jnp.exp(m_prev - m_new)
        p = jnp.exp(s - m_new)
        l_ref[...] = alpha * l_ref[...] + jnp.sum(p, axis=-1, keepdims=True)
        acc_ref[...] = alpha * acc_ref[...] + _dot(p.astype(BF16), v)
        m_ref[...] = m_new

    def finish():
        o = acc_ref[...] / l_ref[...]
        return jnp.concatenate([o[TILE * h:TILE * (h + 1), :] for h in range(GROUP_HEADS)], axis=1)

    def kv_tile(ref, kj):
        return ref[0, pl.ds(pl.multiple_of(kj * TILE, TILE), TILE), :]

    def sel_bias(kj):
        key_blk = 2 * kj + lax.shift_right_logical(lax.broadcasted_iota(jnp.int32, (TILE, SLC_BLOCK), 0), 6)
        blk_id = lax.broadcasted_iota(jnp.int32, (TILE, SLC_BLOCK), 1)
        expand = jnp.where(key_blk == blk_id, 1.0, 0.0).astype(BF16)
        return _dot_nt(sel, expand)

    reset()

    def slc_body(kj, carry):
        step(kv_tile(ksvs_ref, kj), sel_bias(kj))
        return carry

    lax.fori_loop(0, qi, slc_body, 0)
    step(kv_tile(ksvs_ref, qi), sel_bias(qi) + causal)
    o_slc = finish()

    reset()
    n_prev = NSA_WINDOW // TILE
    for d in range(n_prev, -1, -1):
        bias = anti if d == n_prev else (causal if d == 0 else None)

        @pl.when(qi - d >= 0)
        def _(d=d, bias=bias):
            step(kv_tile(kwvw_ref, qi - d), bias)

    o_win = finish()

    y = (_sigmoid(g0_ref[0]) * ocmp_ref[0] + _sigmoid(g1_ref[0]) * o_slc + _sigmoid(g2_ref[0]) * o_win)
    ms = jnp.mean(y * y, axis=-1, keepdims=True)
    o_ref[0] = y * lax.rsqrt(ms + RMS_EPS) * beta_ref[...]


def _nsa_flash(proj, ocmp, sel, beta):
    b, s, _ = proj.shape
    gw = GROUP_WIDTH
    col = lambda c: pl.BlockSpec((1, TILE, gw), lambda i, j: (i, j, c // gw))
    slab = lambda c: pl.BlockSpec((1, s, 2 * HEAD_DIM), lambda i, j: (i, 0, c // (2 * HEAD_DIM)))
    rows = GROUP_HEADS * TILE
    return pl.pallas_call(
        _nsa_flash_kernel,
        grid=(b, s // TILE),
        in_specs=[col(C_QN),
                  pl.BlockSpec((1, TILE, gw), lambda i, j: (i, j, 0)),
                  pl.BlockSpec((1, TILE, TILE), lambda i, j: (i, j, 0)),
                  col(C_G0), col(C_G1), col(C_G2), slab(C_KSVS), slab(C_KWVW),
                  pl.BlockSpec((1, gw), lambda i, j: (0, 0))],
        out_specs=pl.BlockSpec((1, TILE, gw), lambda i, j: (i, j, 0)),
        out_shape=jax.ShapeDtypeStruct((b, s, gw), F32),
        scratch_shapes=[pltpu.VMEM((rows, 1), F32), pltpu.VMEM((rows, 1), F32),
                        pltpu.VMEM((rows, HEAD_DIM), F32)],
        compiler_params=_cparams("parallel", "arbitrary"),
        name="nsa_flash",
    )(proj, ocmp, sel, proj, proj, proj, proj, proj, beta)


def _banded_kernel(q_ref, kc_ref, kp_ref, vc_ref, vp_ref, o_ref, lse_ref):
    has_prev = pl.program_id(2) > 0
    r_i = lax.broadcasted_iota(jnp.int32, (TILE, TILE), 0)
    c_i = lax.broadcasted_iota(jnp.int32, (TILE, TILE), 1)
    mask_cur = c_i <= r_i
    mask_prev = c_i >= r_i + jnp.where(has_prev, 0, TILE)
    outs, lses = [], []
    for h in range(GROUP_HEADS):
        sl = slice(HEAD_DIM * h, HEAD_DIM * (h + 1))
        q = (q_ref[0, :, sl] * SCALE).astype(BF16)
        s_c = jnp.where(mask_cur, _dot_nt(q, kc_ref[0, :, sl].astype(BF16)), NEG)
        s_p = jnp.where(mask_prev, _dot_nt(q, kp_ref[0, :, sl].astype(BF16)), NEG)
        m = jnp.maximum(jnp.max(s_c, axis=-1, keepdims=True), jnp.max(s_p, axis=-1, keepdims=True))
        e_c = jnp.exp(s_c - m)
        e_p = jnp.exp(s_p - m)
        den = jnp.sum(e_c, axis=-1, keepdims=True) + jnp.sum(e_p, axis=-1, keepdims=True)
        o = _dot(e_c.astype(BF16), vc_ref[0, :, sl].astype(BF16)) + _dot(e_p.astype(BF16), vp_ref[0, :, sl].astype(BF16))
        outs.append(o / den)
        lses.append(jnp.broadcast_to(m + jnp.log(den), (TILE, HEAD_DIM)))
    o_ref[0] = jnp.concatenate(outs, axis=1)
    lse_ref[0] = jnp.concatenate(lses, axis=1)


def _banded(proj, dil):
    b, s, n = proj.shape
    gw = GROUP_WIDTH
    ln = s // dil
    pr = proj.reshape(b, ln, dil * n)
    nb = n // gw
    cur = lambda c: pl.BlockSpec((1, TILE, gw), lambda i, r, t: (i, t, r * nb + c // gw))
    prev = lambda c: pl.BlockSpec((1, TILE, gw), lambda i, r, t: (i, jnp.maximum(t - 1, 0), r * nb + c // gw))
    out = pl.BlockSpec((1, TILE, gw), lambda i, r, t: (i, t, r))
    o, lse = pl.pallas_call(
        _banded_kernel,
        grid=(b, dil, ln // TILE),
        in_specs=[cur(C_QD), cur(C_KD), prev(C_KD), cur(C_VD), prev(C_VD)],
        out_specs=[out, out],
        out_shape=[jax.ShapeDtypeStruct((b, ln, dil * gw), F32)] * 2,
        compiler_params=_cparams("parallel", "parallel", "arbitrary"),
        name=f"dilated_band{dil}",
    )(pr, pr, pr, pr, pr)
    return o.reshape(b, s, gw), lse.reshape(b, s, gw)


def _rwkv_prep_kernel(cur_ref, halo_ref, mu_ref, w0_ref, wup_ref, a0_ref, aup_ref, gup_ref, kk_w_ref, ka_ref,
                      rk_ref, ones_ref,
                      kk_o, w_o, b_o, k_o, yr_o, vkr_o, v_o, g_o, c_o):
    gw = GROUP_WIDTH
    tm = cur_ref.shape[1]
    ones = ones_ref[...]
    x = cur_ref[0]
    last = halo_ref[0][7:8, :]
    last = jnp.where(pl.program_id(1) > 0, last, 0.0)
    rid = lax.broadcasted_iota(jnp.int32, (tm, 1), 0)
    xprev = jnp.where(rid == 0, last, pltpu.roll(x, 1, axis=0))
    xs = x + (xprev - x) * mu_ref[...]
    r, k, v = xs[:, 0:gw], xs[:, gw:2 * gw], xs[:, 2 * gw:3 * gw]
    wd, ad, gd = xs[:, 3 * gw:3 * gw + 64], xs[:, 3 * gw + 64:3 * gw + 128], xs[:, 3 * gw + 128:4 * gw]
    z = -(w0_ref[...] + _dot(jnp.tanh(wd).astype(BF16), wup_ref[...]))
    softplus = jnp.maximum(z, 0.0) + jnp.log1p(jnp.exp(-jnp.abs(z)))
    w = jnp.exp(-jnp.exp(-softplus - 0.5))
    a = _sigmoid(a0_ref[...] + _dot(ad.astype(BF16), aup_ref[...]))
    g = _dot(_sigmoid(gd).astype(BF16), gup_ref[...])
    kk = k * kk_w_ref[...]
    kk = kk * lax.rsqrt(_seg_sum(kk * kk, ones) + 1e-12)
    k2 = k * (1.0 + (a - 1.0) * ka_ref[...])
    b = kk * a
    br = _seg_sum(b * r, ones)
    kr = _seg_sum(k2 * r, ones)
    kk_o[0] = kk
    w_o[0] = w
    b_o[0] = b
    k_o[0] = k2
    yr_o[0] = w * r - kk * br
    vkr_o[0] = v * kr
    v_o[0] = v
    g_o[0] = g
    c_o[0] = _seg_sum(r * k2 * rk_ref[...], ones)


def _rwkv_prep(proj, p, ones, tm=256):
    b, s, _ = proj.shape
    gw = GROUP_WIDTH
    vec = lambda n: pl.BlockSpec((1, n), lambda i, j: (0, 0))
    mat = lambda a: pl.BlockSpec(a.shape, lambda i, j: (0, 0))
    out = pl.BlockSpec((1, tm, gw), lambda i, j: (i, j, 0))
    return pl.pallas_call(
        _rwkv_prep_kernel,
        grid=(b, s // tm),
        in_specs=[pl.BlockSpec((1, tm, 4 * gw), lambda i, j: (i, j, C_PC // (4 * gw))),
                  pl.BlockSpec((1, 8, 4 * gw), lambda i, j: (i, jnp.maximum(j * (tm // 8) - 1, 0), C_PC // (4 * gw))),
                  vec(4 * gw), vec(gw), mat(p["w_up"]), vec(gw), mat(p["a_up"]), mat(p["g_up"]),
                  vec(gw), vec(gw), vec(gw), mat(ones)],
        out_specs=[out] * 9,
        out_shape=[jax.ShapeDtypeStruct((b, s, gw), F32)] * 9,
        compiler_params=_cparams("parallel", "arbitrary"),
        name="rwkv_prep",
    )(proj, proj, p["mu"], p["w0"], p["w_up"], p["a0"], p["a_up"], p["g_up"], p["k_k"], p["k_a"], p["r_k"], ones)


def _rwkv_scan_kernel(kk_ref, w_ref, b_ref, k_ref, yr_ref, vkr_ref, v_ref, ones_ref, diag_ref,
                      yt_ref, state_ref, yacc_ref):
    nb, tb = kk_ref.shape[0], kk_ref.shape[1]

    @pl.when(pl.program_id(0) == 0)
    def _():
        state_ref[...] = jnp.zeros(state_ref.shape, F32)
        yacc_ref[...] = jnp.zeros(yacc_ref.shape, F32)

    ones = ones_ref[...]
    diag = diag_ref[...]
    lane = jnp.bitwise_and(lax.broadcasted_iota(jnp.int32, (HEAD_DIM, GROUP_WIDTH), 1), HEAD_DIM - 1)

    def group(c, carry):
        def step(tt, carry2):
            t = c * HEAD_DIM + tt
            here = lane == tt
            for i in range(nb):
                row = lambda ref: ref[i, pl.ds(t, 1), :]
                st = state_ref[i]
                sa = _dot((st * row(kk_ref)).astype(BF16), ones)
                vcol = _dot((row(v_ref) * diag).astype(BF16), ones)
                ycol = _dot((st * row(yr_ref) + row(vkr_ref) * diag).astype(BF16), ones)
                state_ref[i] = st * row(w_ref) - sa * row(b_ref) + vcol * row(k_ref)
                yacc_ref[i] = jnp.where(here, ycol, yacc_ref[i])
            return carry2

        lax.fori_loop(0, HEAD_DIM, step, 0)
        for i in range(nb):
            yt_ref[i, c] = yacc_ref[i]
        return carry

    lax.fori_loop(0, tb // HEAD_DIM, group, 0)


def _rwkv_scan(kk, w, b, k, yr, vkr, v, ones, diag, tb=128):
    nb, s, gw = kk.shape
    inp = pl.BlockSpec((nb, tb, gw), lambda j: (0, j, 0))
    const = lambda a: pl.BlockSpec(a.shape, lambda j: (0, 0))
    yt = pl.pallas_call(
        _rwkv_scan_kernel,
        grid=(s // tb,),
        in_specs=[inp] * 7 + [const(ones), const(diag)],
        out_specs=pl.BlockSpec((nb, tb // HEAD_DIM, HEAD_DIM, gw), lambda j: (0, j, 0, 0)),
        out_shape=jax.ShapeDtypeStruct((nb, s // HEAD_DIM, HEAD_DIM, gw), F32),
        scratch_shapes=[pltpu.VMEM((nb, HEAD_DIM, gw), F32), pltpu.VMEM((nb, HEAD_DIM, gw), F32)],
        compiler_params=_cparams("arbitrary"),
        name="rwkv_scan",
    )(kk, w, b, k, yr, vkr, v, ones, diag)
    yt = yt.reshape(nb, s // HEAD_DIM, HEAD_DIM, GROUP_HEADS, HEAD_DIM)
    return yt.transpose(0, 1, 4, 3, 2).reshape(nb, s, gw)


def _conformer_kernel(cur_ref, halo_ref, dw_ref, dwb_ref, lng_ref, lnb_ref, o_ref, g_ref):
    gw = GROUP_WIDTH
    tm = cur_ref.shape[1]
    hl = halo_ref.shape[1]
    glu = lambda x: x[:, 0:gw] * _sigmoid(x[:, gw:2 * gw])
    g_ref[0:hl, :] = jnp.where(pl.program_id(1) > 0, glu(halo_ref[0]), 0.0)
    g_ref[hl:hl + tm, :] = glu(cur_ref[0])
    acc = jnp.zeros((tm, gw), F32)
    for j in range(CONV_K):
        acc = acc + dw_ref[j:j + 1, :] * g_ref[pl.ds(hl - (CONV_K - 1) + j, tm), :]
    h = acc + dwb_ref[...]
    mu = jnp.mean(h, axis=-1, keepdims=True)
    d = h - mu
    var = jnp.mean(d * d, axis=-1, keepdims=True)
    o_ref[0] = _silu(d * lax.rsqrt(var + LN_EPS) * lng_ref[...] + lnb_ref[...])


def _conformer(proj, dw, dwb, lng, lnb, tm=512, hl=32):
    b, s, _ = proj.shape
    gw = GROUP_WIDTH
    vec = pl.BlockSpec((1, gw), lambda i, j: (0, 0))
    return pl.pallas_call(
        _conformer_kernel,
        grid=(b, s // tm),
        in_specs=[pl.BlockSpec((1, tm, 2 * gw), lambda i, j: (i, j, C_PD // (2 * gw))),
                  pl.BlockSpec((1, hl, 2 * gw), lambda i, j: (i, jnp.maximum(j * (tm // hl) - 1, 0), C_PD // (2 * gw))),
                  pl.BlockSpec((CONV_K, gw), lambda i, j: (0, 0)), vec, vec, vec],
        out_specs=pl.BlockSpec((1, tm, gw), lambda i, j: (i, j, 0)),
        out_shape=jax.ShapeDtypeStruct((b, s, gw), F32),
        scratch_shapes=[pltpu.VMEM((hl + tm, gw), F32)],
        compiler_params=_cparams("parallel", "arbitrary"),
        name="conformer_conv",
    )(proj, proj, dw, dwb, lng, lnb)


def _out_proj_kernel(x_ref, ya_ref, o1_ref, o4_ref, o16_ref, l1_ref, l4_ref, l16_ref,
                     ys_ref, c_ref, v_ref, g_ref, yd_ref, bdil_ref, lng_ref, lnb_ref, ones_ref, w_ref, o_ref):
    gw = GROUP_WIDTH
    l1, l4, l16 = l1_ref[...], l4_ref[...], l16_ref[...]
    mx = jnp.maximum(jnp.maximum(l1, l4), l16)
    e1, e4, e16 = jnp.exp(l1 - mx), jnp.exp(l4 - mx), jnp.exp(l16 - mx)
    den = e1 + e4 + e16
    yb = (e1 / den) * o1_ref[...] + (e4 / den) * o4_ref[...] + (e16 / den) * o16_ref[...]
    yb = yb * lax.rsqrt(jnp.mean(yb * yb, axis=-1, keepdims=True) + RMS_EPS) * bdil_ref[...]
    ones = ones_ref[...]
    y = ys_ref[...] + c_ref[...] * v_ref[...]
    mu = _seg_sum(y, ones) * (1.0 / HEAD_DIM)
    d = y - mu
    var = _seg_sum(d * d, ones) * (1.0 / HEAD_DIM)
    yc = (d * lax.rsqrt(var + RWKV_GN_EPS) * lng_ref[...] + lnb_ref[...]) * g_ref[...]
    acc = x_ref[...]
    for n, part in enumerate((ya_ref[...], yb, yc, yd_ref[...])):
        acc = acc + _dot(part.astype(BF16), w_ref[n * gw:(n + 1) * gw, :])
    o_ref[...] = acc


def _out_proj(x2, parts, vecs, ones, w, tm=512):
    m, d = x2.shape
    gw = GROUP_WIDTH
    blk = pl.BlockSpec((tm, gw), lambda i: (i, 0))
    vec = pl.BlockSpec((1, gw), lambda i: (0, 0))
    return pl.pallas_call(
        _out_proj_kernel,
        grid=(m // tm,),
        in_specs=[pl.BlockSpec((tm, d), lambda i: (i, 0))] + [blk] * len(parts) + [vec] * len(vecs)
                 + [pl.BlockSpec(ones.shape, lambda i: (0, 0)), pl.BlockSpec(w.shape, lambda i: (0, 0))],
        out_specs=pl.BlockSpec((tm, d), lambda i: (i, 0)),
        out_shape=jax.ShapeDtypeStruct((m, d), F32),
        compiler_params=_cparams("parallel"),
        name="out_proj",
    )(x2, *parts, *vecs, ones, w)


def _ffn_kernel(x_ref, halo_ref, gain_ref, upg_ref, upv_ref, dwg_ref, dwv_ref, bg_ref, bv_ref, down_ref, fin_ref,
                o_ref, h_ref, acc_ref, *, tiles_per_seq, final_norm):
    j = pl.program_id(1)
    tm = x_ref.shape[0]
    hl = halo_ref.shape[0]

    def norm(x):
        return (x * lax.rsqrt(jnp.mean(x * x, axis=-1, keepdims=True) + RMS_EPS) * gain_ref[...]).astype(BF16)

    @pl.when(j == 0)
    def _():
        first = pl.program_id(0) % tiles_per_seq == 0
        h_ref[0:hl, :] = jnp.where(first, jnp.zeros((hl, x_ref.shape[1]), BF16), norm(halo_ref[...]))
        h_ref[hl:hl + tm, :] = norm(x_ref[...])
        acc_ref[...] = jnp.zeros(acc_ref.shape, F32)

    h = h_ref[...]

    def conv(up_ref, dw_ref, b_ref):
        u = _dot(h, up_ref[...])
        out = b_ref[...]
        for tap in range(FFN_CONV_K):
            off = hl - (FFN_CONV_K - 1) + tap
            out = out + dw_ref[tap:tap + 1, :] * u[off:off + tm, :]
        return out

    act = _silu(conv(upg_ref, dwg_ref, bg_ref)) * conv(upv_ref, dwv_ref, bv_ref)
    acc_ref[...] += _dot(act.astype(BF16), down_ref[...])

    @pl.when(j == pl.num_programs(1) - 1)
    def _():
        y = x_ref[...] + acc_ref[...]
        if final_norm:
            y = y * lax.rsqrt(jnp.mean(y * y, axis=-1, keepdims=True) + RMS_EPS) * fin_ref[...]
        o_ref[...] = y


def _ffn(x2, gain, up, dw, dwb, down, fin, seq_len, final_norm, tm=512, tn=256, hl=16):
    m, d = x2.shape
    dff = down.shape[0]
    nj = dff // tn
    kern = functools.partial(_ffn_kernel, tiles_per_seq=seq_len // tm, final_norm=final_norm)
    return pl.pallas_call(
        kern,
        grid=(m // tm, nj),
        in_specs=[pl.BlockSpec((tm, d), lambda i, j: (i, 0)),
                  pl.BlockSpec((hl, d), lambda i, j: (jnp.maximum(i * (tm // hl) - 1, 0), 0)),
                  pl.BlockSpec((1, d), lambda i, j: (0, 0)),
                  pl.BlockSpec((d, tn), lambda i, j: (0, j)),
                  pl.BlockSpec((d, tn), lambda i, j: (0, j + nj)),
                  pl.BlockSpec((FFN_CONV_K, tn), lambda i, j: (0, j)),
                  pl.BlockSpec((FFN_CONV_K, tn), lambda i, j: (0, j + nj)),
                  pl.BlockSpec((1, tn), lambda i, j: (0, j)),
                  pl.BlockSpec((1, tn), lambda i, j: (0, j + nj)),
                  pl.BlockSpec((tn, d), lambda i, j: (j, 0)),
                  pl.BlockSpec((1, d), lambda i, j: (0, 0))],
        out_specs=pl.BlockSpec((tm, d), lambda i, j: (i, 0)),
        out_shape=jax.ShapeDtypeStruct((m, d), F32),
        scratch_shapes=[pltpu.VMEM((hl + tm, d), BF16), pltpu.VMEM((tm, d), F32)],
        compiler_params=_cparams("parallel", "arbitrary"),
        name="conv_ffn",
    )(x2, x2, gain, up, up, dw, dw, dwb, dwb, down, fin)


def _block_ones():
    i = np.arange(GROUP_WIDTH) // HEAD_DIM
    return jnp.asarray(i[:, None] == i[None, :], BF16)


def _diag_mask():
    v = np.arange(HEAD_DIM)
    l = np.arange(GROUP_WIDTH) % HEAD_DIM
    return jnp.asarray(v[:, None] == l[None, :], F32)


def _importance_map(n_cmp, n_blk):
    gt = np.zeros((n_blk, n_cmp), np.float32)
    per = SLC_BLOCK // CMP_STRIDE
    for c in range(n_blk * per):
        for i in (c - 1, c):
            if 0 <= i < n_cmp - 1:
                gt[c // per, i] += 1.0
    return jnp.asarray(gt, BF16)


def kernel(x, norm_mix, w_in, cmp_pos, cmp_k_w1, cmp_k_w2, cmp_v_w1, cmp_v_w2, beta_nsa, beta_dil, rwkv_mu, rwkv_w0, rwkv_w_up, rwkv_a0, rwkv_a_up, rwkv_g_up, rwkv_k_k, rwkv_k_a, rwkv_r_k, rwkv_ln_g, rwkv_ln_b, conv_dw, conv_dw_b, conv_ln_g, conv_ln_b, w_out, norm_ffn, ffn_up, ffn_dw, ffn_dw_b, ffn_down, norm_final):
    bsz, seq, d = x.shape
    depth = w_in.shape[0]
    gw = GROUP_WIDTH
    n_cmp = seq // CMP_STRIDE
    n_blk = seq // SLC_BLOCK
    half = CMP_STRIDE * HEAD_DIM

    cols = _proj_columns()
    w_in_p = jnp.where(jnp.asarray(cols >= 0)[None, None, :], w_in[:, :, np.maximum(cols, 0)], 0.0).astype(BF16)
    w_out_b = w_out.astype(BF16)
    ffn_up_b = ffn_up.astype(BF16)
    ffn_down_b = ffn_down.astype(BF16)
    ones = _block_ones()
    diag = _diag_mask()
    gt = _importance_map(n_cmp, n_blk)
    row = lambda a: a.reshape(1, -1)

    x2 = x.reshape(bsz * seq, d)
    for l in range(depth):
        proj = _proj_in(x2, row(norm_mix[l]), w_in_p[l]).reshape(bsz, seq, N_PROJ)

        kh = proj[:, :, C_KCVC:C_KCVC + HEAD_DIM].reshape(bsz, n_cmp, half)
        vh = proj[:, :, C_KCVC + HEAD_DIM:C_KCVC + 2 * HEAD_DIM].reshape(bsz, n_cmp, half)
        pos = cmp_pos[l].reshape(1, 2 * half)
        k_cmp, v_cmp = _compress(kh, vh, pos[:, :half], pos[:, half:],
                                 cmp_k_w1[l].astype(BF16), cmp_k_w2[l].astype(BF16),
                                 cmp_v_w1[l].astype(BF16), cmp_v_w2[l].astype(BF16))
        ocmp, sel = _nsa_cmp(proj, k_cmp, v_cmp, gt)
        y_a = _nsa_flash(proj, ocmp, sel, row(beta_nsa[l]))

        dil_o, dil_l = zip(*[_banded(proj, dil) for dil in DILATIONS])

        rp = dict(mu=row(rwkv_mu[l]), w0=row(rwkv_w0[l]), w_up=rwkv_w_up[l].astype(BF16), a0=row(rwkv_a0[l]),
                  a_up=rwkv_a_up[l].astype(BF16), g_up=rwkv_g_up[l].astype(BF16), k_k=row(rwkv_k_k[l]),
                  k_a=row(rwkv_k_a[l]), r_k=row(rwkv_r_k[l]))
        kk, w, b, k2, yr, vkr, v, g, c = _rwkv_prep(proj, rp, ones)
        y_scan = _rwkv_scan(kk, w, b, k2, yr, vkr, v, ones, diag)

        y_d = _conformer(proj, conv_dw[l], row(conv_dw_b[l]), row(conv_ln_g[l]), row(conv_ln_b[l]))

        flat = lambda a: a.reshape(bsz * seq, gw)
        parts = [flat(y_a)] + [flat(a) for a in dil_o] + [flat(a) for a in dil_l] \
            + [flat(y_scan), flat(c), flat(v), flat(g), flat(y_d)]
        vecs = [row(beta_dil[l]), row(rwkv_ln_g[l]), row(rwkv_ln_b[l])]
        x2 = _out_proj(x2, parts, vecs, ones, w_out_b[l])

        x2 = _ffn(x2, row(norm_ffn[l]), ffn_up_b[l], ffn_dw[l], row(ffn_dw_b[l]), ffn_down_b[l],
                  row(norm_final), seq, final_norm=(l == depth - 1))
    return x2.reshape(bsz, seq, d)
```

```python
import functools

import jax
import jax.numpy as jnp
import numpy as np
from jax import lax
from jax.experimental import pallas as pl
from jax.experimental.pallas import tpu as pltpu

F32 = jnp.float32
BF16 = jnp.bfloat16

HEAD_DIM = 64
GROUP_HEADS = 4
GROUP_WIDTH = HEAD_DIM * GROUP_HEADS
CMP_STRIDE = 16
CMP_BLOCK = 32
SLC_BLOCK = 64
N_SEL_BLOCKS = 16
NSA_WINDOW = 512
DILATIONS = (1, 4, 16)
TILE = 128
CONV_K = 31
FFN_CONV_K = 3
RMS_EPS = 1e-6
LN_EPS = 1e-5
RWKV_GN_EPS = 64e-5
NEG = -1e30
FORCE_SCORE = 1e9
SCALE = HEAD_DIM ** -0.5
LOG2_E = 1.4426950408889634
SLC_CHUNK = 4
V_ROWS = HEAD_DIM + 16

C_PC, C_PD, C_QN, C_G0, C_G1, C_G2 = 0, 1024, 1536, 1792, 2048, 2304
C_QD, C_KD, C_VD, C_KCVC, C_KSVS, C_KWVW = 2560, 2816, 3072, 3328, 3456, 3584
N_PROJ = 3840

VMEM_LIMIT = 48 * 1024 * 1024


def _cparams(*sem):
    return pltpu.CompilerParams(dimension_semantics=sem, vmem_limit_bytes=VMEM_LIMIT)


def _proj_columns():
    gw, hd = GROUP_WIDTH, HEAD_DIM
    a0 = 0
    q = np.arange(a0, a0 + gw)
    kc, vc, ks, vs, kw, vw = [np.arange(a0 + gw + i * hd, a0 + gw + (i + 1) * hd) for i in range(6)]
    gate0 = a0 + gw + 6 * hd
    gates = [np.repeat(gate0 + 3 * np.arange(GROUP_HEADS) + j, hd) for j in range(3)]
    b0 = gate0 + 3 * GROUP_HEADS
    qd, kd, vd = [np.arange(b0 + i * gw, b0 + (i + 1) * gw) for i in range(3)]
    c0 = b0 + 3 * gw
    pc = np.arange(c0, c0 + 4 * gw)
    d0 = c0 + 4 * gw
    pd = np.arange(d0, d0 + 2 * gw)
    cols = np.concatenate([pc, pd, q, gates[0], gates[1], gates[2], qd, kd, vd, kc, vc, ks, vs, kw, vw,
                           np.full(N_PROJ - C_KWVW - 2 * hd, -1)])
    assert cols.shape[0] == N_PROJ
    return cols


def _dot_nt(a, b):
    return lax.dot_general(a, b, (((1,), (1,)), ((), ())), preferred_element_type=F32)


def _dot(a, b):
    return jnp.dot(a, b, preferred_element_type=F32)


def _seg_sum(x, ones_bf16):
    hi = x.astype(BF16)
    lo = (x - hi.astype(F32)).astype(BF16)
    return _dot(hi, ones_bf16) + _dot(lo, ones_bf16)


def _sigmoid(x):
    return 1.0 / (1.0 + jnp.exp(-x))


def _silu(x):
    return x * _sigmoid(x)


def _proj_in_kernel(x_ref, g_ref, w_ref, o_ref, h_ref):
    @pl.when(pl.program_id(1) == 0)
    def _():
        x = x_ref[...]
        ms = jnp.mean(x * x, axis=-1, keepdims=True)
        h_ref[...] = (x * lax.rsqrt(ms + RMS_EPS) * g_ref[...]).astype(BF16)

    o_ref[...] = _dot(h_ref[...], w_ref[...])


def _proj_in(x2, gain, w, tm=512, tn=768):
    m, d = x2.shape
    n = w.shape[1]
    return pl.pallas_call(
        _proj_in_kernel,
        grid=(m // tm, n // tn),
        in_specs=[pl.BlockSpec((tm, d), lambda i, j: (i, 0)),
                  pl.BlockSpec((1, d), lambda i, j: (0, 0)),
                  pl.BlockSpec((d, tn), lambda i, j: (0, j))],
        out_specs=pl.BlockSpec((tm, tn), lambda i, j: (i, j)),
        out_shape=jax.ShapeDtypeStruct((m, n), F32),
        scratch_shapes=[pltpu.VMEM((tm, d), BF16)],
        compiler_params=_cparams("parallel", "arbitrary"),
        name="proj_in",
    )(x2, gain, w)


def _compress_kernel(kh_ref, vh_ref, plo_ref, phi_ref, kw1_ref, kw2_ref, vw1_ref, vw2_ref, ko_ref, vo_ref):
    half = kh_ref.shape[2]
    n = kh_ref.shape[1]

    def mlp(xh, w1_ref, w2_ref):
        a = _dot((xh + plo_ref[...]).astype(BF16), w1_ref[0:half, :])
        bm = _dot((xh + phi_ref[...]).astype(BF16), w1_ref[half:2 * half, :])
        hid = a + pltpu.roll(bm, n - 1, axis=0)
        return _dot(_silu(hid).astype(BF16), w2_ref[...])

    ko_ref[0] = mlp(kh_ref[0], kw1_ref, kw2_ref)
    vo_ref[0] = mlp(vh_ref[0], vw1_ref, vw2_ref)


def _compress(kh, vh, pos_lo, pos_hi, kw1, kw2, vw1, vw2):
    b, n, half = kh.shape
    hid = kw1.shape[1]
    full = lambda shape: pl.BlockSpec(shape, lambda i: (0,) * len(shape))
    return pl.pallas_call(
        _compress_kernel,
        grid=(b,),
        in_specs=[pl.BlockSpec((1, n, half), lambda i: (i, 0, 0)),
                  pl.BlockSpec((1, n, half), lambda i: (i, 0, 0)),
                  full((1, half)), full((1, half)),
                  full((2 * half, hid)), full((hid, HEAD_DIM)),
                  full((2 * half, hid)), full((hid, HEAD_DIM))],
        out_specs=[pl.BlockSpec((1, n, HEAD_DIM), lambda i: (i, 0, 0))] * 2,
        out_shape=[jax.ShapeDtypeStruct((b, n, HEAD_DIM), F32)] * 2,
        compiler_params=_cparams("parallel"),
        name="nsa_compress",
    )(kh, vh, pos_lo, pos_hi, kw1, kw2, vw1, vw2)


def _nsa_cmp_kernel(q_ref, kc_ref, vct_ref, gt_ref, ocmp_ref, sel_ref):
    t0 = pl.program_id(1) * TILE
    n_cmp = kc_ref.shape[1]
    n_blk = gt_ref.shape[0]
    q_t = (q_ref[0] * SCALE).T.astype(BF16)
    kc = kc_ref[0].astype(BF16)
    vc_t = vct_ref[0].astype(BF16)
    ci = lax.broadcasted_iota(jnp.int32, (n_cmp, TILE), 0)
    tq = t0 + lax.broadcasted_iota(jnp.int32, (n_cmp, TILE), 1)
    cmask = jnp.where(ci < n_cmp - 1, CMP_STRIDE * ci + CMP_BLOCK - 1, 1 << 30) <= tq
    imp_t = jnp.zeros((n_cmp, TILE), F32)
    outs = []
    for h in range(GROUP_HEADS):
        s = jnp.where(cmask, _dot(kc, q_t[HEAD_DIM * h:HEAD_DIM * (h + 1), :]), NEG)
        e = jnp.exp(s - jnp.max(s, axis=0, keepdims=True))
        p = jnp.where(cmask, e / jnp.sum(e, axis=0, keepdims=True), 0.0)
        imp_t = imp_t + p
        outs.append(_dot(vc_t, p.astype(BF16)))
    ocmp_ref[0] = jnp.concatenate(outs, axis=0).T

    hi = imp_t.astype(BF16)
    lo = (imp_t - hi.astype(F32)).astype(BF16)
    blk = _dot(gt_ref[...], hi) + _dot(gt_ref[...], lo)
    j = lax.broadcasted_iota(jnp.int32, (n_blk, TILE), 0)
    t = t0 + lax.broadcasted_iota(jnp.int32, (n_blk, TILE), 1)
    cur = lax.shift_right_logical(t, 6)
    forced = jnp.where(j == 0, 1, jnp.where(j == cur, 1, jnp.where(j == cur - 1, 1, 0)))
    score = jnp.where(j * SLC_BLOCK <= t, jnp.where(forced == 1, FORCE_SCORE, blk), NEG)
    rank = jnp.zeros((n_blk, TILE), jnp.int32)
    for jp in range(n_blk):
        row = score[jp:jp + 1, :]
        tie = jnp.where(j > jp, 1, 0)
        rank = rank + jnp.where(row > score, 1, jnp.where(row == score, tie, 0))
    sel_ref[0] = jnp.where(rank < N_SEL_BLOCKS, 0.0, NEG)


def _nsa_cmp(proj, k_cmp, v_cmp_t, gt):
    b, s, _ = proj.shape
    n_cmp = k_cmp.shape[1]
    n_blk = gt.shape[0]
    return pl.pallas_call(
        _nsa_cmp_kernel,
        grid=(b, s // TILE),
        in_specs=[pl.BlockSpec((1, TILE, GROUP_WIDTH), lambda i, j: (i, j, C_QN // GROUP_WIDTH)),
                  pl.BlockSpec((1, n_cmp, HEAD_DIM), lambda i, j: (i, 0, 0)),
                  pl.BlockSpec((1, HEAD_DIM, n_cmp), lambda i, j: (i, 0, 0)),
                  pl.BlockSpec(gt.shape, lambda i, j: (0, 0))],
        out_specs=[pl.BlockSpec((1, TILE, GROUP_WIDTH), lambda i, j: (i, j, 0)),
                   pl.BlockSpec((1, n_blk, TILE), lambda i, j: (i, 0, j))],
        out_shape=[jax.ShapeDtypeStruct((b, s, GROUP_WIDTH), F32),
                   jax.ShapeDtypeStruct((b, n_blk, s), F32)],
        compiler_params=_cparams("parallel", "parallel"),
        name="nsa_cmp",
    )(proj, k_cmp, v_cmp_t, gt)


def _nsa_flash_kernel(q_ref, ocmp_ref, sel_ref, g0_ref, g1_ref, g2_ref, ks_ref, vs_ref, kw_ref, vw_ref, beta_ref,
                      o_ref):
    qi = pl.program_id(1)
    cols = GROUP_HEADS * TILE
    q_t = (q_ref[0] * (SCALE * LOG2_E)).T
    qs_t = jnp.concatenate([q_t[HEAD_DIM * h:HEAD_DIM * (h + 1), :] for h in range(GROUP_HEADS)], axis=1)
    sel4 = jnp.concatenate([sel_ref[0]] * GROUP_HEADS, axis=1)
    w = jnp.concatenate([qs_t, sel4], axis=0).astype(BF16)
    q_pos = qi * TILE + jnp.bitwise_and(lax.broadcasted_iota(jnp.int32, (1, cols), 1), TILE - 1)

    def keys(ref, start, n):
        return ref[0, pl.ds(start, n)].reshape(n * TILE, 2 * HEAD_DIM)

    def values_t(ref, start, n):
        v = ref[0, pl.ds(start, n)]
        return jnp.concatenate([v[u] for u in range(n)], axis=1)

    def band_bias(start, n, window):
        key_pos = start * TILE + lax.broadcasted_iota(jnp.int32, (n * TILE, cols), 0)
        dist = q_pos - key_pos
        ok = jnp.where(dist >= 0, 1, 0) if window is None else jnp.where(dist >= 0, jnp.where(dist <= window, 1, 0), 0)
        return jnp.where(ok == 1, 0.0, NEG)

    def col_max(s):
        return jnp.max(s, axis=0, keepdims=True)

    def weighted_values(k, v_t, m, bias):
        s = _dot(k, w)
        if bias is not None:
            s = s + bias
        return _dot(v_t, jnp.exp2(s - m).astype(BF16))

    def finish(acc):
        o_t = acc[0:HEAD_DIM, :] / acc[HEAD_DIM:HEAD_DIM + 1, :]
        return jnp.concatenate([o_t[:, TILE * h:TILE * (h + 1)] for h in range(GROUP_HEADS)], axis=0).T

    n_full = qi // SLC_CHUNK
    last = n_full * SLC_CHUNK
    last_bias = band_bias(last, SLC_CHUNK, None)
    m = lax.fori_loop(0, n_full, lambda c, m: jnp.maximum(m, col_max(_dot(keys(ks_ref, c * SLC_CHUNK, SLC_CHUNK), w))),
                      col_max(_dot(keys(ks_ref, last, SLC_CHUNK), w) + last_bias))
    acc = lax.fori_loop(
        0, n_full,
        lambda c, acc: acc + weighted_values(keys(ks_ref, c * SLC_CHUNK, SLC_CHUNK),
                                             values_t(vs_ref, c * SLC_CHUNK, SLC_CHUNK), m, None),
        weighted_values(keys(ks_ref, last, SLC_CHUNK), values_t(vs_ref, last, SLC_CHUNK), m, last_bias))
    o_slc = finish(acc)

    n_win = NSA_WINDOW // TILE + 1
    first = jnp.maximum(qi - (n_win - 1), 0)
    win_bias = band_bias(first, n_win, NSA_WINDOW)
    k_win = keys(kw_ref, first, n_win)
    m = col_max(_dot(k_win, w) + win_bias)
    o_win = finish(weighted_values(k_win, values_t(vw_ref, first, n_win), m, win_bias))

    y = (_sigmoid(g0_ref[0]) * ocmp_ref[0] + _sigmoid(g1_ref[0]) * o_slc + _sigmoid(g2_ref[0]) * o_win)
    ms = jnp.mean(y * y, axis=-1, keepdims=True)
    o_ref[0] = y * lax.rsqrt(ms + RMS_EPS) * beta_ref[...]


def _key_tiles(k, onehot):
    b, s, _ = k.shape
    aug = jnp.zeros((b, s, HEAD_DIM), BF16) if onehot is None else jnp.broadcast_to(onehot, (b, s, HEAD_DIM))
    return jnp.concatenate([k.astype(BF16), aug], axis=-1).reshape(b, s // TILE, TILE, 2 * HEAD_DIM)


def _value_tiles_t(v):
    b, s, _ = v.shape
    v_t = v.astype(BF16).reshape(b, s // TILE, TILE, HEAD_DIM).transpose(0, 1, 3, 2)
    pad = jnp.zeros((b, s // TILE, V_ROWS - HEAD_DIM, TILE), BF16).at[:, :, 0, :].set(1.0)
    return jnp.concatenate([v_t, pad], axis=2)


def _nsa_flash(proj, ocmp, sel_t, beta):
    b, s, _ = proj.shape
    gw = GROUP_WIDTH
    n_blk = sel_t.shape[1]
    assert n_blk == HEAD_DIM, "the selection rows share the 128-deep contraction with the 64 head dims"
    piece = lambda c: proj[:, :, c:c + HEAD_DIM]
    blk_of_key = np.arange(s)[:, None] // SLC_BLOCK == np.arange(n_blk)[None, :]
    ks = _key_tiles(piece(C_KSVS), jnp.asarray(blk_of_key, BF16))
    kw = _key_tiles(piece(C_KWVW), None)
    vs = _value_tiles_t(piece(C_KSVS + HEAD_DIM))
    vw = _value_tiles_t(piece(C_KWVW + HEAD_DIM))
    nt = s // TILE
    col = lambda c: pl.BlockSpec((1, TILE, gw), lambda i, j: (i, j, c // gw))
    ktile = pl.BlockSpec((1, nt, TILE, 2 * HEAD_DIM), lambda i, j: (i, 0, 0, 0))
    vtile = pl.BlockSpec((1, nt, V_ROWS, TILE), lambda i, j: (i, 0, 0, 0))
    assert nt % SLC_CHUNK == 0 and nt >= NSA_WINDOW // TILE + 1
    return pl.pallas_call(
        _nsa_flash_kernel,
        grid=(b, nt),
        in_specs=[col(C_QN),
                  pl.BlockSpec((1, TILE, gw), lambda i, j: (i, j, 0)),
                  pl.BlockSpec((1, n_blk, TILE), lambda i, j: (i, 0, j)),
                  col(C_G0), col(C_G1), col(C_G2), ktile, vtile, ktile, vtile,
                  pl.BlockSpec((1, gw), lambda i, j: (0, 0))],
        out_specs=pl.BlockSpec((1, TILE, gw), lambda i, j: (i, j, 0)),
        out_shape=jax.ShapeDtypeStruct((b, s, gw), F32),
        compiler_params=_cparams("parallel", "arbitrary"),
        name="nsa_flash",
    )(proj, ocmp, sel_t, proj, proj, proj, ks, vs, kw, vw, beta)


def _banded_kernel(q_ref, kc_ref, kp_ref, vc_ref, vp_ref, o_ref, lse_ref):
    has_prev = pl.program_id(2) > 0
    r_i = lax.broadcasted_iota(jnp.int32, (TILE, TILE), 0)
    c_i = lax.broadcasted_iota(jnp.int32, (TILE, TILE), 1)
    mask_cur = c_i <= r_i
    mask_prev = c_i >= r_i + jnp.where(has_prev, 0, TILE)
    outs, lses = [], []
    for h in range(GROUP_HEADS):
        sl = slice(HEAD_DIM * h, HEAD_DIM * (h + 1))
        q = (q_ref[0, :, sl] * SCALE).astype(BF16)
        s_c = jnp.where(mask_cur, _dot_nt(q, kc_ref[0, :, sl].astype(BF16)), NEG)
        s_p = jnp.where(mask_prev, _dot_nt(q, kp_ref[0, :, sl].astype(BF16)), NEG)
        m = jnp.maximum(jnp.max(s_c, axis=-1, keepdims=True), jnp.max(s_p, axis=-1, keepdims=True))
        e_c = jnp.exp(s_c - m)
        e_p = jnp.exp(s_p - m)
        den = jnp.sum(e_c, axis=-1, keepdims=True) + jnp.sum(e_p, axis=-1, keepdims=True)
        o = _dot(e_c.astype(BF16), vc_ref[0, :, sl].astype(BF16)) + _dot(e_p.astype(BF16), vp_ref[0, :, sl].astype(BF16))
        outs.append(o / den)
        lses.append(jnp.broadcast_to(m + jnp.log(den), (TILE, HEAD_DIM)))
    o_ref[0] = jnp.concatenate(outs, axis=1)
    lse_ref[0] = jnp.concatenate(lses, axis=1)


def _banded(proj, dil):
    b, s, n = proj.shape
    gw = GROUP_WIDTH
    ln = s // dil
    if dil == 1:
        pr, nb, base = proj, n // gw, C_QD
    else:
        pr, nb, base = proj[:, :, C_QD:C_QD + 3 * gw].reshape(b, ln, dil * 3 * gw), 3, 0
    cur = lambda c: pl.BlockSpec((1, TILE, gw), lambda i, r, t: (i, t, r * nb + (c - C_QD + base) // gw))
    prev = lambda c: pl.BlockSpec((1, TILE, gw),
                                  lambda i, r, t: (i, jnp.maximum(t - 1, 0), r * nb + (c - C_QD + base) // gw))
    out = pl.BlockSpec((1, TILE, gw), lambda i, r, t: (i, t, r))
    o, lse = pl.pallas_call(
        _banded_kernel,
        grid=(b, dil, ln // TILE),
        in_specs=[cur(C_QD), cur(C_KD), prev(C_KD), cur(C_VD), prev(C_VD)],
        out_specs=[out, out],
        out_shape=[jax.ShapeDtypeStruct((b, ln, dil * gw), F32)] * 2,
        compiler_params=_cparams("parallel", "parallel", "arbitrary"),
        name=f"dilated_band{dil}",
    )(pr, pr, pr, pr, pr)
    return o.reshape(b, s, gw), lse.reshape(b, s, gw)


def _rwkv_prep_kernel(cur_ref, halo_ref, mu_ref, w0_ref, wup_ref, a0_ref, aup_ref, gup_ref, kk_w_ref, ka_ref,
                      rk_ref, ones_ref,
                      kk_o, w_o, b_o, k_o, yr_o, vkr_o, v_o, g_o, c_o):
    gw = GROUP_WIDTH
    tm = cur_ref.shape[1]
    ones = ones_ref[...]
    x = cur_ref[0]
    last = halo_ref[0][7:8, :]
    last = jnp.where(pl.program_id(1) > 0, last, 0.0)
    rid = lax.broadcasted_iota(jnp.int32, (tm, 1), 0)
    xprev = jnp.where(rid == 0, last, pltpu.roll(x, 1, axis=0))
    xs = x + (xprev - x) * mu_ref[...]
    r, k, v = xs[:, 0:gw], xs[:, gw:2 * gw], xs[:, 2 * gw:3 * gw]
    wd, ad, gd = xs[:, 3 * gw:3 * gw + 64], xs[:, 3 * gw + 64:3 * gw + 128], xs[:, 3 * gw + 128:4 * gw]
    z = -(w0_ref[...] + _dot(jnp.tanh(wd).astype(BF16), wup_ref[...]))
    softplus = jnp.maximum(z, 0.0) + jnp.log1p(jnp.exp(-jnp.abs(z)))
    w = jnp.exp(-jnp.exp(-softplus - 0.5))
    a = _sigmoid(a0_ref[...] + _dot(ad.astype(BF16), aup_ref[...]))
    g = _dot(_sigmoid(gd).astype(BF16), gup_ref[...])
    kk = k * kk_w_ref[...]
    kk = kk * lax.rsqrt(_seg_sum(kk * kk, ones) + 1e-12)
    k2 = k * (1.0 + (a - 1.0) * ka_ref[...])
    b = kk * a
    br = _seg_sum(b * r, ones)
    kr = _seg_sum(k2 * r, ones)
    kk_o[0] = kk
    w_o[0] = w
    b_o[0] = b
    k_o[0] = k2
    yr_o[0] = w * r - kk * br
    vkr_o[0] = v * kr
    v_o[0] = v
    g_o[0] = g
    c_o[0] = _seg_sum(r * k2 * rk_ref[...], ones)


def _rwkv_prep(proj, p, ones, tm=256):
    b, s, _ = proj.shape
    gw = GROUP_WIDTH
    vec = lambda n: pl.BlockSpec((1, n), lambda i, j: (0, 0))
    mat = lambda a: pl.BlockSpec(a.shape, lambda i, j: (0, 0))
    out = pl.BlockSpec((1, tm, gw), lambda i, j: (i, j, 0))
    return pl.pallas_call(
        _rwkv_prep_kernel,
        grid=(b, s // tm),
        in_specs=[pl.BlockSpec((1, tm, 4 * gw), lambda i, j: (i, j, C_PC // (4 * gw))),
                  pl.BlockSpec((1, 8, 4 * gw), lambda i, j: (i, jnp.maximum(j * (tm // 8) - 1, 0), C_PC // (4 * gw))),
                  vec(4 * gw), vec(gw), mat(p["w_up"]), vec(gw), mat(p["a_up"]), mat(p["g_up"]),
                  vec(gw), vec(gw), vec(gw), mat(ones)],
        out_specs=[out] * 9,
        out_shape=[jax.ShapeDtypeStruct((b, s, gw), F32)] * 9,
        compiler_params=_cparams("parallel", "arbitrary"),
        name="rwkv_prep",
    )(proj, proj, p["mu"], p["w0"], p["w_up"], p["a0"], p["a_up"], p["g_up"], p["k_k"], p["k_a"], p["r_k"], ones)


def _rwkv_scan_kernel(kk_ref, w_ref, b_ref, k_ref, yr_ref, vkr_ref, v_ref, ones_ref, diag_ref,
                      yt_ref, state_ref, yacc_ref):
    nb, tb = kk_ref.shape[0], kk_ref.shape[1]

    @pl.when(pl.program_id(0) == 0)
    def _():
        state_ref[...] = jnp.zeros(state_ref.shape, F32)
        yacc_ref[...] = jnp.zeros(yacc_ref.shape, F32)

    ones = ones_ref[...]
    diag = diag_ref[...]
    lane = jnp.bitwise_and(lax.broadcasted_iota(jnp.int32, (HEAD_DIM, GROUP_WIDTH), 1), HEAD_DIM - 1)

    def group(c, carry):
        def step(tt, carry2):
            t = c * HEAD_DIM + tt
            here = lane == tt
            row = lambda ref, i: ref[i, pl.ds(t, 1), :]
            st = [state_ref[i] for i in range(nb)]
            lhs = ([(st[i] * row(kk_ref, i)).astype(BF16) for i in range(nb)]
                   + [(row(v_ref, i) * diag).astype(BF16) for i in range(nb)]
                   + [(st[i] * row(yr_ref, i) + row(vkr_ref, i) * diag).astype(BF16) for i in range(nb)])
            red = _dot(jnp.concatenate(lhs, axis=0), ones)
            part = lambda n: red[n * HEAD_DIM:(n + 1) * HEAD_DIM, :]
            for i in range(nb):
                sa, vcol, ycol = part(i), part(nb + i), part(2 * nb + i)
                state_ref[i] = st[i] * row(w_ref, i) - sa * row(b_ref, i) + vcol * row(k_ref, i)
                yacc_ref[i] = jnp.where(here, ycol, yacc_ref[i])
            return carry2

        lax.fori_loop(0, HEAD_DIM, step, 0)
        for i in range(nb):
            yt_ref[i, c] = yacc_ref[i]
        return carry

    lax.fori_loop(0, tb // HEAD_DIM, group, 0)


def _rwkv_scan(kk, w, b, k, yr, vkr, v, ones, diag, tb=128):
    nb, s, gw = kk.shape
    inp = pl.BlockSpec((nb, tb, gw), lambda j: (0, j, 0))
    const = lambda a: pl.BlockSpec(a.shape, lambda j: (0, 0))
    yt = pl.pallas_call(
        _rwkv_scan_kernel,
        grid=(s // tb,),
        in_specs=[inp] * 7 + [const(ones), const(diag)],
        out_specs=pl.BlockSpec((nb, tb // HEAD_DIM, HEAD_DIM, gw), lambda j: (0, j, 0, 0)),
        out_shape=jax.ShapeDtypeStruct((nb, s // HEAD_DIM, HEAD_DIM, gw), F32),
        scratch_shapes=[pltpu.VMEM((nb, HEAD_DIM, gw), F32), pltpu.VMEM((nb, HEAD_DIM, gw), F32)],
        compiler_params=_cparams("arbitrary"),
        name="rwkv_scan",
    )(kk, w, b, k, yr, vkr, v, ones, diag)
    yt = yt.reshape(nb, s // HEAD_DIM, HEAD_DIM, GROUP_HEADS, HEAD_DIM)
    return yt.transpose(0, 1, 4, 3, 2).reshape(nb, s, gw)


def _conformer_kernel(cur_ref, halo_ref, dw_ref, dwb_ref, lng_ref, lnb_ref, o_ref, g_ref):
    gw = GROUP_WIDTH
    tm = cur_ref.shape[1]
    hl = halo_ref.shape[1]
    glu = lambda x: x[:, 0:gw] * _sigmoid(x[:, gw:2 * gw])
    g_ref[0:hl, :] = jnp.where(pl.program_id(1) > 0, glu(halo_ref[0]), 0.0)
    g_ref[hl:hl + tm, :] = glu(cur_ref[0])
    acc = jnp.zeros((tm, gw), F32)
    for j in range(CONV_K):
        acc = acc + dw_ref[j:j + 1, :] * g_ref[pl.ds(hl - (CONV_K - 1) + j, tm), :]
    h = acc + dwb_ref[...]
    mu = jnp.mean(h, axis=-1, keepdims=True)
    d = h - mu
    var = jnp.mean(d * d, axis=-1, keepdims=True)
    o_ref[0] = _silu(d * lax.rsqrt(var + LN_EPS) * lng_ref[...] + lnb_ref[...])


def _conformer(proj, dw, dwb, lng, lnb, tm=512, hl=32):
    b, s, _ = proj.shape
    gw = GROUP_WIDTH
    vec = pl.BlockSpec((1, gw), lambda i, j: (0, 0))
    return pl.pallas_call(
        _conformer_kernel,
        grid=(b, s // tm),
        in_specs=[pl.BlockSpec((1, tm, 2 * gw), lambda i, j: (i, j, C_PD // (2 * gw))),
                  pl.BlockSpec((1, hl, 2 * gw), lambda i, j: (i, jnp.maximum(j * (tm // hl) - 1, 0), C_PD // (2 * gw))),
                  pl.BlockSpec((CONV_K, gw), lambda i, j: (0, 0)), vec, vec, vec],
        out_specs=pl.BlockSpec((1, tm, gw), lambda i, j: (i, j, 0)),
        out_shape=jax.ShapeDtypeStruct((b, s, gw), F32),
        scratch_shapes=[pltpu.VMEM((hl + tm, gw), F32)],
        compiler_params=_cparams("parallel", "arbitrary"),
        name="conformer_conv",
    )(proj, proj, dw, dwb, lng, lnb)


def _out_proj_kernel(x_ref, ya_ref, o1_ref, o4_ref, o16_ref, l1_ref, l4_ref, l16_ref,
                     ys_ref, c_ref, v_ref, g_ref, yd_ref, bdil_ref, lng_ref, lnb_ref, ones_ref, w_ref, o_ref):
    gw = GROUP_WIDTH
    l1, l4, l16 = l1_ref[...], l4_ref[...], l16_ref[...]
    mx = jnp.maximum(jnp.maximum(l1, l4), l16)
    e1, e4, e16 = jnp.exp(l1 - mx), jnp.exp(l4 - mx), jnp.exp(l16 - mx)
    den = e1 + e4 + e16
    yb = (e1 / den) * o1_ref[...] + (e4 / den) * o4_ref[...] + (e16 / den) * o16_ref[...]
    yb = yb * lax.rsqrt(jnp.mean(yb * yb, axis=-1, keepdims=True) + RMS_EPS) * bdil_ref[...]
    ones = ones_ref[...]
    y = ys_ref[...] + c_ref[...] * v_ref[...]
    mu = _seg_sum(y, ones) * (1.0 / HEAD_DIM)
    d = y - mu
    var = _seg_sum(d * d, ones) * (1.0 / HEAD_DIM)
    yc = (d * lax.rsqrt(var + RWKV_GN_EPS) * lng_ref[...] + lnb_ref[...]) * g_ref[...]
    acc = x_ref[...]
    for n, part in enumerate((ya_ref[...], yb, yc, yd_ref[...])):
        acc = acc + _dot(part.astype(BF16), w_ref[n * gw:(n + 1) * gw, :])
    o_ref[...] = acc


def _out_proj(x2, parts, vecs, ones, w, tm=512):
    m, d = x2.shape
    gw = GROUP_WIDTH
    blk = pl.BlockSpec((tm, gw), lambda i: (i, 0))
    vec = pl.BlockSpec((1, gw), lambda i: (0, 0))
    return pl.pallas_call(
        _out_proj_kernel,
        grid=(m // tm,),
        in_specs=[pl.BlockSpec((tm, d), lambda i: (i, 0))] + [blk] * len(parts) + [vec] * len(vecs)
                 + [pl.BlockSpec(ones.shape, lambda i: (0, 0)), pl.BlockSpec(w.shape, lambda i: (0, 0))],
        out_specs=pl.BlockSpec((tm, d), lambda i: (i, 0)),
        out_shape=jax.ShapeDtypeStruct((m, d), F32),
        compiler_params=_cparams("parallel"),
        name="out_proj",
    )(x2, *parts, *vecs, ones, w)


def _ffn_kernel(x_ref, halo_ref, gain_ref, upg_ref, upv_ref, dwg_ref, dwv_ref, bg_ref, bv_ref, down_ref, fin_ref,
                o_ref, h_ref, acc_ref, *, tiles_per_seq, final_norm):
    j = pl.program_id(1)
    tm = x_ref.shape[0]
    hl = halo_ref.shape[0]

    def norm(x):
        return (x * lax.rsqrt(jnp.mean(x * x, axis=-1, keepdims=True) + RMS_EPS) * gain_ref[...]).astype(BF16)

    @pl.when(j == 0)
    def _():
        first = pl.program_id(0) % tiles_per_seq == 0
        h_ref[0:hl, :] = jnp.where(first, jnp.zeros((hl, x_ref.shape[1]), BF16), norm(halo_ref[...]))
        h_ref[hl:hl + tm, :] = norm(x_ref[...])
        acc_ref[...] = jnp.zeros(acc_ref.shape, F32)

    h = h_ref[...]

    def conv(up_ref, dw_ref, b_ref):
        u = _dot(h, up_ref[...])
        out = b_ref[...]
        for tap in range(FFN_CONV_K):
            off = hl - (FFN_CONV_K - 1) + tap
            out = out + dw_ref[tap:tap + 1, :] * u[off:off + tm, :]
        return out

    act = _silu(conv(upg_ref, dwg_ref, bg_ref)) * conv(upv_ref, dwv_ref, bv_ref)
    acc_ref[...] += _dot(act.astype(BF16), down_ref[...])

    @pl.when(j == pl.num_programs(1) - 1)
    def _():
        y = x_ref[...] + acc_ref[...]
        if final_norm:
            y = y * lax.rsqrt(jnp.mean(y * y, axis=-1, keepdims=True) + RMS_EPS) * fin_ref[...]
        o_ref[...] = y


def _ffn(x2, gain, up, dw, dwb, down, fin, seq_len, final_norm, tm=512, tn=256, hl=16):
    m, d = x2.shape
    dff = down.shape[0]
    nj = dff // tn
    kern = functools.partial(_ffn_kernel, tiles_per_seq=seq_len // tm, final_norm=final_norm)
    return pl.pallas_call(
        kern,
        grid=(m // tm, nj),
        in_specs=[pl.BlockSpec((tm, d), lambda i, j: (i, 0)),
                  pl.BlockSpec((hl, d), lambda i, j: (jnp.maximum(i * (tm // hl) - 1, 0), 0)),
                  pl.BlockSpec((1, d), lambda i, j: (0, 0)),
                  pl.BlockSpec((d, tn), lambda i, j: (0, j)),
                  pl.BlockSpec((d, tn), lambda i, j: (0, j + nj)),
                  pl.BlockSpec((FFN_CONV_K, tn), lambda i, j: (0, j)),
                  pl.BlockSpec((FFN_CONV_K, tn), lambda i, j: (0, j + nj)),
                  pl.BlockSpec((1, tn), lambda i, j: (0, j)),
                  pl.BlockSpec((1, tn), lambda i, j: (0, j + nj)),
                  pl.BlockSpec((tn, d), lambda i, j: (j, 0)),
                  pl.BlockSpec((1, d), lambda i, j: (0, 0))],
        out_specs=pl.BlockSpec((tm, d), lambda i, j: (i, 0)),
        out_shape=jax.ShapeDtypeStruct((m, d), F32),
        scratch_shapes=[pltpu.VMEM((hl + tm, d), BF16), pltpu.VMEM((tm, d), F32)],
        compiler_params=_cparams("parallel", "arbitrary"),
        name="conv_ffn",
    )(x2, x2, gain, up, up, dw, dw, dwb, dwb, down, fin)


def _block_ones():
    i = np.arange(GROUP_WIDTH) // HEAD_DIM
    return jnp.asarray(i[:, None] == i[None, :], BF16)


def _diag_mask():
    v = np.arange(HEAD_DIM)
    l = np.arange(GROUP_WIDTH) % HEAD_DIM
    return jnp.asarray(v[:, None] == l[None, :], F32)


def _importance_map(n_cmp, n_blk):
    gt = np.zeros((n_blk, n_cmp), np.float32)
    per = SLC_BLOCK // CMP_STRIDE
    for c in range(n_blk * per):
        for i in (c - 1, c):
            if 0 <= i < n_cmp - 1:
                gt[c // per, i] += 1.0
    return jnp.asarray(gt, BF16)


def kernel(x, norm_mix, w_in, cmp_pos, cmp_k_w1, cmp_k_w2, cmp_v_w1, cmp_v_w2, beta_nsa, beta_dil, rwkv_mu, rwkv_w0, rwkv_w_up, rwkv_a0, rwkv_a_up, rwkv_g_up, rwkv_k_k, rwkv_k_a, rwkv_r_k, rwkv_ln_g, rwkv_ln_b, conv_dw, conv_dw_b, conv_ln_g, conv_ln_b, w_out, norm_ffn, ffn_up, ffn_dw, ffn_dw_b, ffn_down, norm_final):
    bsz, seq, d = x.shape
    depth = w_in.shape[0]
    gw = GROUP_WIDTH
    n_cmp = seq // CMP_STRIDE
    n_blk = seq // SLC_BLOCK
    half = CMP_STRIDE * HEAD_DIM

    cols = _proj_columns()
    w_in_p = jnp.where(jnp.asarray(cols >= 0)[None, None, :], w_in[:, :, np.maximum(cols, 0)], 0.0).astype(BF16)
    w_out_b = w_out.astype(BF16)
    ffn_up_b = ffn_up.astype(BF16)
    ffn_down_b = ffn_down.astype(BF16)
    ones = _block_ones()
    diag = _diag_mask()
    gt = _importance_map(n_cmp, n_blk)
    row = lambda a: a.reshape(1, -1)

    x2 = x.reshape(bsz * seq, d)
    for l in range(depth):
        proj = _proj_in(x2, row(norm_mix[l]), w_in_p[l]).reshape(bsz, seq, N_PROJ)

        kh = proj[:, :, C_KCVC:C_KCVC + HEAD_DIM].reshape(bsz, n_cmp, half)
        vh = proj[:, :, C_KCVC + HEAD_DIM:C_KCVC + 2 * HEAD_DIM].reshape(bsz, n_cmp, half)
        pos = cmp_pos[l].reshape(1, 2 * half)
        k_cmp, v_cmp = _compress(kh, vh, pos[:, :half], pos[:, half:],
                                 cmp_k_w1[l].astype(BF16), cmp_k_w2[l].astype(BF16),
                                 cmp_v_w1[l].astype(BF16), cmp_v_w2[l].astype(BF16))
        ocmp, sel_t = _nsa_cmp(proj, k_cmp, jnp.swapaxes(v_cmp, 1, 2), gt)
        y_a = _nsa_flash(proj, ocmp, sel_t, row(beta_nsa[l]))

        dil_o, dil_l = zip(*[_banded(proj, dil) for dil in DILATIONS])

        rp = dict(mu=row(rwkv_mu[l]), w0=row(rwkv_w0[l]), w_up=rwkv_w_up[l].astype(BF16), a0=row(rwkv_a0[l]),
                  a_up=rwkv_a_up[l].astype(BF16), g_up=rwkv_g_up[l].astype(BF16), k_k=row(rwkv_k_k[l]),
                  k_a=row(rwkv_k_a[l]), r_k=row(rwkv_r_k[l]))
        kk, w, b, k2, yr, vkr, v, g, c = _rwkv_prep(proj, rp, ones)
        y_scan = _rwkv_scan(kk, w, b, k2, yr, vkr, v, ones, diag)

        y_d = _conformer(proj, conv_dw[l], row(conv_dw_b[l]), row(conv_ln_g[l]), row(conv_ln_b[l]))

        flat = lambda a: a.reshape(bsz * seq, gw)
        parts = [flat(y_a)] + [flat(a) for a in dil_o] + [flat(a) for a in dil_l] \
            + [flat(y_scan), flat(c), flat(v), flat(g), flat(y_d)]
        vecs = [row(beta_dil[l]), row(rwkv_ln_g[l]), row(rwkv_ln_b[l])]
        x2 = _out_proj(x2, parts, vecs, ones, w_out_b[l])

        x2 = _ffn(x2, row(norm_ffn[l]), ffn_up_b[l], ffn_dw[l], row(ffn_dw_b[l]), ffn_down_b[l],
                  row(norm_final), seq, final_norm=(l == depth - 1))
    return x2.reshape(bsz, seq, d)
```

```python
import functools

import jax
import jax.numpy as jnp
import numpy as np
from jax import lax
from jax.experimental import pallas as pl
from jax.experimental.pallas import tpu as pltpu

F32 = jnp.float32
BF16 = jnp.bfloat16

HEAD_DIM = 64
GROUP_HEADS = 4
GROUP_WIDTH = HEAD_DIM * GROUP_HEADS
CMP_STRIDE = 16
CMP_BLOCK = 32
SLC_BLOCK = 64
N_SEL_BLOCKS = 16
NSA_WINDOW = 512
DILATIONS = (1, 4, 16)
TILE = 128
CONV_K = 31
FFN_CONV_K = 3
RMS_EPS = 1e-6
LN_EPS = 1e-5
RWKV_GN_EPS = 64e-5
NEG = -1e30
FORCE_SCORE = 1e9
SCALE = HEAD_DIM ** -0.5
LOG2_E = 1.4426950408889634
SLC_CHUNK = 4
V_ROWS = HEAD_DIM + 16

C_PC, C_PD, C_QD, C_QN, C_G0, C_G1, C_G2 = 0, 1024, 1536, 2304, 2560, 2816, 3072
N_PROJ = 3328
KV_KC, KV_VC, KV_KS, KV_VS, KV_KW, KV_VW = 0, 64, 128, 192, 256, 320
N_KV = 384
DIL_SPAN = 2048

VMEM_LIMIT = 48 * 1024 * 1024


def _cparams(*sem):
    return pltpu.CompilerParams(dimension_semantics=sem, vmem_limit_bytes=VMEM_LIMIT)


def _proj_columns():
    gw, hd = GROUP_WIDTH, HEAD_DIM
    a0 = 0
    q = np.arange(a0, a0 + gw)
    kc, vc, ks, vs, kw, vw = [np.arange(a0 + gw + i * hd, a0 + gw + (i + 1) * hd) for i in range(6)]
    gate0 = a0 + gw + 6 * hd
    gates = [np.repeat(gate0 + 3 * np.arange(GROUP_HEADS) + j, hd) for j in range(3)]
    b0 = gate0 + 3 * GROUP_HEADS
    qd, kd, vd = [np.arange(b0 + i * gw, b0 + (i + 1) * gw) for i in range(3)]
    c0 = b0 + 3 * gw
    pc = np.arange(c0, c0 + 4 * gw)
    d0 = c0 + 4 * gw
    pd = np.arange(d0, d0 + 2 * gw)
    cols = np.concatenate([pc, pd, qd, kd, vd, q, gates[0], gates[1], gates[2]])
    kv_cols = np.concatenate([kc, vc, ks, vs, kw, vw])
    assert cols.shape[0] == N_PROJ and kv_cols.shape[0] == N_KV
    return cols, kv_cols


def _dot_nt(a, b):
    return lax.dot_general(a, b, (((1,), (1,)), ((), ())), preferred_element_type=F32)


def _dot(a, b):
    return jnp.dot(a, b, preferred_element_type=F32)


def _seg_sum(x, ones_bf16):
    hi = x.astype(BF16)
    lo = (x - hi.astype(F32)).astype(BF16)
    return _dot(hi, ones_bf16) + _dot(lo, ones_bf16)


def _sigmoid(x):
    return 1.0 / (1.0 + jnp.exp(-x))


def _silu(x):
    return x * _sigmoid(x)


def _proj_in_kernel(x_ref, g_ref, w_ref, o_ref, h_ref):
    @pl.when(pl.program_id(1) == 0)
    def _():
        x = x_ref[...]
        ms = jnp.mean(x * x, axis=-1, keepdims=True)
        h_ref[...] = (x * lax.rsqrt(ms + RMS_EPS) * g_ref[...]).astype(BF16)

    o_ref[...] = _dot(h_ref[...], w_ref[...])


def _proj_in(x2, gain, w, tn, tm=512):
    m, d = x2.shape
    n = w.shape[1]
    assert n % tn == 0
    return pl.pallas_call(
        _proj_in_kernel,
        grid=(m // tm, n // tn),
        in_specs=[pl.BlockSpec((tm, d), lambda i, j: (i, 0)),
                  pl.BlockSpec((1, d), lambda i, j: (0, 0)),
                  pl.BlockSpec((d, tn), lambda i, j: (0, j))],
        out_specs=pl.BlockSpec((tm, tn), lambda i, j: (i, j)),
        out_shape=jax.ShapeDtypeStruct((m, n), F32),
        scratch_shapes=[pltpu.VMEM((tm, d), BF16)],
        compiler_params=_cparams("parallel", "arbitrary"),
        name=f"proj_in_{n}",
    )(x2, gain, w)


def _compress_kernel(kh_ref, vh_ref, plo_ref, phi_ref, kw1_ref, kw2_ref, vw1_ref, vw2_ref, ko_ref, vo_ref):
    half = kh_ref.shape[2]
    n = kh_ref.shape[1]

    def mlp(xh, w1_ref, w2_ref):
        a = _dot((xh + plo_ref[...]).astype(BF16), w1_ref[0:half, :])
        bm = _dot((xh + phi_ref[...]).astype(BF16), w1_ref[half:2 * half, :])
        hid = a + pltpu.roll(bm, n - 1, axis=0)
        return _dot(_silu(hid).astype(BF16), w2_ref[...])

    ko_ref[0] = mlp(kh_ref[0], kw1_ref, kw2_ref)
    vo_ref[0] = mlp(vh_ref[0], vw1_ref, vw2_ref)


def _compress(kh, vh, pos_lo, pos_hi, kw1, kw2, vw1, vw2):
    b, n, half = kh.shape
    hid = kw1.shape[1]
    full = lambda shape: pl.BlockSpec(shape, lambda i: (0,) * len(shape))
    return pl.pallas_call(
        _compress_kernel,
        grid=(b,),
        in_specs=[pl.BlockSpec((1, n, half), lambda i: (i, 0, 0)),
                  pl.BlockSpec((1, n, half), lambda i: (i, 0, 0)),
                  full((1, half)), full((1, half)),
                  full((2 * half, hid)), full((hid, HEAD_DIM)),
                  full((2 * half, hid)), full((hid, HEAD_DIM))],
        out_specs=[pl.BlockSpec((1, n, HEAD_DIM), lambda i: (i, 0, 0))] * 2,
        out_shape=[jax.ShapeDtypeStruct((b, n, HEAD_DIM), F32)] * 2,
        compiler_params=_cparams("parallel"),
        name="nsa_compress",
    )(kh, vh, pos_lo, pos_hi, kw1, kw2, vw1, vw2)


def _nsa_cmp_kernel(q_ref, kc_ref, vct_ref, gt_ref, ocmp_ref, sel_ref):
    t0 = pl.program_id(1) * TILE
    n_cmp = kc_ref.shape[1]
    n_blk = gt_ref.shape[0]
    q_t = (q_ref[0] * SCALE).T.astype(BF16)
    kc = kc_ref[0].astype(BF16)
    vc_t = vct_ref[0].astype(BF16)
    ci = lax.broadcasted_iota(jnp.int32, (n_cmp, TILE), 0)
    tq = t0 + lax.broadcasted_iota(jnp.int32, (n_cmp, TILE), 1)
    cmask = jnp.where(ci < n_cmp - 1, CMP_STRIDE * ci + CMP_BLOCK - 1, 1 << 30) <= tq
    imp_t = jnp.zeros((n_cmp, TILE), F32)
    outs = []
    for h in range(GROUP_HEADS):
        s = jnp.where(cmask, _dot(kc, q_t[HEAD_DIM * h:HEAD_DIM * (h + 1), :]), NEG)
        e = jnp.exp(s - jnp.max(s, axis=0, keepdims=True))
        p = jnp.where(cmask, e / jnp.sum(e, axis=0, keepdims=True), 0.0)
        imp_t = imp_t + p
        outs.append(_dot(vc_t, p.astype(BF16)))
    ocmp_ref[0] = jnp.concatenate(outs, axis=0).T

    hi = imp_t.astype(BF16)
    lo = (imp_t - hi.astype(F32)).astype(BF16)
    blk = _dot(gt_ref[...], hi) + _dot(gt_ref[...], lo)
    j = lax.broadcasted_iota(jnp.int32, (n_blk, TILE), 0)
    t = t0 + lax.broadcasted_iota(jnp.int32, (n_blk, TILE), 1)
    cur = lax.shift_right_logical(t, 6)
    forced = jnp.where(j == 0, 1, jnp.where(j == cur, 1, jnp.where(j == cur - 1, 1, 0)))
    score = jnp.where(j * SLC_BLOCK <= t, jnp.where(forced == 1, FORCE_SCORE, blk), NEG)
    rank = jnp.zeros((n_blk, TILE), jnp.int32)
    for jp in range(n_blk):
        row = score[jp:jp + 1, :]
        tie = jnp.where(j > jp, 1, 0)
        rank = rank + jnp.where(row > score, 1, jnp.where(row == score, tie, 0))
    sel_ref[0] = jnp.where(rank < N_SEL_BLOCKS, 0.0, NEG)


def _nsa_cmp(proj, k_cmp, v_cmp_t, gt):
    b, s, _ = proj.shape
    n_cmp = k_cmp.shape[1]
    n_blk = gt.shape[0]
    return pl.pallas_call(
        _nsa_cmp_kernel,
        grid=(b, s // TILE),
        in_specs=[pl.BlockSpec((1, TILE, GROUP_WIDTH), lambda i, j: (i, j, C_QN // GROUP_WIDTH)),
                  pl.BlockSpec((1, n_cmp, HEAD_DIM), lambda i, j: (i, 0, 0)),
                  pl.BlockSpec((1, HEAD_DIM, n_cmp), lambda i, j: (i, 0, 0)),
                  pl.BlockSpec(gt.shape, lambda i, j: (0, 0))],
        out_specs=[pl.BlockSpec((1, TILE, GROUP_WIDTH), lambda i, j: (i, j, 0)),
                   pl.BlockSpec((1, n_blk, TILE), lambda i, j: (i, 0, j))],
        out_shape=[jax.ShapeDtypeStruct((b, s, GROUP_WIDTH), F32),
                   jax.ShapeDtypeStruct((b, n_blk, s), F32)],
        compiler_params=_cparams("parallel", "parallel"),
        name="nsa_cmp",
    )(proj, k_cmp, v_cmp_t, gt)


def _nsa_flash_kernel(q_ref, ocmp_ref, sel_ref, g0_ref, g1_ref, g2_ref, ks_ref, vs_ref, kw_ref, vw_ref, beta_ref,
                      o_ref):
    qi = pl.program_id(1)
    cols = GROUP_HEADS * TILE
    q_t = (q_ref[0] * (SCALE * LOG2_E)).T
    qs_t = jnp.concatenate([q_t[HEAD_DIM * h:HEAD_DIM * (h + 1), :] for h in range(GROUP_HEADS)], axis=1)
    sel4 = jnp.concatenate([sel_ref[0]] * GROUP_HEADS, axis=1)
    w = jnp.concatenate([qs_t, sel4], axis=0).astype(BF16)
    q_pos = qi * TILE + lax.broadcasted_iota(jnp.int32, (1, TILE), 1)

    def keys(ref, start, n):
        return ref[0, pl.ds(start, n)].reshape(n * TILE, 2 * HEAD_DIM)

    def values_t(ref, start, n):
        v = ref[0, pl.ds(start, n)]
        return jnp.concatenate([v[u] for u in range(n)], axis=1)

    def band_bias(start, n, window):
        key_pos = start * TILE + lax.broadcasted_iota(jnp.int32, (n * TILE, TILE), 0)
        dist = q_pos - key_pos
        ok = jnp.where(dist >= 0, 1, 0) if window is None else jnp.where(dist >= 0, jnp.where(dist <= window, 1, 0), 0)
        return jnp.concatenate([jnp.where(ok == 1, 0.0, NEG)] * GROUP_HEADS, axis=1)

    def col_max(s):
        return jnp.max(s, axis=0, keepdims=True)

    def weighted_values(k, v_t, m, bias):
        s = _dot(k, w)
        if bias is not None:
            s = s + bias
        return _dot(v_t, jnp.exp2(s - m).astype(BF16))

    def finish(acc):
        o_t = acc[0:HEAD_DIM, :] / acc[HEAD_DIM:HEAD_DIM + 1, :]
        return jnp.concatenate([o_t[:, TILE * h:TILE * (h + 1)] for h in range(GROUP_HEADS)], axis=0).T

    n_full = qi // SLC_CHUNK
    last = n_full * SLC_CHUNK
    last_bias = band_bias(last, SLC_CHUNK, None)
    m = lax.fori_loop(0, n_full, lambda c, m: jnp.maximum(m, col_max(_dot(keys(ks_ref, c * SLC_CHUNK, SLC_CHUNK), w))),
                      col_max(_dot(keys(ks_ref, last, SLC_CHUNK), w) + last_bias))
    acc = lax.fori_loop(
        0, n_full,
        lambda c, acc: acc + weighted_values(keys(ks_ref, c * SLC_CHUNK, SLC_CHUNK),
                                             values_t(vs_ref, c * SLC_CHUNK, SLC_CHUNK), m, None),
        weighted_values(keys(ks_ref, last, SLC_CHUNK), values_t(vs_ref, last, SLC_CHUNK), m, last_bias))
    o_slc = finish(acc)

    n_win = NSA_WINDOW // TILE + 1
    first = jnp.maximum(qi - (n_win - 1), 0)
    win_bias = band_bias(first, n_win, NSA_WINDOW)
    k_win = keys(kw_ref, first, n_win)
    m = col_max(_dot(k_win, w) + win_bias)
    o_win = finish(weighted_values(k_win, values_t(vw_ref, first, n_win), m, win_bias))

    y = (_sigmoid(g0_ref[0]) * ocmp_ref[0] + _sigmoid(g1_ref[0]) * o_slc + _sigmoid(g2_ref[0]) * o_win)
    ms = jnp.mean(y * y, axis=-1, keepdims=True)
    o_ref[0] = y * lax.rsqrt(ms + RMS_EPS) * beta_ref[...]


def _key_tiles(k, onehot):
    b, s, _ = k.shape
    aug = jnp.zeros((b, s, HEAD_DIM), BF16) if onehot is None else jnp.broadcast_to(onehot, (b, s, HEAD_DIM))
    return jnp.concatenate([k.astype(BF16), aug], axis=-1).reshape(b, s // TILE, TILE, 2 * HEAD_DIM)


def _value_tiles_t(v):
    b, s, _ = v.shape
    v_t = v.astype(BF16).reshape(b, s // TILE, TILE, HEAD_DIM).transpose(0, 1, 3, 2)
    pad = jnp.zeros((b, s // TILE, V_ROWS - HEAD_DIM, TILE), BF16).at[:, :, 0, :].set(1.0)
    return jnp.concatenate([v_t, pad], axis=2)


def _nsa_flash(proj, kv, ocmp, sel_t, beta):
    b, s, _ = proj.shape
    gw = GROUP_WIDTH
    n_blk = sel_t.shape[1]
    assert n_blk == HEAD_DIM, "the selection rows share the 128-deep contraction with the 64 head dims"
    piece = lambda c: kv[:, :, c:c + HEAD_DIM]
    blk_of_key = np.arange(s)[:, None] // SLC_BLOCK == np.arange(n_blk)[None, :]
    ks = _key_tiles(piece(KV_KS), jnp.asarray(blk_of_key, BF16))
    kw = _key_tiles(piece(KV_KW), None)
    vs = _value_tiles_t(piece(KV_VS))
    vw = _value_tiles_t(piece(KV_VW))
    nt = s // TILE
    col = lambda c: pl.BlockSpec((1, TILE, gw), lambda i, j: (i, j, c // gw))
    ktile = pl.BlockSpec((1, nt, TILE, 2 * HEAD_DIM), lambda i, j: (i, 0, 0, 0))
    vtile = pl.BlockSpec((1, nt, V_ROWS, TILE), lambda i, j: (i, 0, 0, 0))
    assert nt % SLC_CHUNK == 0 and nt >= NSA_WINDOW // TILE + 1
    return pl.pallas_call(
        _nsa_flash_kernel,
        grid=(b, nt),
        in_specs=[col(C_QN),
                  pl.BlockSpec((1, TILE, gw), lambda i, j: (i, j, 0)),
                  pl.BlockSpec((1, n_blk, TILE), lambda i, j: (i, 0, j)),
                  col(C_G0), col(C_G1), col(C_G2), ktile, vtile, ktile, vtile,
                  pl.BlockSpec((1, gw), lambda i, j: (0, 0))],
        out_specs=pl.BlockSpec((1, TILE, gw), lambda i, j: (i, j, 0)),
        out_shape=jax.ShapeDtypeStruct((b, s, gw), F32),
        compiler_params=_cparams("parallel", "arbitrary"),
        name="nsa_flash",
    )(proj, ocmp, sel_t, proj, proj, proj, ks, vs, kw, vw, beta)


def _dilated_kernel(q0_ref, q1_ref, k0_ref, k1_ref, v0_ref, v1_ref, beta_ref, o_ref, og_ref, lg_ref):
    gw = GROUP_WIDTH
    cols = GROUP_HEADS * TILE
    p0 = pl.program_id(1) * DIL_SPAN

    def load(refs, first, n, rows):
        window = pl.ds(pl.multiple_of(first, TILE), n)
        return jnp.concatenate([ref.at[0, window][rows, :] for ref in refs], axis=1)

    def store(ref, g, first, n, rows, val):
        window = pl.ds(pl.multiple_of(first, TILE), n)
        for u in range(2):
            ref.at[g, u, window][rows, :] = val[:, TILE * u:TILE * (u + 1)]

    row_head = lax.shift_right_logical(lax.broadcasted_iota(jnp.int32, (gw, TILE), 0), 6)
    rel = (lax.broadcasted_iota(jnp.int32, (2 * TILE, TILE), 1)
           - lax.broadcasted_iota(jnp.int32, (2 * TILE, TILE), 0))

    for g, dil in enumerate(DILATIONS):
        n_t = DIL_SPAN // (TILE * dil)

        def body(i, carry, r, g=g, dil=dil):
            sub0 = p0 // dil + i * TILE
            ksub0 = jnp.maximum(sub0 - TILE, 0)
            rows_q = pl.ds(r, TILE, stride=dil)
            rows_k = pl.ds(r, 2 * TILE, stride=dil)
            keys = functools.partial(load, first=dil * ksub0, n=2 * dil * TILE, rows=rows_k)
            q_t = (load((q0_ref, q1_ref), dil * sub0, dil * TILE, rows_q) * (SCALE * LOG2_E)).T
            wq = jnp.concatenate([jnp.where(row_head == h, q_t, 0.0) for h in range(GROUP_HEADS)],
                                 axis=1).astype(BF16)
            dist = rel + (sub0 - ksub0)
            bias = jnp.where(dist >= 0, jnp.where(dist <= TILE, 0.0, NEG), NEG)
            bias = jnp.concatenate([bias] * GROUP_HEADS, axis=1)
            s = _dot(keys((k0_ref, k1_ref)).astype(BF16), wq) + bias
            m = jnp.max(s, axis=0, keepdims=True)
            p = jnp.exp2(s - m)
            l = jnp.sum(p, axis=0, keepdims=True)
            p = p.astype(BF16)
            v_t = keys((v0_ref, v1_ref)).T.astype(BF16)
            lse = m + jnp.log2(l)
            o_t, l_t = [], []
            for h in range(GROUP_HEADS):
                qs = slice(TILE * h, TILE * (h + 1))
                o_t.append(_dot(v_t[HEAD_DIM * h:HEAD_DIM * (h + 1), :], p[:, qs]) / l[:, qs])
                l_t.append(jnp.broadcast_to(lse[:, qs], (HEAD_DIM, TILE)))
            store(og_ref, g, dil * i * TILE, dil * TILE, rows_q, jnp.concatenate(o_t, axis=0).T)
            store(lg_ref, g, dil * i * TILE, dil * TILE, rows_q, jnp.concatenate(l_t, axis=0).T)
            return carry

        for r in range(dil):
            lax.fori_loop(0, n_t, functools.partial(body, r=r), 0)

    piece = 256

    def combine(c, carry):
        rows = pl.ds(pl.multiple_of(c * piece, piece), piece)
        both = lambda ref, g: jnp.concatenate([ref[g, 0, rows, :], ref[g, 1, rows, :]], axis=1)
        ls = [both(lg_ref, g) for g in range(len(DILATIONS))]
        mx = functools.reduce(jnp.maximum, ls)
        es = [jnp.exp2(x - mx) for x in ls]
        den = functools.reduce(lambda a, b: a + b, es)
        y = functools.reduce(lambda a, b: a + b, [(e / den) * both(og_ref, g) for g, e in enumerate(es)])
        o_ref[0, rows, :] = y * lax.rsqrt(jnp.mean(y * y, axis=-1, keepdims=True) + RMS_EPS) * beta_ref[...]
        return carry

    lax.fori_loop(0, DIL_SPAN // piece, combine, 0)


def _dilated(proj, beta):
    b, s, _ = proj.shape
    gw = GROUP_WIDTH
    assert s % DIL_SPAN == 0 and s >= 2 * TILE * max(DILATIONS)
    lane_block = lambda n: pl.BlockSpec((1, s, TILE), lambda i, j: (i, 0, C_QD // TILE + n))
    return pl.pallas_call(
        _dilated_kernel,
        grid=(b, s // DIL_SPAN),
        in_specs=[lane_block(n) for n in range(3 * gw // TILE)] + [pl.BlockSpec((1, gw), lambda i, j: (0, 0))],
        out_specs=pl.BlockSpec((1, DIL_SPAN, gw), lambda i, j: (i, j, 0)),
        out_shape=jax.ShapeDtypeStruct((b, s, gw), F32),
        scratch_shapes=[pltpu.VMEM((len(DILATIONS), gw // TILE, DIL_SPAN, TILE), F32)] * 2,
        compiler_params=_cparams("parallel", "arbitrary"),
        name="dilated",
    )(*([proj] * (3 * gw // TILE)), beta)


def _rwkv_prep_kernel(cur_ref, halo_ref, mu_ref, w0_ref, wup_ref, a0_ref, aup_ref, gup_ref, kk_w_ref, ka_ref,
                      rk_ref, ones_ref,
                      kk_o, w_o, b_o, k_o, yr_o, vkr_o, v_o, g_o, c_o):
    gw = GROUP_WIDTH
    tm = cur_ref.shape[1]
    ones = ones_ref[...]
    x = cur_ref[0]
    last = halo_ref[0][7:8, :]
    last = jnp.where(pl.program_id(1) > 0, last, 0.0)
    rid = lax.broadcasted_iota(jnp.int32, (tm, 1), 0)
    xprev = jnp.where(rid == 0, last, pltpu.roll(x, 1, axis=0))
    xs = x + (xprev - x) * mu_ref[...]
    r, k, v = xs[:, 0:gw], xs[:, gw:2 * gw], xs[:, 2 * gw:3 * gw]
    wd, ad, gd = xs[:, 3 * gw:3 * gw + 64], xs[:, 3 * gw + 64:3 * gw + 128], xs[:, 3 * gw + 128:4 * gw]
    z = -(w0_ref[...] + _dot(jnp.tanh(wd).astype(BF16), wup_ref[...]))
    softplus = jnp.maximum(z, 0.0) + jnp.log1p(jnp.exp(-jnp.abs(z)))
    w = jnp.exp(-jnp.exp(-softplus - 0.5))
    a = _sigmoid(a0_ref[...] + _dot(ad.astype(BF16), aup_ref[...]))
    g = _dot(_sigmoid(gd).astype(BF16), gup_ref[...])
    kk = k * kk_w_ref[...]
    kk = kk * lax.rsqrt(_seg_sum(kk * kk, ones) + 1e-12)
    k2 = k * (1.0 + (a - 1.0) * ka_ref[...])
    b = kk * a
    br = _seg_sum(b * r, ones)
    kr = _seg_sum(k2 * r, ones)
    kk_o[0] = kk
    w_o[0] = w
    b_o[0] = b
    k_o[0] = k2
    yr_o[0] = w * r - kk * br
    vkr_o[0] = v * kr
    v_o[0] = v
    g_o[0] = g
    c_o[0] = _seg_sum(r * k2 * rk_ref[...], ones)


def _rwkv_prep(proj, p, ones, tm=256):
    b, s, _ = proj.shape
    gw = GROUP_WIDTH
    vec = lambda n: pl.BlockSpec((1, n), lambda i, j: (0, 0))
    mat = lambda a: pl.BlockSpec(a.shape, lambda i, j: (0, 0))
    out = pl.BlockSpec((1, tm, gw), lambda i, j: (i, j, 0))
    return pl.pallas_call(
        _rwkv_prep_kernel,
        grid=(b, s // tm),
        in_specs=[pl.BlockSpec((1, tm, 4 * gw), lambda i, j: (i, j, C_PC // (4 * gw))),
                  pl.BlockSpec((1, 8, 4 * gw), lambda i, j: (i, jnp.maximum(j * (tm // 8) - 1, 0), C_PC // (4 * gw))),
                  vec(4 * gw), vec(gw), mat(p["w_up"]), vec(gw), mat(p["a_up"]), mat(p["g_up"]),
                  vec(gw), vec(gw), vec(gw), mat(ones)],
        out_specs=[out] * 9,
        out_shape=[jax.ShapeDtypeStruct((b, s, gw), F32)] * 9,
        compiler_params=_cparams("parallel", "arbitrary"),
        name="rwkv_prep",
    )(proj, proj, p["mu"], p["w0"], p["w_up"], p["a0"], p["a_up"], p["g_up"], p["k_k"], p["k_a"], p["r_k"], ones)


def _rwkv_scan_kernel(kk_ref, w_ref, b_ref, k_ref, yr_ref, vkr_ref, v_ref, ones_ref, diag_ref,
                      yt_ref, state_ref, yacc_ref):
    nb, tb = kk_ref.shape[0], kk_ref.shape[1]

    @pl.when(pl.program_id(0) == 0)
    def _():
        state_ref[...] = jnp.zeros(state_ref.shape, F32)
        yacc_ref[...] = jnp.zeros(yacc_ref.shape, F32)

    ones = ones_ref[...]
    diag = diag_ref[...].astype(BF16)
    lane = jnp.bitwise_and(lax.broadcasted_iota(jnp.int32, (HEAD_DIM, GROUP_WIDTH), 1), HEAD_DIM - 1)

    def group(c, carry):
        def step(tt, carry2):
            t = c * HEAD_DIM + tt
            here = lane == tt
            row = lambda ref, i: ref[i, pl.ds(t, 1), :]
            st = [state_ref[i] for i in range(nb)]
            sb = [st[i].astype(BF16) for i in range(nb)]
            rb = lambda ref, i: row(ref, i).astype(BF16)
            lhs = ([sb[i] * rb(kk_ref, i) for i in range(nb)]
                   + [rb(v_ref, i) * diag for i in range(nb)]
                   + [sb[i] * rb(yr_ref, i) + rb(vkr_ref, i) * diag for i in range(nb)])
            red = _dot(jnp.concatenate(lhs, axis=0), ones)
            part = lambda n: red[n * HEAD_DIM:(n + 1) * HEAD_DIM, :]
            for i in range(nb):
                sa, vcol, ycol = part(i), part(nb + i), part(2 * nb + i)
                state_ref[i] = st[i] * row(w_ref, i) - sa * row(b_ref, i) + vcol * row(k_ref, i)
                yacc_ref[i] = jnp.where(here, ycol, yacc_ref[i])
            return carry2

        lax.fori_loop(0, HEAD_DIM, step, 0)
        for i in range(nb):
            yt_ref[i, c] = yacc_ref[i]
        return carry

    lax.fori_loop(0, tb // HEAD_DIM, group, 0)


def _rwkv_scan(kk, w, b, k, yr, vkr, v, ones, diag, tb=128):
    nb, s, gw = kk.shape
    inp = pl.BlockSpec((nb, tb, gw), lambda j: (0, j, 0))
    const = lambda a: pl.BlockSpec(a.shape, lambda j: (0, 0))
    yt = pl.pallas_call(
        _rwkv_scan_kernel,
        grid=(s // tb,),
        in_specs=[inp] * 7 + [const(ones), const(diag)],
        out_specs=pl.BlockSpec((nb, tb // HEAD_DIM, HEAD_DIM, gw), lambda j: (0, j, 0, 0)),
        out_shape=jax.ShapeDtypeStruct((nb, s // HEAD_DIM, HEAD_DIM, gw), F32),
        scratch_shapes=[pltpu.VMEM((nb, HEAD_DIM, gw), F32), pltpu.VMEM((nb, HEAD_DIM, gw), F32)],
        compiler_params=_cparams("arbitrary"),
        name="rwkv_scan",
    )(kk, w, b, k, yr, vkr, v, ones, diag)
    yt = yt.reshape(nb, s // HEAD_DIM, HEAD_DIM, GROUP_HEADS, HEAD_DIM)
    return yt.transpose(0, 1, 4, 3, 2).reshape(nb, s, gw)


def _conformer_kernel(cur_ref, halo_ref, dw_ref, dwb_ref, lng_ref, lnb_ref, o_ref, g_ref):
    gw = GROUP_WIDTH
    tm = cur_ref.shape[1]
    hl = halo_ref.shape[1]
    glu = lambda x: x[:, 0:gw] * _sigmoid(x[:, gw:2 * gw])
    g_ref[0:hl, :] = jnp.where(pl.program_id(1) > 0, glu(halo_ref[0]), 0.0)
    g_ref[hl:hl + tm, :] = glu(cur_ref[0])
    acc = jnp.zeros((tm, gw), F32)
    for j in range(CONV_K):
        acc = acc + dw_ref[j:j + 1, :] * g_ref[pl.ds(hl - (CONV_K - 1) + j, tm), :]
    h = acc + dwb_ref[...]
    mu = jnp.mean(h, axis=-1, keepdims=True)
    d = h - mu
    var = jnp.mean(d * d, axis=-1, keepdims=True)
    o_ref[0] = _silu(d * lax.rsqrt(var + LN_EPS) * lng_ref[...] + lnb_ref[...])


def _conformer(proj, dw, dwb, lng, lnb, tm=512, hl=32):
    b, s, _ = proj.shape
    gw = GROUP_WIDTH
    vec = pl.BlockSpec((1, gw), lambda i, j: (0, 0))
    return pl.pallas_call(
        _conformer_kernel,
        grid=(b, s // tm),
        in_specs=[pl.BlockSpec((1, tm, 2 * gw), lambda i, j: (i, j, C_PD // (2 * gw))),
                  pl.BlockSpec((1, hl, 2 * gw), lambda i, j: (i, jnp.maximum(j * (tm // hl) - 1, 0), C_PD // (2 * gw))),
                  pl.BlockSpec((CONV_K, gw), lambda i, j: (0, 0)), vec, vec, vec],
        out_specs=pl.BlockSpec((1, tm, gw), lambda i, j: (i, j, 0)),
        out_shape=jax.ShapeDtypeStruct((b, s, gw), F32),
        scratch_shapes=[pltpu.VMEM((hl + tm, gw), F32)],
        compiler_params=_cparams("parallel", "arbitrary"),
        name="conformer_conv",
    )(proj, proj, dw, dwb, lng, lnb)


def _out_proj_kernel(x_ref, ya_ref, yb_ref, ys_ref, c_ref, v_ref, g_ref, yd_ref, lng_ref, lnb_ref, ones_ref, w_ref,
                     o_ref):
    gw = GROUP_WIDTH
    ones = ones_ref[...]
    y = ys_ref[...] + c_ref[...] * v_ref[...]
    mu = _seg_sum(y, ones) * (1.0 / HEAD_DIM)
    d = y - mu
    var = _seg_sum(d * d, ones) * (1.0 / HEAD_DIM)
    yc = (d * lax.rsqrt(var + RWKV_GN_EPS) * lng_ref[...] + lnb_ref[...]) * g_ref[...]
    acc = x_ref[...]
    for n, part in enumerate((ya_ref[...], yb_ref[...], yc, yd_ref[...])):
        acc = acc + _dot(part.astype(BF16), w_ref[n * gw:(n + 1) * gw, :])
    o_ref[...] = acc


def _out_proj(x2, parts, vecs, ones, w, tm=512):
    m, d = x2.shape
    gw = GROUP_WIDTH
    blk = pl.BlockSpec((tm, gw), lambda i: (i, 0))
    vec = pl.BlockSpec((1, gw), lambda i: (0, 0))
    return pl.pallas_call(
        _out_proj_kernel,
        grid=(m // tm,),
        in_specs=[pl.BlockSpec((tm, d), lambda i: (i, 0))] + [blk] * len(parts) + [vec] * len(vecs)
                 + [pl.BlockSpec(ones.shape, lambda i: (0, 0)), pl.BlockSpec(w.shape, lambda i: (0, 0))],
        out_specs=pl.BlockSpec((tm, d), lambda i: (i, 0)),
        out_shape=jax.ShapeDtypeStruct((m, d), F32),
        compiler_params=_cparams("parallel"),
        name="out_proj",
    )(x2, *parts, *vecs, ones, w)


def _ffn_kernel(x_ref, halo_ref, gain_ref, upg_ref, upv_ref, dwg_ref, dwv_ref, bg_ref, bv_ref, down_ref, fin_ref,
                o_ref, h_ref, acc_ref, ug_ref, uv_ref, *, tiles_per_seq, final_norm):
    j = pl.program_id(1)
    tm = x_ref.shape[0]
    hl = halo_ref.shape[0]

    def norm(x):
        return (x * lax.rsqrt(jnp.mean(x * x, axis=-1, keepdims=True) + RMS_EPS) * gain_ref[...]).astype(BF16)

    @pl.when(j == 0)
    def _():
        first = pl.program_id(0) % tiles_per_seq == 0
        h_ref[0:hl, :] = jnp.where(first, jnp.zeros((hl, x_ref.shape[1]), BF16), norm(halo_ref[...]))
        h_ref[hl:hl + tm, :] = norm(x_ref[...])
        acc_ref[...] = jnp.zeros(acc_ref.shape, F32)

    h = h_ref[...]

    def conv(up_ref, dw_ref, b_ref, u_ref):
        u_ref[...] = _dot(h, up_ref[...])
        out = b_ref[...]
        for tap in range(FFN_CONV_K):
            out = out + dw_ref[tap:tap + 1, :] * u_ref[pl.ds(hl - (FFN_CONV_K - 1) + tap, tm), :]
        return out

    act = _silu(conv(upg_ref, dwg_ref, bg_ref, ug_ref)) * conv(upv_ref, dwv_ref, bv_ref, uv_ref)
    acc_ref[...] += _dot(act.astype(BF16), down_ref[...])

    @pl.when(j == pl.num_programs(1) - 1)
    def _():
        y = x_ref[...] + acc_ref[...]
        if final_norm:
            y = y * lax.rsqrt(jnp.mean(y * y, axis=-1, keepdims=True) + RMS_EPS) * fin_ref[...]
        o_ref[...] = y


def _ffn(x2, gain, up, dw, dwb, down, fin, seq_len, final_norm, tm=512, tn=1408, hl=16):
    m, d = x2.shape
    dff = down.shape[0]
    nj = dff // tn
    kern = functools.partial(_ffn_kernel, tiles_per_seq=seq_len // tm, final_norm=final_norm)
    return pl.pallas_call(
        kern,
        grid=(m // tm, nj),
        in_specs=[pl.BlockSpec((tm, d), lambda i, j: (i, 0)),
                  pl.BlockSpec((hl, d), lambda i, j: (jnp.maximum(i * (tm // hl) - 1, 0), 0)),
                  pl.BlockSpec((1, d), lambda i, j: (0, 0)),
                  pl.BlockSpec((d, tn), lambda i, j: (0, j)),
                  pl.BlockSpec((d, tn), lambda i, j: (0, j + nj)),
                  pl.BlockSpec((FFN_CONV_K, tn), lambda i, j: (0, j)),
                  pl.BlockSpec((FFN_CONV_K, tn), lambda i, j: (0, j + nj)),
                  pl.BlockSpec((1, tn), lambda i, j: (0, j)),
                  pl.BlockSpec((1, tn), lambda i, j: (0, j + nj)),
                  pl.BlockSpec((tn, d), lambda i, j: (j, 0)),
                  pl.BlockSpec((1, d), lambda i, j: (0, 0))],
        out_specs=pl.BlockSpec((tm, d), lambda i, j: (i, 0)),
        out_shape=jax.ShapeDtypeStruct((m, d), F32),
        scratch_shapes=[pltpu.VMEM((hl + tm, d), BF16), pltpu.VMEM((tm, d), F32),
                        pltpu.VMEM((hl + tm, tn), F32), pltpu.VMEM((hl + tm, tn), F32)],
        compiler_params=_cparams("parallel", "arbitrary"),
        name="conv_ffn",
    )(x2, x2, gain, up, up, dw, dw, dwb, dwb, down, fin)


def _block_ones():
    i = np.arange(GROUP_WIDTH) // HEAD_DIM
    return jnp.asarray(i[:, None] == i[None, :], BF16)


def _diag_mask():
    v = np.arange(HEAD_DIM)
    l = np.arange(GROUP_WIDTH) % HEAD_DIM
    return jnp.asarray(v[:, None] == l[None, :], F32)


def _importance_map(n_cmp, n_blk):
    gt = np.zeros((n_blk, n_cmp), np.float32)
    per = SLC_BLOCK // CMP_STRIDE
    for c in range(n_blk * per):
        for i in (c - 1, c):
            if 0 <= i < n_cmp - 1:
                gt[c // per, i] += 1.0
    return jnp.asarray(gt, BF16)


def kernel(x, norm_mix, w_in, cmp_pos, cmp_k_w1, cmp_k_w2, cmp_v_w1, cmp_v_w2, beta_nsa, beta_dil, rwkv_mu, rwkv_w0, rwkv_w_up, rwkv_a0, rwkv_a_up, rwkv_g_up, rwkv_k_k, rwkv_k_a, rwkv_r_k, rwkv_ln_g, rwkv_ln_b, conv_dw, conv_dw_b, conv_ln_g, conv_ln_b, w_out, norm_ffn, ffn_up, ffn_dw, ffn_dw_b, ffn_down, norm_final):
    bsz, seq, d = x.shape
    depth = w_in.shape[0]
    gw = GROUP_WIDTH
    n_cmp = seq // CMP_STRIDE
    n_blk = seq // SLC_BLOCK
    half = CMP_STRIDE * HEAD_DIM

    cols, kv_cols = _proj_columns()
    w_in_p = w_in[:, :, cols].astype(BF16)
    w_kv_p = w_in[:, :, kv_cols].astype(BF16)
    w_out_b = w_out.astype(BF16)
    ffn_up_b = ffn_up.astype(BF16)
    ffn_down_b = ffn_down.astype(BF16)
    ones = _block_ones()
    diag = _diag_mask()
    gt = _importance_map(n_cmp, n_blk)
    row = lambda a: a.reshape(1, -1)

    x2 = x.reshape(bsz * seq, d)
    for l in range(depth):
        proj = _proj_in(x2, row(norm_mix[l]), w_in_p[l], tn=N_PROJ // 2).reshape(bsz, seq, N_PROJ)
        kv = _proj_in(x2, row(norm_mix[l]), w_kv_p[l], tn=N_KV).reshape(bsz, seq, N_KV)

        kh = kv[:, :, KV_KC:KV_KC + HEAD_DIM].reshape(bsz, n_cmp, half)
        vh = kv[:, :, KV_VC:KV_VC + HEAD_DIM].reshape(bsz, n_cmp, half)
        pos = cmp_pos[l].reshape(1, 2 * half)
        k_cmp, v_cmp = _compress(kh, vh, pos[:, :half], pos[:, half:],
                                 cmp_k_w1[l].astype(BF16), cmp_k_w2[l].astype(BF16),
                                 cmp_v_w1[l].astype(BF16), cmp_v_w2[l].astype(BF16))
        ocmp, sel_t = _nsa_cmp(proj, k_cmp, jnp.swapaxes(v_cmp, 1, 2), gt)
        y_a = _nsa_flash(proj, kv, ocmp, sel_t, row(beta_nsa[l]))

        y_b = _dilated(proj, row(beta_dil[l]))

        rp = dict(mu=row(rwkv_mu[l]), w0=row(rwkv_w0[l]), w_up=rwkv_w_up[l].astype(BF16), a0=row(rwkv_a0[l]),
                  a_up=rwkv_a_up[l].astype(BF16), g_up=rwkv_g_up[l].astype(BF16), k_k=row(rwkv_k_k[l]),
                  k_a=row(rwkv_k_a[l]), r_k=row(rwkv_r_k[l]))
        kk, w, b, k2, yr, vkr, v, g, c = _rwkv_prep(proj, rp, ones)
        y_scan = _rwkv_scan(kk, w, b, k2, yr, vkr, v, ones, diag)

        y_d = _conformer(proj, conv_dw[l], row(conv_dw_b[l]), row(conv_ln_g[l]), row(conv_ln_b[l]))

        flat = lambda a: a.reshape(bsz * seq, gw)
        parts = [flat(a) for a in (y_a, y_b, y_scan, c, v, g, y_d)]
        vecs = [row(rwkv_ln_g[l]), row(rwkv_ln_b[l])]
        x2 = _out_proj(x2, parts, vecs, ones, w_out_b[l])

        x2 = _ffn(x2, row(norm_ffn[l]), ffn_up_b[l], ffn_dw[l], row(ffn_dw_b[l]), ffn_down_b[l],
                  row(norm_final), seq, final_norm=(l == depth - 1))
    return x2.reshape(bsz, seq, d)
```

```python
import functools

import jax
import jax.numpy as jnp
import numpy as np
from jax import lax
from jax.experimental import pallas as pl
from jax.experimental.pallas import tpu as pltpu

F32 = jnp.float32
BF16 = jnp.bfloat16

HEAD_DIM = 64
GROUP_HEADS = 4
GROUP_WIDTH = HEAD_DIM * GROUP_HEADS
CMP_STRIDE = 16
CMP_BLOCK = 32
SLC_BLOCK = 64
N_SEL_BLOCKS = 16
NSA_WINDOW = 512
DILATIONS = (1, 4, 16)
TILE = 128
CONV_K = 31
FFN_CONV_K = 3
RMS_EPS = 1e-6
LN_EPS = 1e-5
RWKV_GN_EPS = 64e-5
NEG = -1e30
FORCE_SCORE = 1e9
SCALE = HEAD_DIM ** -0.5
LOG2_E = 1.4426950408889634
SCAN_UNROLL = 4
SLC_CHUNK = 4
V_ROWS = HEAD_DIM + 16

C_PC, C_PD, C_QD, C_QN, C_G0, C_G1, C_G2 = 0, 1024, 1536, 2304, 2560, 2816, 3072
N_PROJ = 3328
KV_KC, KV_VC, KV_KS, KV_VS, KV_KW, KV_VW = 0, 64, 128, 192, 256, 320
N_KV = 384
DIL_SPAN = 2048

VMEM_LIMIT = 48 * 1024 * 1024


def _cparams(*sem):
    return pltpu.CompilerParams(dimension_semantics=sem, vmem_limit_bytes=VMEM_LIMIT)


def _proj_columns():
    gw, hd = GROUP_WIDTH, HEAD_DIM
    a0 = 0
    q = np.arange(a0, a0 + gw)
    kc, vc, ks, vs, kw, vw = [np.arange(a0 + gw + i * hd, a0 + gw + (i + 1) * hd) for i in range(6)]
    gate0 = a0 + gw + 6 * hd
    gates = [np.repeat(gate0 + 3 * np.arange(GROUP_HEADS) + j, hd) for j in range(3)]
    b0 = gate0 + 3 * GROUP_HEADS
    qd, kd, vd = [np.arange(b0 + i * gw, b0 + (i + 1) * gw) for i in range(3)]
    c0 = b0 + 3 * gw
    pc = np.arange(c0, c0 + 4 * gw)
    d0 = c0 + 4 * gw
    pd = np.arange(d0, d0 + 2 * gw)
    cols = np.concatenate([pc, pd, qd, kd, vd, q, gates[0], gates[1], gates[2]])
    kv_cols = np.concatenate([kc, vc, ks, vs, kw, vw])
    assert cols.shape[0] == N_PROJ and kv_cols.shape[0] == N_KV
    return cols, kv_cols


def _dot_nt(a, b):
    return lax.dot_general(a, b, (((1,), (1,)), ((), ())), preferred_element_type=F32)


def _dot(a, b):
    return jnp.dot(a, b, preferred_element_type=F32)


def _seg_sum(x, ones_bf16):
    hi = x.astype(BF16)
    lo = (x - hi.astype(F32)).astype(BF16)
    return _dot(hi, ones_bf16) + _dot(lo, ones_bf16)


def _sigmoid(x):
    return 1.0 / (1.0 + jnp.exp(-x))


def _silu(x):
    return x * _sigmoid(x)


def _proj_in_kernel(x_ref, g_ref, w_ref, o_ref, h_ref):
    @pl.when(pl.program_id(1) == 0)
    def _():
        x = x_ref[...]
        ms = jnp.mean(x * x, axis=-1, keepdims=True)
        h_ref[...] = (x * lax.rsqrt(ms + RMS_EPS) * g_ref[...]).astype(BF16)

    o_ref[...] = _dot(h_ref[...], w_ref[...])


def _proj_in(x2, gain, w, tn, tm=512):
    m, d = x2.shape
    n = w.shape[1]
    assert n % tn == 0
    return pl.pallas_call(
        _proj_in_kernel,
        grid=(m // tm, n // tn),
        in_specs=[pl.BlockSpec((tm, d), lambda i, j: (i, 0)),
                  pl.BlockSpec((1, d), lambda i, j: (0, 0)),
                  pl.BlockSpec((d, tn), lambda i, j: (0, j))],
        out_specs=pl.BlockSpec((tm, tn), lambda i, j: (i, j)),
        out_shape=jax.ShapeDtypeStruct((m, n), F32),
        scratch_shapes=[pltpu.VMEM((tm, d), BF16)],
        compiler_params=_cparams("parallel", "arbitrary"),
        name=f"proj_in_{n}",
    )(x2, gain, w)


def _compress_kernel(kh_ref, vh_ref, plo_ref, phi_ref, kw1_ref, kw2_ref, vw1_ref, vw2_ref, ko_ref, vo_ref):
    half = kh_ref.shape[2]
    n = kh_ref.shape[1]

    def mlp(xh, w1_ref, w2_ref):
        a = _dot((xh + plo_ref[...]).astype(BF16), w1_ref[0:half, :])
        bm = _dot((xh + phi_ref[...]).astype(BF16), w1_ref[half:2 * half, :])
        hid = a + pltpu.roll(bm, n - 1, axis=0)
        return _dot(_silu(hid).astype(BF16), w2_ref[...])

    ko_ref[0] = mlp(kh_ref[0], kw1_ref, kw2_ref)
    vo_ref[0] = mlp(vh_ref[0], vw1_ref, vw2_ref)


def _compress(kh, vh, pos_lo, pos_hi, kw1, kw2, vw1, vw2):
    b, n, half = kh.shape
    hid = kw1.shape[1]
    full = lambda shape: pl.BlockSpec(shape, lambda i: (0,) * len(shape))
    return pl.pallas_call(
        _compress_kernel,
        grid=(b,),
        in_specs=[pl.BlockSpec((1, n, half), lambda i: (i, 0, 0)),
                  pl.BlockSpec((1, n, half), lambda i: (i, 0, 0)),
                  full((1, half)), full((1, half)),
                  full((2 * half, hid)), full((hid, HEAD_DIM)),
                  full((2 * half, hid)), full((hid, HEAD_DIM))],
        out_specs=[pl.BlockSpec((1, n, HEAD_DIM), lambda i: (i, 0, 0))] * 2,
        out_shape=[jax.ShapeDtypeStruct((b, n, HEAD_DIM), F32)] * 2,
        compiler_params=_cparams("parallel"),
        name="nsa_compress",
    )(kh, vh, pos_lo, pos_hi, kw1, kw2, vw1, vw2)


def _nsa_cmp_kernel(q_ref, kc_ref, vct_ref, gt_ref, ocmp_ref, sel_ref):
    t0 = pl.program_id(1) * TILE
    n_cmp = kc_ref.shape[1]
    n_blk = gt_ref.shape[0]
    q_t = (q_ref[0] * SCALE).T.astype(BF16)
    kc = kc_ref[0].astype(BF16)
    vc_t = vct_ref[0].astype(BF16)
    ci = lax.broadcasted_iota(jnp.int32, (n_cmp, TILE), 0)
    tq = t0 + lax.broadcasted_iota(jnp.int32, (n_cmp, TILE), 1)
    cmask = jnp.where(ci < n_cmp - 1, CMP_STRIDE * ci + CMP_BLOCK - 1, 1 << 30) <= tq
    imp_t = jnp.zeros((n_cmp, TILE), F32)
    outs = []
    for h in range(GROUP_HEADS):
        s = jnp.where(cmask, _dot(kc, q_t[HEAD_DIM * h:HEAD_DIM * (h + 1), :]), NEG)
        e = jnp.exp(s - jnp.max(s, axis=0, keepdims=True))
        p = jnp.where(cmask, e / jnp.sum(e, axis=0, keepdims=True), 0.0)
        imp_t = imp_t + p
        outs.append(_dot(vc_t, p.astype(BF16)))
    ocmp_ref[0] = jnp.concatenate(outs, axis=0).T

    hi = imp_t.astype(BF16)
    lo = (imp_t - hi.astype(F32)).astype(BF16)
    blk = _dot(gt_ref[...], hi) + _dot(gt_ref[...], lo)
    j = lax.broadcasted_iota(jnp.int32, (n_blk, TILE), 0)
    t = t0 + lax.broadcasted_iota(jnp.int32, (n_blk, TILE), 1)
    cur = lax.shift_right_logical(t, 6)
    forced = jnp.where(j == 0, 1, jnp.where(j == cur, 1, jnp.where(j == cur - 1, 1, 0)))
    score = jnp.where(j * SLC_BLOCK <= t, jnp.where(forced == 1, FORCE_SCORE, blk), NEG)
    rank = jnp.zeros((n_blk, TILE), jnp.int32)
    for jp in range(n_blk):
        row = score[jp:jp + 1, :]
        tie = jnp.where(j > jp, 1, 0)
        rank = rank + jnp.where(row > score, 1, jnp.where(row == score, tie, 0))
    sel_ref[0] = jnp.where(rank < N_SEL_BLOCKS, 0.0, NEG)


def _nsa_cmp(proj, k_cmp, v_cmp_t, gt):
    b, s, _ = proj.shape
    n_cmp = k_cmp.shape[1]
    n_blk = gt.shape[0]
    return pl.pallas_call(
        _nsa_cmp_kernel,
        grid=(b, s // TILE),
        in_specs=[pl.BlockSpec((1, TILE, GROUP_WIDTH), lambda i, j: (i, j, C_QN // GROUP_WIDTH)),
                  pl.BlockSpec((1, n_cmp, HEAD_DIM), lambda i, j: (i, 0, 0)),
                  pl.BlockSpec((1, HEAD_DIM, n_cmp), lambda i, j: (i, 0, 0)),
                  pl.BlockSpec(gt.shape, lambda i, j: (0, 0))],
        out_specs=[pl.BlockSpec((1, TILE, GROUP_WIDTH), lambda i, j: (i, j, 0)),
                   pl.BlockSpec((1, n_blk, TILE), lambda i, j: (i, 0, j))],
        out_shape=[jax.ShapeDtypeStruct((b, s, GROUP_WIDTH), F32),
                   jax.ShapeDtypeStruct((b, n_blk, s), F32)],
        compiler_params=_cparams("parallel", "parallel"),
        name="nsa_cmp",
    )(proj, k_cmp, v_cmp_t, gt)


def _nsa_flash_kernel(q_ref, ocmp_ref, sel_ref, g0_ref, g1_ref, g2_ref, ks_ref, vs_ref, kw_ref, vw_ref, beta_ref,
                      o_ref, sa_ref, sb_ref, sw_ref):
    qi = pl.program_id(1)
    cols = GROUP_HEADS * TILE
    q_t = (q_ref[0] * (SCALE * LOG2_E)).T
    qs_t = jnp.concatenate([q_t[HEAD_DIM * h:HEAD_DIM * (h + 1), :] for h in range(GROUP_HEADS)], axis=1)
    sel4 = jnp.concatenate([sel_ref[0]] * GROUP_HEADS, axis=1)
    w = jnp.concatenate([qs_t, sel4], axis=0).astype(BF16)
    q_pos = qi * TILE + lax.broadcasted_iota(jnp.int32, (1, TILE), 1)

    def keys(ref, start, n):
        return ref[0, pl.ds(start, n)].reshape(n * TILE, 2 * HEAD_DIM)

    def values_t(ref, start, n):
        v = ref[0, pl.ds(start, n)]
        return jnp.concatenate([v[u] for u in range(n)], axis=1)

    def band_bias(start, n, window):
        key_pos = start * TILE + lax.broadcasted_iota(jnp.int32, (n * TILE, TILE), 0)
        dist = q_pos - key_pos
        ok = jnp.where(dist >= 0, 1, 0) if window is None else jnp.where(dist >= 0, jnp.where(dist <= window, 1, 0), 0)
        return jnp.concatenate([jnp.where(ok == 1, 0.0, NEG)] * GROUP_HEADS, axis=1)

    def col_max(s):
        return jnp.max(s, axis=0, keepdims=True)

    def finish(acc):
        o_t = acc[0:HEAD_DIM, :] / acc[HEAD_DIM:HEAD_DIM + 1, :]
        return jnp.concatenate([o_t[:, TILE * h:TILE * (h + 1)] for h in range(GROUP_HEADS)], axis=0).T

    n_win = NSA_WINDOW // TILE + 1
    first = jnp.maximum(qi - (n_win - 1), 0)
    sw_ref[...] = _dot(keys(kw_ref, first, n_win), w) + band_bias(first, n_win, NSA_WINDOW)
    p = jnp.exp2(sw_ref[...] - col_max(sw_ref[...])).astype(BF16)
    o_win = finish(_dot(values_t(vw_ref, first, n_win), p))

    def scores(c):
        return _dot(keys(ks_ref, c * SLC_CHUNK, SLC_CHUNK), w) + band_bias(c * SLC_CHUNK, SLC_CHUNK, None)

    def prefetch(ref, c, extra=0.0):
        s = scores(c) + extra
        ref[...] = s
        return col_max(s)

    def consume(ref, c, m, cm, acc):
        m_new = jnp.maximum(m, cm)
        p = jnp.exp2(ref[...] - m_new).astype(BF16)
        return m_new, jnp.exp2(m - m_new) * acc + _dot(values_t(vs_ref, c * SLC_CHUNK, SLC_CHUNK), p)

    def two_chunks(j, carry):
        m, cm_a, acc = carry
        cm_b = prefetch(sb_ref, 2 * j + 1)
        m, acc = consume(sa_ref, 2 * j, m, cm_a, acc)
        cm_a = prefetch(sa_ref, 2 * j + 2)
        m, acc = consume(sb_ref, 2 * j + 1, m, cm_b, acc)
        return m, cm_a, acc

    n_full = qi // SLC_CHUNK
    n_pairs = n_full // 2
    m, cm_a, acc = lax.fori_loop(0, n_pairs, two_chunks,
                                 (jnp.full((1, cols), NEG, F32), prefetch(sa_ref, 0), jnp.zeros((V_ROWS, cols), F32)))
    odd = 2 * n_pairs + 1
    cm_b = prefetch(sb_ref, jnp.minimum(odd, n_full), jnp.where(odd <= n_full, 0.0, NEG))
    m, acc = consume(sa_ref, 2 * n_pairs, m, cm_a, acc)
    o_slc = finish(consume(sb_ref, jnp.minimum(odd, n_full), m, cm_b, acc)[1])

    y = (_sigmoid(g0_ref[0]) * ocmp_ref[0] + _sigmoid(g1_ref[0]) * o_slc + _sigmoid(g2_ref[0]) * o_win)
    ms = jnp.mean(y * y, axis=-1, keepdims=True)
    o_ref[0] = y * lax.rsqrt(ms + RMS_EPS) * beta_ref[...]


def _key_tiles(k, onehot):
    b, s, _ = k.shape
    aug = jnp.zeros((b, s, HEAD_DIM), BF16) if onehot is None else jnp.broadcast_to(onehot, (b, s, HEAD_DIM))
    return jnp.concatenate([k.astype(BF16), aug], axis=-1).reshape(b, s // TILE, TILE, 2 * HEAD_DIM)


def _value_tiles_t(v):
    b, s, _ = v.shape
    v_t = v.astype(BF16).reshape(b, s // TILE, TILE, HEAD_DIM).transpose(0, 1, 3, 2)
    pad = jnp.zeros((b, s // TILE, V_ROWS - HEAD_DIM, TILE), BF16).at[:, :, 0, :].set(1.0)
    return jnp.concatenate([v_t, pad], axis=2)


def _nsa_flash(proj, kv, ocmp, sel_t, beta):
    b, s, _ = proj.shape
    gw = GROUP_WIDTH
    n_blk = sel_t.shape[1]
    assert n_blk == HEAD_DIM, "the selection rows share the 128-deep contraction with the 64 head dims"
    piece = lambda c: kv[:, :, c:c + HEAD_DIM]
    blk_of_key = np.arange(s)[:, None] // SLC_BLOCK == np.arange(n_blk)[None, :]
    ks = _key_tiles(piece(KV_KS), jnp.asarray(blk_of_key, BF16))
    kw = _key_tiles(piece(KV_KW), None)
    vs = _value_tiles_t(piece(KV_VS))
    vw = _value_tiles_t(piece(KV_VW))
    nt = s // TILE
    col = lambda c: pl.BlockSpec((1, TILE, gw), lambda i, j: (i, j, c // gw))
    ktile = pl.BlockSpec((1, nt, TILE, 2 * HEAD_DIM), lambda i, j: (i, 0, 0, 0))
    vtile = pl.BlockSpec((1, nt, V_ROWS, TILE), lambda i, j: (i, 0, 0, 0))
    assert nt % SLC_CHUNK == 0 and nt >= NSA_WINDOW // TILE + 1
    return pl.pallas_call(
        _nsa_flash_kernel,
        grid=(b, nt),
        in_specs=[col(C_QN),
                  pl.BlockSpec((1, TILE, gw), lambda i, j: (i, j, 0)),
                  pl.BlockSpec((1, n_blk, TILE), lambda i, j: (i, 0, j)),
                  col(C_G0), col(C_G1), col(C_G2), ktile, vtile, ktile, vtile,
                  pl.BlockSpec((1, gw), lambda i, j: (0, 0))],
        out_specs=pl.BlockSpec((1, TILE, gw), lambda i, j: (i, j, 0)),
        out_shape=jax.ShapeDtypeStruct((b, s, gw), F32),
        scratch_shapes=[pltpu.VMEM((SLC_CHUNK * TILE, GROUP_HEADS * TILE), F32),
                        pltpu.VMEM((SLC_CHUNK * TILE, GROUP_HEADS * TILE), F32),
                        pltpu.VMEM((NSA_WINDOW + TILE, GROUP_HEADS * TILE), F32)],
        compiler_params=_cparams("parallel", "arbitrary"),
        name="nsa_flash",
    )(proj, ocmp, sel_t, proj, proj, proj, ks, vs, kw, vw, beta)


def _dilated_kernel(q0_ref, q1_ref, k0_ref, k1_ref, v0_ref, v1_ref, beta_ref, o_ref, og_ref, lg_ref):
    gw = GROUP_WIDTH
    cols = GROUP_HEADS * TILE
    p0 = pl.program_id(1) * DIL_SPAN

    def load(refs, first, n, rows):
        window = pl.ds(pl.multiple_of(first, TILE), n)
        return jnp.concatenate([ref.at[0, window][rows, :] for ref in refs], axis=1)

    def store(ref, g, first, n, rows, val):
        window = pl.ds(pl.multiple_of(first, TILE), n)
        for u in range(2):
            ref.at[g, u, window][rows, :] = val[:, TILE * u:TILE * (u + 1)]

    row_head = lax.shift_right_logical(lax.broadcasted_iota(jnp.int32, (gw, TILE), 0), 6)
    rel = (lax.broadcasted_iota(jnp.int32, (2 * TILE, TILE), 1)
           - lax.broadcasted_iota(jnp.int32, (2 * TILE, TILE), 0))

    for g, dil in enumerate(DILATIONS):
        n_t = DIL_SPAN // (TILE * dil)

        def body(i, carry, r, g=g, dil=dil):
            sub0 = p0 // dil + i * TILE
            ksub0 = jnp.maximum(sub0 - TILE, 0)
            rows_q = pl.ds(r, TILE, stride=dil)
            rows_k = pl.ds(r, 2 * TILE, stride=dil)
            keys = functools.partial(load, first=dil * ksub0, n=2 * dil * TILE, rows=rows_k)
            q_t = (load((q0_ref, q1_ref), dil * sub0, dil * TILE, rows_q) * (SCALE * LOG2_E)).T
            wq = jnp.concatenate([jnp.where(row_head == h, q_t, 0.0) for h in range(GROUP_HEADS)],
                                 axis=1).astype(BF16)
            dist = rel + (sub0 - ksub0)
            bias = jnp.where(dist >= 0, jnp.where(dist <= TILE, 0.0, NEG), NEG)
            bias = jnp.concatenate([bias] * GROUP_HEADS, axis=1)
            s = _dot(keys((k0_ref, k1_ref)).astype(BF16), wq) + bias
            m = jnp.max(s, axis=0, keepdims=True)
            p = jnp.exp2(s - m)
            l = jnp.sum(p, axis=0, keepdims=True)
            p = p.astype(BF16)
            v_t = keys((v0_ref, v1_ref)).T.astype(BF16)
            lse = m + jnp.log2(l)
            o_t, l_t = [], []
            for h in range(GROUP_HEADS):
                qs = slice(TILE * h, TILE * (h + 1))
                o_t.append(_dot(v_t[HEAD_DIM * h:HEAD_DIM * (h + 1), :], p[:, qs]) / l[:, qs])
                l_t.append(jnp.broadcast_to(lse[:, qs], (HEAD_DIM, TILE)))
            store(og_ref, g, dil * i * TILE, dil * TILE, rows_q, jnp.concatenate(o_t, axis=0).T)
            store(lg_ref, g, dil * i * TILE, dil * TILE, rows_q, jnp.concatenate(l_t, axis=0).T)
            return carry

        for r in range(dil):
            if n_t == 1:
                body(0, 0, r)
            else:
                lax.fori_loop(0, n_t, functools.partial(body, r=r), 0, unroll=2)

    piece = 256

    def combine(c, carry):
        rows = pl.ds(pl.multiple_of(c * piece, piece), piece)
        both = lambda ref, g: jnp.concatenate([ref[g, 0, rows, :], ref[g, 1, rows, :]], axis=1)
        ls = [both(lg_ref, g) for g in range(len(DILATIONS))]
        mx = functools.reduce(jnp.maximum, ls)
        es = [jnp.exp2(x - mx) for x in ls]
        den = functools.reduce(lambda a, b: a + b, es)
        y = functools.reduce(lambda a, b: a + b, [(e / den) * both(og_ref, g) for g, e in enumerate(es)])
        o_ref[0, rows, :] = y * lax.rsqrt(jnp.mean(y * y, axis=-1, keepdims=True) + RMS_EPS) * beta_ref[...]
        return carry

    lax.fori_loop(0, DIL_SPAN // piece, combine, 0)


def _dilated(proj, beta):
    b, s, _ = proj.shape
    gw = GROUP_WIDTH
    assert s % DIL_SPAN == 0 and s >= 2 * TILE * max(DILATIONS)
    lane_block = lambda n: pl.BlockSpec((1, s, TILE), lambda i, j: (i, 0, C_QD // TILE + n))
    return pl.pallas_call(
        _dilated_kernel,
        grid=(b, s // DIL_SPAN),
        in_specs=[lane_block(n) for n in range(3 * gw // TILE)] + [pl.BlockSpec((1, gw), lambda i, j: (0, 0))],
        out_specs=pl.BlockSpec((1, DIL_SPAN, gw), lambda i, j: (i, j, 0)),
        out_shape=jax.ShapeDtypeStruct((b, s, gw), F32),
        scratch_shapes=[pltpu.VMEM((len(DILATIONS), gw // TILE, DIL_SPAN, TILE), F32)] * 2,
        compiler_params=_cparams("parallel", "arbitrary"),
        name="dilated",
    )(*([proj] * (3 * gw // TILE)), beta)


def _rwkv_prep_kernel(cur_ref, halo_ref, mu_ref, w0_ref, wup_ref, a0_ref, aup_ref, gup_ref, kk_w_ref, ka_ref,
                      rk_ref, ones_ref,
                      kk_o, w_o, b_o, k_o, yr_o, vkr_o, v_o, g_o, c_o):
    gw = GROUP_WIDTH
    tm = cur_ref.shape[1]
    ones = ones_ref[...]
    x = cur_ref[0]
    last = halo_ref[0][7:8, :]
    last = jnp.where(pl.program_id(1) > 0, last, 0.0)
    rid = lax.broadcasted_iota(jnp.int32, (tm, 1), 0)
    xprev = jnp.where(rid == 0, last, pltpu.roll(x, 1, axis=0))
    xs = x + (xprev - x) * mu_ref[...]
    r, k, v = xs[:, 0:gw], xs[:, gw:2 * gw], xs[:, 2 * gw:3 * gw]
    wd, ad, gd = xs[:, 3 * gw:3 * gw + 64], xs[:, 3 * gw + 64:3 * gw + 128], xs[:, 3 * gw + 128:4 * gw]
    z = -(w0_ref[...] + _dot(jnp.tanh(wd).astype(BF16), wup_ref[...]))
    softplus = jnp.maximum(z, 0.0) + jnp.log1p(jnp.exp(-jnp.abs(z)))
    w = jnp.exp(-jnp.exp(-softplus - 0.5))
    a = _sigmoid(a0_ref[...] + _dot(ad.astype(BF16), aup_ref[...]))
    g = _dot(_sigmoid(gd).astype(BF16), gup_ref[...])
    kk = k * kk_w_ref[...]
    kk = kk * lax.rsqrt(_seg_sum(kk * kk, ones) + 1e-12)
    k2 = k * (1.0 + (a - 1.0) * ka_ref[...])
    b = kk * a
    br = _seg_sum(b * r, ones)
    kr = _seg_sum(k2 * r, ones)
    kk_o[0] = kk
    w_o[0] = w
    b_o[0] = b
    k_o[0] = k2
    yr_o[0] = w * r - kk * br
    vkr_o[0] = v * kr
    v_o[0] = v
    g_o[0] = g
    c_o[0] = _seg_sum(r * k2 * rk_ref[...], ones)


def _rwkv_prep(proj, p, ones, tm=256):
    b, s, _ = proj.shape
    gw = GROUP_WIDTH
    vec = lambda n: pl.BlockSpec((1, n), lambda i, j: (0, 0))
    mat = lambda a: pl.BlockSpec(a.shape, lambda i, j: (0, 0))
    out = pl.BlockSpec((1, tm, gw), lambda i, j: (i, j, 0))
    return pl.pallas_call(
        _rwkv_prep_kernel,
        grid=(b, s // tm),
        in_specs=[pl.BlockSpec((1, tm, 4 * gw), lambda i, j: (i, j, C_PC // (4 * gw))),
                  pl.BlockSpec((1, 8, 4 * gw), lambda i, j: (i, jnp.maximum(j * (tm // 8) - 1, 0), C_PC // (4 * gw))),
                  vec(4 * gw), vec(gw), mat(p["w_up"]), vec(gw), mat(p["a_up"]), mat(p["g_up"]),
                  vec(gw), vec(gw), vec(gw), mat(ones)],
        out_specs=[out] * 9,
        out_shape=[jax.ShapeDtypeStruct((b, s, gw), F32)] * 9,
        compiler_params=_cparams("parallel", "arbitrary"),
        name="rwkv_prep",
    )(proj, proj, p["mu"], p["w0"], p["w_up"], p["a0"], p["a_up"], p["g_up"], p["k_k"], p["k_a"], p["r_k"], ones)


def _rwkv_scan_kernel(kk_ref, w_ref, b_ref, k_ref, yr_ref, vkr_ref, v_ref, ones_ref, diag_ref,
                      yt_ref, state_ref, yacc_ref):
    nb, tb = kk_ref.shape[0], kk_ref.shape[1]

    @pl.when(pl.program_id(0) == 0)
    def _():
        state_ref[...] = jnp.zeros(state_ref.shape, F32)
        yacc_ref[...] = jnp.zeros(yacc_ref.shape, F32)

    ones = ones_ref[...]
    diag = diag_ref[...].astype(BF16)
    lane = jnp.bitwise_and(lax.broadcasted_iota(jnp.int32, (HEAD_DIM, GROUP_WIDTH), 1), HEAD_DIM - 1)

    def group(c, carry):
        def step(tt, carry2):
            t = c * HEAD_DIM + tt
            here = lane == tt
            row = lambda ref, i: ref[i, pl.ds(t, 1), :]
            st = [state_ref[i] for i in range(nb)]
            sb = [st[i].astype(BF16) for i in range(nb)]
            rb = lambda ref, i: row(ref, i).astype(BF16)
            reduce = lambda parts: _dot(jnp.concatenate(parts, axis=0), ones)
            red_v = reduce([rb(v_ref, i) * diag for i in range(nb)])
            red_s = reduce([sb[i] * rb(kk_ref, i) for i in range(nb)])
            red_y = reduce([sb[i] * rb(yr_ref, i) + rb(vkr_ref, i) * diag for i in range(nb)])
            part = lambda red, n: red[n * HEAD_DIM:(n + 1) * HEAD_DIM, :]
            for i in range(nb):
                sa, vcol, ycol = part(red_s, i), part(red_v, i), part(red_y, i)
                state_ref[i] = st[i] * row(w_ref, i) - sa * row(b_ref, i) + vcol * row(k_ref, i)
                yacc_ref[i] = jnp.where(here, ycol, yacc_ref[i])
            return carry2

        lax.fori_loop(0, HEAD_DIM, step, 0, unroll=SCAN_UNROLL)
        for i in range(nb):
            yt_ref[i, c] = yacc_ref[i]
        return carry

    lax.fori_loop(0, tb // HEAD_DIM, group, 0)


def _rwkv_scan(kk, w, b, k, yr, vkr, v, ones, diag, tb=128):
    nb, s, gw = kk.shape
    inp = pl.BlockSpec((nb, tb, gw), lambda j: (0, j, 0))
    const = lambda a: pl.BlockSpec(a.shape, lambda j: (0, 0))
    yt = pl.pallas_call(
        _rwkv_scan_kernel,
        grid=(s // tb,),
        in_specs=[inp] * 7 + [const(ones), const(diag)],
        out_specs=pl.BlockSpec((nb, tb // HEAD_DIM, HEAD_DIM, gw), lambda j: (0, j, 0, 0)),
        out_shape=jax.ShapeDtypeStruct((nb, s // HEAD_DIM, HEAD_DIM, gw), F32),
        scratch_shapes=[pltpu.VMEM((nb, HEAD_DIM, gw), F32), pltpu.VMEM((nb, HEAD_DIM, gw), F32)],
        compiler_params=_cparams("arbitrary"),
        name="rwkv_scan",
    )(kk, w, b, k, yr, vkr, v, ones, diag)
    yt = yt.reshape(nb, s // HEAD_DIM, HEAD_DIM, GROUP_HEADS, HEAD_DIM)
    return yt.transpose(0, 1, 4, 3, 2).reshape(nb, s, gw)


def _conformer_kernel(cur_ref, halo_ref, dw_ref, dwb_ref, lng_ref, lnb_ref, o_ref, g_ref):
    gw = GROUP_WIDTH
    tm = cur_ref.shape[1]
    hl = halo_ref.shape[1]
    glu = lambda x: x[:, 0:gw] * _sigmoid(x[:, gw:2 * gw])
    g_ref[0:hl, :] = jnp.where(pl.program_id(1) > 0, glu(halo_ref[0]), 0.0)
    g_ref[hl:hl + tm, :] = glu(cur_ref[0])
    acc = jnp.zeros((tm, gw), F32)
    for j in range(CONV_K):
        acc = acc + dw_ref[j:j + 1, :] * g_ref[pl.ds(hl - (CONV_K - 1) + j, tm), :]
    h = acc + dwb_ref[...]
    mu = jnp.mean(h, axis=-1, keepdims=True)
    d = h - mu
    var = jnp.mean(d * d, axis=-1, keepdims=True)
    o_ref[0] = _silu(d * lax.rsqrt(var + LN_EPS) * lng_ref[...] + lnb_ref[...])


def _conformer(proj, dw, dwb, lng, lnb, tm=512, hl=32):
    b, s, _ = proj.shape
    gw = GROUP_WIDTH
    vec = pl.BlockSpec((1, gw), lambda i, j: (0, 0))
    return pl.pallas_call(
        _conformer_kernel,
        grid=(b, s // tm),
        in_specs=[pl.BlockSpec((1, tm, 2 * gw), lambda i, j: (i, j, C_PD // (2 * gw))),
                  pl.BlockSpec((1, hl, 2 * gw), lambda i, j: (i, jnp.maximum(j * (tm // hl) - 1, 0), C_PD // (2 * gw))),
                  pl.BlockSpec((CONV_K, gw), lambda i, j: (0, 0)), vec, vec, vec],
        out_specs=pl.BlockSpec((1, tm, gw), lambda i, j: (i, j, 0)),
        out_shape=jax.ShapeDtypeStruct((b, s, gw), F32),
        scratch_shapes=[pltpu.VMEM((hl + tm, gw), F32)],
        compiler_params=_cparams("parallel", "arbitrary"),
        name="conformer_conv",
    )(proj, proj, dw, dwb, lng, lnb)


def _out_proj_kernel(x_ref, ya_ref, yb_ref, ys_ref, c_ref, v_ref, g_ref, yd_ref, lng_ref, lnb_ref, ones_ref, w_ref,
                     o_ref):
    gw = GROUP_WIDTH
    ones = ones_ref[...]
    y = ys_ref[...] + c_ref[...] * v_ref[...]
    mu = _seg_sum(y, ones) * (1.0 / HEAD_DIM)
    d = y - mu
    var = _seg_sum(d * d, ones) * (1.0 / HEAD_DIM)
    yc = (d * lax.rsqrt(var + RWKV_GN_EPS) * lng_ref[...] + lnb_ref[...]) * g_ref[...]
    acc = x_ref[...]
    for n, part in enumerate((ya_ref[...], yb_ref[...], yc, yd_ref[...])):
        acc = acc + _dot(part.astype(BF16), w_ref[n * gw:(n + 1) * gw, :])
    o_ref[...] = acc


def _out_proj(x2, parts, vecs, ones, w, tm=512):
    m, d = x2.shape
    gw = GROUP_WIDTH
    blk = pl.BlockSpec((tm, gw), lambda i: (i, 0))
    vec = pl.BlockSpec((1, gw), lambda i: (0, 0))
    return pl.pallas_call(
        _out_proj_kernel,
        grid=(m // tm,),
        in_specs=[pl.BlockSpec((tm, d), lambda i: (i, 0))] + [blk] * len(parts) + [vec] * len(vecs)
                 + [pl.BlockSpec(ones.shape, lambda i: (0, 0)), pl.BlockSpec(w.shape, lambda i: (0, 0))],
        out_specs=pl.BlockSpec((tm, d), lambda i: (i, 0)),
        out_shape=jax.ShapeDtypeStruct((m, d), F32),
        compiler_params=_cparams("parallel"),
        name="out_proj",
    )(x2, *parts, *vecs, ones, w)


def _ffn_kernel(x_ref, halo_ref, gain_ref, upg_ref, upv_ref, dwg_ref, dwv_ref, bg_ref, bv_ref, down_ref, fin_ref,
                o_ref, h_ref, acc_ref, ug_ref, uv_ref, *, tiles_per_seq, final_norm):
    j = pl.program_id(1)
    tm = x_ref.shape[0]
    hl = halo_ref.shape[0]

    def norm(x):
        return (x * lax.rsqrt(jnp.mean(x * x, axis=-1, keepdims=True) + RMS_EPS) * gain_ref[...]).astype(BF16)

    @pl.when(j == 0)
    def _():
        first = pl.program_id(0) % tiles_per_seq == 0
        h_ref[0:hl, :] = jnp.where(first, jnp.zeros((hl, x_ref.shape[1]), BF16), norm(halo_ref[...]))
        h_ref[hl:hl + tm, :] = norm(x_ref[...])
        acc_ref[...] = jnp.zeros(acc_ref.shape, F32)

    h = h_ref[...]

    def conv(up_ref, dw_ref, b_ref, u_ref):
        u_ref[...] = _dot(h, up_ref[...])
        out = b_ref[...]
        for tap in range(FFN_CONV_K):
            out = out + dw_ref[tap:tap + 1, :] * u_ref[pl.ds(hl - (FFN_CONV_K - 1) + tap, tm), :]
        return out

    act = _silu(conv(upg_ref, dwg_ref, bg_ref, ug_ref)) * conv(upv_ref, dwv_ref, bv_ref, uv_ref)
    acc_ref[...] += _dot(act.astype(BF16), down_ref[...])

    @pl.when(j == pl.num_programs(1) - 1)
    def _():
        y = x_ref[...] + acc_ref[...]
        if final_norm:
            y = y * lax.rsqrt(jnp.mean(y * y, axis=-1, keepdims=True) + RMS_EPS) * fin_ref[...]
        o_ref[...] = y


def _ffn(x2, gain, up, dw, dwb, down, fin, seq_len, final_norm, tm=512, tn=1408, hl=16):
    m, d = x2.shape
    dff = down.shape[0]
    nj = dff // tn
    kern = functools.partial(_ffn_kernel, tiles_per_seq=seq_len // tm, final_norm=final_norm)
    return pl.pallas_call(
        kern,
        grid=(m // tm, nj),
        in_specs=[pl.BlockSpec((tm, d), lambda i, j: (i, 0)),
                  pl.BlockSpec((hl, d), lambda i, j: (jnp.maximum(i * (tm // hl) - 1, 0), 0)),
                  pl.BlockSpec((1, d), lambda i, j: (0, 0)),
                  pl.BlockSpec((d, tn), lambda i, j: (0, j)),
                  pl.BlockSpec((d, tn), lambda i, j: (0, j + nj)),
                  pl.BlockSpec((FFN_CONV_K, tn), lambda i, j: (0, j)),
                  pl.BlockSpec((FFN_CONV_K, tn), lambda i, j: (0, j + nj)),
                  pl.BlockSpec((1, tn), lambda i, j: (0, j)),
                  pl.BlockSpec((1, tn), lambda i, j: (0, j + nj)),
                  pl.BlockSpec((tn, d), lambda i, j: (j, 0)),
                  pl.BlockSpec((1, d), lambda i, j: (0, 0))],
        out_specs=pl.BlockSpec((tm, d), lambda i, j: (i, 0)),
        out_shape=jax.ShapeDtypeStruct((m, d), F32),
        scratch_shapes=[pltpu.VMEM((hl + tm, d), BF16), pltpu.VMEM((tm, d), F32),
                        pltpu.VMEM((hl + tm, tn), F32), pltpu.VMEM((hl + tm, tn), F32)],
        compiler_params=_cparams("parallel", "arbitrary"),
        name="conv_ffn",
    )(x2, x2, gain, up, up, dw, dw, dwb, dwb, down, fin)


def _block_ones():
    i = np.arange(GROUP_WIDTH) // HEAD_DIM
    return jnp.asarray(i[:, None] == i[None, :], BF16)


def _diag_mask():
    v = np.arange(HEAD_DIM)
    l = np.arange(GROUP_WIDTH) % HEAD_DIM
    return jnp.asarray(v[:, None] == l[None, :], F32)


def _importance_map(n_cmp, n_blk):
    gt = np.zeros((n_blk, n_cmp), np.float32)
    per = SLC_BLOCK // CMP_STRIDE
    for c in range(n_blk * per):
        for i in (c - 1, c):
            if 0 <= i < n_cmp - 1:
                gt[c // per, i] += 1.0
    return jnp.asarray(gt, BF16)


def kernel(x, norm_mix, w_in, cmp_pos, cmp_k_w1, cmp_k_w2, cmp_v_w1, cmp_v_w2, beta_nsa, beta_dil, rwkv_mu, rwkv_w0, rwkv_w_up, rwkv_a0, rwkv_a_up, rwkv_g_up, rwkv_k_k, rwkv_k_a, rwkv_r_k, rwkv_ln_g, rwkv_ln_b, conv_dw, conv_dw_b, conv_ln_g, conv_ln_b, w_out, norm_ffn, ffn_up, ffn_dw, ffn_dw_b, ffn_down, norm_final):
    bsz, seq, d = x.shape
    depth = w_in.shape[0]
    gw = GROUP_WIDTH
    n_cmp = seq // CMP_STRIDE
    n_blk = seq // SLC_BLOCK
    half = CMP_STRIDE * HEAD_DIM

    cols, kv_cols = _proj_columns()
    w_in_p = w_in[:, :, cols].astype(BF16)
    w_kv_p = w_in[:, :, kv_cols].astype(BF16)
    w_out_b = w_out.astype(BF16)
    ffn_up_b = ffn_up.astype(BF16)
    ffn_down_b = ffn_down.astype(BF16)
    ones = _block_ones()
    diag = _diag_mask()
    gt = _importance_map(n_cmp, n_blk)
    row = lambda a: a.reshape(1, -1)

    x2 = x.reshape(bsz * seq, d)
    for l in range(depth):
        proj = _proj_in(x2, row(norm_mix[l]), w_in_p[l], tn=N_PROJ // 2).reshape(bsz, seq, N_PROJ)
        kv = _proj_in(x2, row(norm_mix[l]), w_kv_p[l], tn=N_KV).reshape(bsz, seq, N_KV)

        kh = kv[:, :, KV_KC:KV_KC + HEAD_DIM].reshape(bsz, n_cmp, half)
        vh = kv[:, :, KV_VC:KV_VC + HEAD_DIM].reshape(bsz, n_cmp, half)
        pos = cmp_pos[l].reshape(1, 2 * half)
        k_cmp, v_cmp = _compress(kh, vh, pos[:, :half], pos[:, half:],
                                 cmp_k_w1[l].astype(BF16), cmp_k_w2[l].astype(BF16),
                                 cmp_v_w1[l].astype(BF16), cmp_v_w2[l].astype(BF16))
        ocmp, sel_t = _nsa_cmp(proj, k_cmp, jnp.swapaxes(v_cmp, 1, 2), gt)
        y_a = _nsa_flash(proj, kv, ocmp, sel_t, row(beta_nsa[l]))

        y_b = _dilated(proj, row(beta_dil[l]))

        rp = dict(mu=row(rwkv_mu[l]), w0=row(rwkv_w0[l]), w_up=rwkv_w_up[l].astype(BF16), a0=row(rwkv_a0[l]),
                  a_up=rwkv_a_up[l].astype(BF16), g_up=rwkv_g_up[l].astype(BF16), k_k=row(rwkv_k_k[l]),
                  k_a=row(rwkv_k_a[l]), r_k=row(rwkv_r_k[l]))
        kk, w, b, k2, yr, vkr, v, g, c = _rwkv_prep(proj, rp, ones)
        y_scan = _rwkv_scan(kk, w, b, k2, yr, vkr, v, ones, diag)

        y_d = _conformer(proj, conv_dw[l], row(conv_dw_b[l]), row(conv_ln_g[l]), row(conv_ln_b[l]))

        flat = lambda a: a.reshape(bsz * seq, gw)
        parts = [flat(a) for a in (y_a, y_b, y_scan, c, v, g, y_d)]
        vecs = [row(rwkv_ln_g[l]), row(rwkv_ln_b[l])]
        x2 = _out_proj(x2, parts, vecs, ones, w_out_b[l])

        x2 = _ffn(x2, row(norm_ffn[l]), ffn_up_b[l], ffn_dw[l], row(ffn_dw_b[l]), ffn_down_b[l],
                  row(norm_final), seq, final_norm=(l == depth - 1))
    return x2.reshape(bsz, seq, d)
```

```python
import functools

import jax
import jax.numpy as jnp
import numpy as np
from jax import lax
from jax.experimental import pallas as pl
from jax.experimental.pallas import tpu as pltpu

F32 = jnp.float32
BF16 = jnp.bfloat16

HEAD_DIM = 64
GROUP_HEADS = 4
GROUP_WIDTH = HEAD_DIM * GROUP_HEADS
CMP_STRIDE = 16
CMP_BLOCK = 32
SLC_BLOCK = 64
N_SEL_BLOCKS = 16
NSA_WINDOW = 512
DILATIONS = (1, 4, 16)
TILE = 128
CONV_K = 31
FFN_CONV_K = 3
RMS_EPS = 1e-6
LN_EPS = 1e-5
RWKV_GN_EPS = 64e-5
NEG = -1e30
FORCE_SCORE = 1e9
SCALE = HEAD_DIM ** -0.5
LOG2_E = 1.4426950408889634
SCAN_UNROLL = 8
SLC_CHUNK = 4
V_ROWS = HEAD_DIM + 16

C_PC, C_PD, C_QD, C_QN, C_G0, C_G1, C_G2 = 0, 1024, 1536, 2304, 2560, 2816, 3072
N_PROJ = 3328
KV_KC, KV_VC, KV_KS, KV_VS, KV_KW, KV_VW = 0, 64, 128, 192, 256, 320
N_KV = 384
DIL_SPAN = 2048

VMEM_LIMIT = 48 * 1024 * 1024


def _cparams(*sem):
    return pltpu.CompilerParams(dimension_semantics=sem, vmem_limit_bytes=VMEM_LIMIT)


def _proj_columns():
    gw, hd = GROUP_WIDTH, HEAD_DIM
    a0 = 0
    q = np.arange(a0, a0 + gw)
    kc, vc, ks, vs, kw, vw = [np.arange(a0 + gw + i * hd, a0 + gw + (i + 1) * hd) for i in range(6)]
    gate0 = a0 + gw + 6 * hd
    gates = [np.repeat(gate0 + 3 * np.arange(GROUP_HEADS) + j, hd) for j in range(3)]
    b0 = gate0 + 3 * GROUP_HEADS
    qd, kd, vd = [np.arange(b0 + i * gw, b0 + (i + 1) * gw) for i in range(3)]
    c0 = b0 + 3 * gw
    pc = np.arange(c0, c0 + 4 * gw)
    d0 = c0 + 4 * gw
    pd = np.arange(d0, d0 + 2 * gw)
    cols = np.concatenate([pc, pd, qd, kd, vd, q, gates[0], gates[1], gates[2]])
    kv_cols = np.concatenate([kc, vc, ks, vs, kw, vw])
    assert cols.shape[0] == N_PROJ and kv_cols.shape[0] == N_KV
    return cols, kv_cols


def _dot_nt(a, b):
    return lax.dot_general(a, b, (((1,), (1,)), ((), ())), preferred_element_type=F32)


def _dot(a, b):
    return jnp.dot(a, b, preferred_element_type=F32)


def _seg_sum(x, ones_bf16):
    hi = x.astype(BF16)
    lo = (x - hi.astype(F32)).astype(BF16)
    return _dot(hi, ones_bf16) + _dot(lo, ones_bf16)


def _sigmoid(x):
    return 1.0 / (1.0 + jnp.exp(-x))


def _silu(x):
    return x * _sigmoid(x)


def _proj_in_kernel(x_ref, g_ref, w_ref, o_ref, h_ref):
    @pl.when(pl.program_id(1) == 0)
    def _():
        x = x_ref[...]
        ms = jnp.mean(x * x, axis=-1, keepdims=True)
        h_ref[...] = (x * lax.rsqrt(ms + RMS_EPS) * g_ref[...]).astype(BF16)

    o_ref[...] = _dot(h_ref[...], w_ref[...])


def _proj_in(x2, gain, w, tn, tm=512):
    m, d = x2.shape
    n = w.shape[1]
    assert n % tn == 0
    return pl.pallas_call(
        _proj_in_kernel,
        grid=(m // tm, n // tn),
        in_specs=[pl.BlockSpec((tm, d), lambda i, j: (i, 0)),
                  pl.BlockSpec((1, d), lambda i, j: (0, 0)),
                  pl.BlockSpec((d, tn), lambda i, j: (0, j))],
        out_specs=pl.BlockSpec((tm, tn), lambda i, j: (i, j)),
        out_shape=jax.ShapeDtypeStruct((m, n), F32),
        scratch_shapes=[pltpu.VMEM((tm, d), BF16)],
        compiler_params=_cparams("parallel", "arbitrary"),
        name=f"proj_in_{n}",
    )(x2, gain, w)


def _compress_kernel(kh_ref, vh_ref, plo_ref, phi_ref, kw1_ref, kw2_ref, vw1_ref, vw2_ref, ko_ref, vo_ref):
    half = kh_ref.shape[2]
    n = kh_ref.shape[1]

    def mlp(xh, w1_ref, w2_ref):
        a = _dot((xh + plo_ref[...]).astype(BF16), w1_ref[0:half, :])
        bm = _dot((xh + phi_ref[...]).astype(BF16), w1_ref[half:2 * half, :])
        hid = a + pltpu.roll(bm, n - 1, axis=0)
        return _dot(_silu(hid).astype(BF16), w2_ref[...])

    ko_ref[0] = mlp(kh_ref[0], kw1_ref, kw2_ref)
    vo_ref[0] = mlp(vh_ref[0], vw1_ref, vw2_ref)


def _compress(kh, vh, pos_lo, pos_hi, kw1, kw2, vw1, vw2):
    b, n, half = kh.shape
    hid = kw1.shape[1]
    full = lambda shape: pl.BlockSpec(shape, lambda i: (0,) * len(shape))
    return pl.pallas_call(
        _compress_kernel,
        grid=(b,),
        in_specs=[pl.BlockSpec((1, n, half), lambda i: (i, 0, 0)),
                  pl.BlockSpec((1, n, half), lambda i: (i, 0, 0)),
                  full((1, half)), full((1, half)),
                  full((2 * half, hid)), full((hid, HEAD_DIM)),
                  full((2 * half, hid)), full((hid, HEAD_DIM))],
        out_specs=[pl.BlockSpec((1, n, HEAD_DIM), lambda i: (i, 0, 0))] * 2,
        out_shape=[jax.ShapeDtypeStruct((b, n, HEAD_DIM), F32)] * 2,
        compiler_params=_cparams("parallel"),
        name="nsa_compress",
    )(kh, vh, pos_lo, pos_hi, kw1, kw2, vw1, vw2)


def _nsa_cmp_kernel(q_ref, kc_ref, vct_ref, gt_ref, ocmp_ref, sel_ref, score_ref):
    t0 = pl.program_id(1) * TILE
    n_cmp = kc_ref.shape[1]
    n_blk = gt_ref.shape[0]
    heads = range(GROUP_HEADS)
    q_t = (q_ref[0] * SCALE).T.astype(BF16)
    qs_t = jnp.concatenate([q_t[HEAD_DIM * h:HEAD_DIM * (h + 1), :] for h in heads], axis=1)
    kc = kc_ref[0].astype(BF16)
    vc_t = vct_ref[0].astype(BF16)
    ci = lax.broadcasted_iota(jnp.int32, (n_cmp, TILE), 0)
    tq = t0 + lax.broadcasted_iota(jnp.int32, (n_cmp, TILE), 1)
    bias = jnp.where(jnp.where(ci < n_cmp - 1, CMP_STRIDE * ci + CMP_BLOCK - 1, 1 << 30) <= tq, 0.0, NEG)
    bias = jnp.concatenate([bias] * GROUP_HEADS, axis=1)
    s = _dot(kc, qs_t) + bias
    e = jnp.exp(s - jnp.max(s, axis=0, keepdims=True))
    p = jnp.where(bias == 0.0, e / jnp.sum(e, axis=0, keepdims=True), 0.0)
    imp_t = functools.reduce(lambda a, b: a + b, [p[:, TILE * h:TILE * (h + 1)] for h in heads])
    o_t = _dot(vc_t, p.astype(BF16))
    ocmp_ref[0] = jnp.concatenate([o_t[:, TILE * h:TILE * (h + 1)] for h in heads], axis=0).T

    hi = imp_t.astype(BF16)
    lo = (imp_t - hi.astype(F32)).astype(BF16)
    blk = _dot(gt_ref[...], hi) + _dot(gt_ref[...], lo)
    j = lax.broadcasted_iota(jnp.int32, (n_blk, TILE), 0)
    t = t0 + lax.broadcasted_iota(jnp.int32, (n_blk, TILE), 1)
    cur = lax.shift_right_logical(t, 6)
    forced = jnp.where(j == 0, 1, jnp.where(j == cur, 1, jnp.where(j == cur - 1, 1, 0)))
    score = jnp.where(j * SLC_BLOCK <= t, jnp.where(forced == 1, FORCE_SCORE, blk), NEG)
    score_ref[...] = score

    def count(jp, rank):
        row = score_ref[pl.ds(jp, 1), :]
        return rank + jnp.where(row > score, 1, jnp.where(row == score, jnp.where(j > jp, 1, 0), 0))

    n_valid = jnp.minimum((t0 + TILE) // SLC_BLOCK, n_blk)
    rank = lax.fori_loop(0, n_valid, count, jnp.zeros((n_blk, TILE), jnp.int32))
    sel_ref[0] = jnp.where(rank < N_SEL_BLOCKS, 0.0, NEG)


def _nsa_cmp(proj, k_cmp, v_cmp_t, gt):
    b, s, _ = proj.shape
    n_cmp = k_cmp.shape[1]
    n_blk = gt.shape[0]
    return pl.pallas_call(
        _nsa_cmp_kernel,
        grid=(b, s // TILE),
        in_specs=[pl.BlockSpec((1, TILE, GROUP_WIDTH), lambda i, j: (i, j, C_QN // GROUP_WIDTH)),
                  pl.BlockSpec((1, n_cmp, HEAD_DIM), lambda i, j: (i, 0, 0)),
                  pl.BlockSpec((1, HEAD_DIM, n_cmp), lambda i, j: (i, 0, 0)),
                  pl.BlockSpec(gt.shape, lambda i, j: (0, 0))],
        out_specs=[pl.BlockSpec((1, TILE, GROUP_WIDTH), lambda i, j: (i, j, 0)),
                   pl.BlockSpec((1, n_blk, TILE), lambda i, j: (i, 0, j))],
        out_shape=[jax.ShapeDtypeStruct((b, s, GROUP_WIDTH), F32),
                   jax.ShapeDtypeStruct((b, n_blk, s), F32)],
        scratch_shapes=[pltpu.VMEM((n_blk, TILE), F32)],
        compiler_params=_cparams("parallel", "parallel"),
        name="nsa_cmp",
    )(proj, k_cmp, v_cmp_t, gt)


def _nsa_flash_kernel(q_ref, ocmp_ref, sel_ref, g0_ref, g1_ref, g2_ref, ks_ref, vs_ref, kw_ref, vw_ref, beta_ref,
                      o_ref, sa_ref, sb_ref, sw_ref):
    qi = pl.program_id(1)
    cols = GROUP_HEADS * TILE
    q_t = (q_ref[0] * (SCALE * LOG2_E)).T
    qs_t = jnp.concatenate([q_t[HEAD_DIM * h:HEAD_DIM * (h + 1), :] for h in range(GROUP_HEADS)], axis=1)
    sel4 = jnp.concatenate([sel_ref[0]] * GROUP_HEADS, axis=1)
    w = jnp.concatenate([qs_t, sel4], axis=0).astype(BF16)
    q_pos = qi * TILE + lax.broadcasted_iota(jnp.int32, (1, TILE), 1)

    def keys(ref, start, n):
        return ref[0, pl.ds(start, n)].reshape(n * TILE, 2 * HEAD_DIM)

    def values_t(ref, start, n):
        v = ref[0, pl.ds(start, n)]
        return jnp.concatenate([v[u] for u in range(n)], axis=1)

    def band_bias(start, n, window):
        key_pos = start * TILE + lax.broadcasted_iota(jnp.int32, (n * TILE, TILE), 0)
        dist = q_pos - key_pos
        ok = jnp.where(dist >= 0, 1, 0) if window is None else jnp.where(dist >= 0, jnp.where(dist <= window, 1, 0), 0)
        return jnp.concatenate([jnp.where(ok == 1, 0.0, NEG)] * GROUP_HEADS, axis=1)

    def col_max(s):
        return jnp.max(s, axis=0, keepdims=True)

    def finish(acc):
        o_t = acc[0:HEAD_DIM, :] / acc[HEAD_DIM:HEAD_DIM + 1, :]
        return jnp.concatenate([o_t[:, TILE * h:TILE * (h + 1)] for h in range(GROUP_HEADS)], axis=0).T

    n_win = NSA_WINDOW // TILE + 1
    first = jnp.maximum(qi - (n_win - 1), 0)
    sw_ref[...] = _dot(keys(kw_ref, first, n_win), w) + band_bias(first, n_win, NSA_WINDOW)
    p = jnp.exp2(sw_ref[...] - col_max(sw_ref[...])).astype(BF16)
    o_win = finish(_dot(values_t(vw_ref, first, n_win), p))

    def scores(c):
        return _dot(keys(ks_ref, c * SLC_CHUNK, SLC_CHUNK), w) + band_bias(c * SLC_CHUNK, SLC_CHUNK, None)

    def prefetch(ref, c, extra=0.0):
        s = scores(c) + extra
        ref[...] = s
        return col_max(s)

    def consume(ref, c, m, cm, acc):
        m_new = jnp.maximum(m, cm)
        p = jnp.exp2(ref[...] - m_new).astype(BF16)
        return m_new, jnp.exp2(m - m_new) * acc + _dot(values_t(vs_ref, c * SLC_CHUNK, SLC_CHUNK), p)

    def two_chunks(j, carry):
        m, cm_a, acc = carry
        cm_b = prefetch(sb_ref, 2 * j + 1)
        m, acc = consume(sa_ref, 2 * j, m, cm_a, acc)
        cm_a = prefetch(sa_ref, 2 * j + 2)
        m, acc = consume(sb_ref, 2 * j + 1, m, cm_b, acc)
        return m, cm_a, acc

    n_full = qi // SLC_CHUNK
    n_pairs = n_full // 2
    m, cm_a, acc = lax.fori_loop(0, n_pairs, two_chunks,
                                 (jnp.full((1, cols), NEG, F32), prefetch(sa_ref, 0), jnp.zeros((V_ROWS, cols), F32)))
    odd = 2 * n_pairs + 1
    cm_b = prefetch(sb_ref, jnp.minimum(odd, n_full), jnp.where(odd <= n_full, 0.0, NEG))
    m, acc = consume(sa_ref, 2 * n_pairs, m, cm_a, acc)
    o_slc = finish(consume(sb_ref, jnp.minimum(odd, n_full), m, cm_b, acc)[1])

    y = (_sigmoid(g0_ref[0]) * ocmp_ref[0] + _sigmoid(g1_ref[0]) * o_slc + _sigmoid(g2_ref[0]) * o_win)
    ms = jnp.mean(y * y, axis=-1, keepdims=True)
    o_ref[0] = y * lax.rsqrt(ms + RMS_EPS) * beta_ref[...]


def _key_tiles(k, onehot):
    b, s, _ = k.shape
    aug = jnp.zeros((b, s, HEAD_DIM), BF16) if onehot is None else jnp.broadcast_to(onehot, (b, s, HEAD_DIM))
    return jnp.concatenate([k.astype(BF16), aug], axis=-1).reshape(b, s // TILE, TILE, 2 * HEAD_DIM)


def _value_tiles_t(v):
    b, s, _ = v.shape
    v_t = v.astype(BF16).reshape(b, s // TILE, TILE, HEAD_DIM).transpose(0, 1, 3, 2)
    pad = jnp.zeros((b, s // TILE, V_ROWS - HEAD_DIM, TILE), BF16).at[:, :, 0, :].set(1.0)
    return jnp.concatenate([v_t, pad], axis=2)


def _nsa_flash(proj, kv, ocmp, sel_t, beta):
    b, s, _ = proj.shape
    gw = GROUP_WIDTH
    n_blk = sel_t.shape[1]
    assert n_blk == HEAD_DIM, "the selection rows share the 128-deep contraction with the 64 head dims"
    piece = lambda c: kv[:, :, c:c + HEAD_DIM]
    blk_of_key = np.arange(s)[:, None] // SLC_BLOCK == np.arange(n_blk)[None, :]
    ks = _key_tiles(piece(KV_KS), jnp.asarray(blk_of_key, BF16))
    kw = _key_tiles(piece(KV_KW), None)
    vs = _value_tiles_t(piece(KV_VS))
    vw = _value_tiles_t(piece(KV_VW))
    nt = s // TILE
    col = lambda c: pl.BlockSpec((1, TILE, gw), lambda i, j: (i, j, c // gw))
    ktile = pl.BlockSpec((1, nt, TILE, 2 * HEAD_DIM), lambda i, j: (i, 0, 0, 0))
    vtile = pl.BlockSpec((1, nt, V_ROWS, TILE), lambda i, j: (i, 0, 0, 0))
    assert nt % SLC_CHUNK == 0 and nt >= NSA_WINDOW // TILE + 1
    return pl.pallas_call(
        _nsa_flash_kernel,
        grid=(b, nt),
        in_specs=[col(C_QN),
                  pl.BlockSpec((1, TILE, gw), lambda i, j: (i, j, 0)),
                  pl.BlockSpec((1, n_blk, TILE), lambda i, j: (i, 0, j)),
                  col(C_G0), col(C_G1), col(C_G2), ktile, vtile, ktile, vtile,
                  pl.BlockSpec((1, gw), lambda i, j: (0, 0))],
        out_specs=pl.BlockSpec((1, TILE, gw), lambda i, j: (i, j, 0)),
        out_shape=jax.ShapeDtypeStruct((b, s, gw), F32),
        scratch_shapes=[pltpu.VMEM((SLC_CHUNK * TILE, GROUP_HEADS * TILE), F32),
                        pltpu.VMEM((SLC_CHUNK * TILE, GROUP_HEADS * TILE), F32),
                        pltpu.VMEM((NSA_WINDOW + TILE, GROUP_HEADS * TILE), F32)],
        compiler_params=_cparams("parallel", "arbitrary"),
        name="nsa_flash",
    )(proj, ocmp, sel_t, proj, proj, proj, ks, vs, kw, vw, beta)


def _dilated_kernel(q0_ref, q1_ref, k0_ref, k1_ref, v0_ref, v1_ref, beta_ref, o_ref, og_ref, lg_ref):
    gw = GROUP_WIDTH
    cols = GROUP_HEADS * TILE
    p0 = pl.program_id(1) * DIL_SPAN

    def load(refs, first, n, rows):
        window = pl.ds(pl.multiple_of(first, TILE), n)
        return jnp.concatenate([ref.at[0, window][rows, :] for ref in refs], axis=1)

    def store(ref, g, first, n, rows, val):
        window = pl.ds(pl.multiple_of(first, TILE), n)
        for u in range(2):
            ref.at[g, u, window][rows, :] = val[:, TILE * u:TILE * (u + 1)]

    row_head = lax.shift_right_logical(lax.broadcasted_iota(jnp.int32, (gw, TILE), 0), 6)
    rel = (lax.broadcasted_iota(jnp.int32, (2 * TILE, TILE), 1)
           - lax.broadcasted_iota(jnp.int32, (2 * TILE, TILE), 0))

    for g, dil in enumerate(DILATIONS):
        n_t = DIL_SPAN // (TILE * dil)

        def body(i, carry, r, g=g, dil=dil):
            sub0 = p0 // dil + i * TILE
            ksub0 = jnp.maximum(sub0 - TILE, 0)
            rows_q = pl.ds(r, TILE, stride=dil)
            rows_k = pl.ds(r, 2 * TILE, stride=dil)
            keys = functools.partial(load, first=dil * ksub0, n=2 * dil * TILE, rows=rows_k)
            q_t = (load((q0_ref, q1_ref), dil * sub0, dil * TILE, rows_q) * (SCALE * LOG2_E)).T
            wq = jnp.concatenate([jnp.where(row_head == h, q_t, 0.0) for h in range(GROUP_HEADS)],
                                 axis=1).astype(BF16)
            dist = rel + (sub0 - ksub0)
            bias = jnp.where(dist >= 0, jnp.where(dist <= TILE, 0.0, NEG), NEG)
            bias = jnp.concatenate([bias] * GROUP_HEADS, axis=1)
            s = _dot(keys((k0_ref, k1_ref)).astype(BF16), wq) + bias
            m = jnp.max(s, axis=0, keepdims=True)
            p = jnp.exp2(s - m)
            l = jnp.sum(p, axis=0, keepdims=True)
            p = p.astype(BF16)
            v_t = keys((v0_ref, v1_ref)).T.astype(BF16)
            lse = m + jnp.log2(l)
            o_t, l_t = [], []
            for h in range(GROUP_HEADS):
                qs = slice(TILE * h, TILE * (h + 1))
                o_t.append(_dot(v_t[HEAD_DIM * h:HEAD_DIM * (h + 1), :], p[:, qs]) / l[:, qs])
                l_t.append(jnp.broadcast_to(lse[:, qs], (HEAD_DIM, TILE)))
            store(og_ref, g, dil * i * TILE, dil * TILE, rows_q, jnp.concatenate(o_t, axis=0).T)
            store(lg_ref, g, dil * i * TILE, dil * TILE, rows_q, jnp.concatenate(l_t, axis=0).T)
            return carry

        for r in range(dil):
            if n_t == 1:
                body(0, 0, r)
            else:
                lax.fori_loop(0, n_t, functools.partial(body, r=r), 0, unroll=4)

    piece = 256

    def combine(c, carry):
        rows = pl.ds(pl.multiple_of(c * piece, piece), piece)
        both = lambda ref, g: jnp.concatenate([ref[g, 0, rows, :], ref[g, 1, rows, :]], axis=1)
        ls = [both(lg_ref, g) for g in range(len(DILATIONS))]
        mx = functools.reduce(jnp.maximum, ls)
        es = [jnp.exp2(x - mx) for x in ls]
        den = functools.reduce(lambda a, b: a + b, es)
        y = functools.reduce(lambda a, b: a + b, [(e / den) * both(og_ref, g) for g, e in enumerate(es)])
        o_ref[0, rows, :] = y * lax.rsqrt(jnp.mean(y * y, axis=-1, keepdims=True) + RMS_EPS) * beta_ref[...]
        return carry

    lax.fori_loop(0, DIL_SPAN // piece, combine, 0)


def _dilated(proj, beta):
    b, s, _ = proj.shape
    gw = GROUP_WIDTH
    assert s % DIL_SPAN == 0 and s >= 2 * TILE * max(DILATIONS)
    lane_block = lambda n: pl.BlockSpec((1, s, TILE), lambda i, j: (i, 0, C_QD // TILE + n))
    return pl.pallas_call(
        _dilated_kernel,
        grid=(b, s // DIL_SPAN),
        in_specs=[lane_block(n) for n in range(3 * gw // TILE)] + [pl.BlockSpec((1, gw), lambda i, j: (0, 0))],
        out_specs=pl.BlockSpec((1, DIL_SPAN, gw), lambda i, j: (i, j, 0)),
        out_shape=jax.ShapeDtypeStruct((b, s, gw), F32),
        scratch_shapes=[pltpu.VMEM((len(DILATIONS), gw // TILE, DIL_SPAN, TILE), F32)] * 2,
        compiler_params=_cparams("parallel", "arbitrary"),
        name="dilated",
    )(*([proj] * (3 * gw // TILE)), beta)


def _rwkv_prep_kernel(cur_ref, halo_ref, mu_ref, w0_ref, wup_ref, a0_ref, aup_ref, gup_ref, kk_w_ref, ka_ref,
                      rk_ref, ones_ref,
                      kk_o, w_o, b_o, k_o, yr_o, vkr_o, v_o, g_o, c_o):
    gw = GROUP_WIDTH
    tm = cur_ref.shape[1]
    ones = ones_ref[...]
    x = cur_ref[0]
    last = halo_ref[0][7:8, :]
    last = jnp.where(pl.program_id(1) > 0, last, 0.0)
    rid = lax.broadcasted_iota(jnp.int32, (tm, 1), 0)
    xprev = jnp.where(rid == 0, last, pltpu.roll(x, 1, axis=0))
    xs = x + (xprev - x) * mu_ref[...]
    r, k, v = xs[:, 0:gw], xs[:, gw:2 * gw], xs[:, 2 * gw:3 * gw]
    wd, ad, gd = xs[:, 3 * gw:3 * gw + 64], xs[:, 3 * gw + 64:3 * gw + 128], xs[:, 3 * gw + 128:4 * gw]
    z = -(w0_ref[...] + _dot(jnp.tanh(wd).astype(BF16), wup_ref[...]))
    softplus = jnp.maximum(z, 0.0) + jnp.log1p(jnp.exp(-jnp.abs(z)))
    w = jnp.exp(-jnp.exp(-softplus - 0.5))
    a = _sigmoid(a0_ref[...] + _dot(ad.astype(BF16), aup_ref[...]))
    g = _dot(_sigmoid(gd).astype(BF16), gup_ref[...])
    kk = k * kk_w_ref[...]
    kk = kk * lax.rsqrt(_seg_sum(kk * kk, ones) + 1e-12)
    k2 = k * (1.0 + (a - 1.0) * ka_ref[...])
    b = kk * a
    br = _seg_sum(b * r, ones)
    kr = _seg_sum(k2 * r, ones)
    kk_o[0] = kk
    w_o[0] = w
    b_o[0] = b
    k_o[0] = k2
    yr_o[0] = w * r - kk * br
    vkr_o[0] = v * kr
    v_o[0] = v
    g_o[0] = g
    c_o[0] = _seg_sum(r * k2 * rk_ref[...], ones)


def _rwkv_prep(proj, p, ones, tm=256):
    b, s, _ = proj.shape
    gw = GROUP_WIDTH
    vec = lambda n: pl.BlockSpec((1, n), lambda i, j: (0, 0))
    mat = lambda a: pl.BlockSpec(a.shape, lambda i, j: (0, 0))
    out = pl.BlockSpec((1, tm, gw), lambda i, j: (i, j, 0))
    return pl.pallas_call(
        _rwkv_prep_kernel,
        grid=(b, s // tm),
        in_specs=[pl.BlockSpec((1, tm, 4 * gw), lambda i, j: (i, j, C_PC // (4 * gw))),
                  pl.BlockSpec((1, 8, 4 * gw), lambda i, j: (i, jnp.maximum(j * (tm // 8) - 1, 0), C_PC // (4 * gw))),
                  vec(4 * gw), vec(gw), mat(p["w_up"]), vec(gw), mat(p["a_up"]), mat(p["g_up"]),
                  vec(gw), vec(gw), vec(gw), mat(ones)],
        out_specs=[out] * 9,
        out_shape=[jax.ShapeDtypeStruct((b, s, gw), F32)] * 9,
        compiler_params=_cparams("parallel", "arbitrary"),
        name="rwkv_prep",
    )(proj, proj, p["mu"], p["w0"], p["w_up"], p["a0"], p["a_up"], p["g_up"], p["k_k"], p["k_a"], p["r_k"], ones)


def _rwkv_scan_kernel(kk_ref, w_ref, b_ref, k_ref, yr_ref, vkr_ref, v_ref, ones_ref, diag_ref,
                      y_ref, state_ref, yacc_ref):
    nb, tb = kk_ref.shape[0], kk_ref.shape[1]

    @pl.when(pl.program_id(0) == 0)
    def _():
        state_ref[...] = jnp.zeros(state_ref.shape, F32)
        yacc_ref[...] = jnp.zeros(yacc_ref.shape, F32)

    ones = ones_ref[...]
    diag = diag_ref[...].astype(BF16)
    lane = jnp.bitwise_and(lax.broadcasted_iota(jnp.int32, (HEAD_DIM, GROUP_WIDTH), 1), HEAD_DIM - 1)

    def group(c, carry):
        def step(tt, carry2):
            t = c * HEAD_DIM + tt
            here = lane == tt
            row = lambda ref, i: ref[i, pl.ds(t, 1), :]
            st = [state_ref[i] for i in range(nb)]
            sb = [st[i].astype(BF16) for i in range(nb)]
            rb = lambda ref, i: row(ref, i).astype(BF16)
            reduce = lambda parts: _dot(jnp.concatenate(parts, axis=0), ones)
            red_v = reduce([rb(v_ref, i) * diag for i in range(nb)])
            red_s = reduce([sb[i] * rb(kk_ref, i) for i in range(nb)])
            red_y = reduce([sb[i] * rb(yr_ref, i) + rb(vkr_ref, i) * diag for i in range(nb)])
            part = lambda red, n: red[n * HEAD_DIM:(n + 1) * HEAD_DIM, :]
            for i in range(nb):
                sa, vcol, ycol = part(red_s, i), part(red_v, i), part(red_y, i)
                state_ref[i] = st[i] * row(w_ref, i) - sa * row(b_ref, i) + vcol * row(k_ref, i)
                yacc_ref[i] = jnp.where(here, ycol, yacc_ref[i])
            return carry2

        lax.fori_loop(0, HEAD_DIM, step, 0, unroll=SCAN_UNROLL)
        rows = pl.ds(pl.multiple_of(c * HEAD_DIM, HEAD_DIM), HEAD_DIM)
        for i in range(nb):
            full_t = jnp.concatenate([yacc_ref[i], jnp.zeros((HEAD_DIM, GROUP_WIDTH), F32)], axis=0).T
            y_ref[i, rows, :] = jnp.concatenate(
                [full_t[HEAD_DIM * h:HEAD_DIM * (h + 1), 0:HEAD_DIM] for h in range(GROUP_HEADS)], axis=1)
        return carry

    lax.fori_loop(0, tb // HEAD_DIM, group, 0)


def _rwkv_scan(kk, w, b, k, yr, vkr, v, ones, diag, tb=128):
    nb, s, gw = kk.shape
    inp = pl.BlockSpec((nb, tb, gw), lambda j: (0, j, 0))
    const = lambda a: pl.BlockSpec(a.shape, lambda j: (0, 0))
    return pl.pallas_call(
        _rwkv_scan_kernel,
        grid=(s // tb,),
        in_specs=[inp] * 7 + [const(ones), const(diag)],
        out_specs=inp,
        out_shape=jax.ShapeDtypeStruct((nb, s, gw), F32),
        scratch_shapes=[pltpu.VMEM((nb, HEAD_DIM, gw), F32), pltpu.VMEM((nb, HEAD_DIM, gw), F32)],
        compiler_params=_cparams("arbitrary"),
        name="rwkv_scan",
    )(kk, w, b, k, yr, vkr, v, ones, diag)


def _conformer_kernel(cur_ref, halo_ref, dw_ref, dwb_ref, lng_ref, lnb_ref, o_ref, g_ref, sh_ref):
    gw = GROUP_WIDTH
    tm = cur_ref.shape[1]
    hl = halo_ref.shape[1]
    sub = 8
    glu = lambda x: x[:, 0:gw] * _sigmoid(x[:, gw:2 * gw])
    g_ref[0:hl, :] = jnp.where(pl.program_id(1) > 0, glu(halo_ref[0]), 0.0)
    g_ref[hl:hl + tm, :] = glu(cur_ref[0])
    for r in range(1, sub):
        sh_ref[r - 1] = g_ref[pl.ds(r, sh_ref.shape[1]), :]
    acc = jnp.zeros((tm, gw), F32)
    for j in range(CONV_K):
        off = hl - (CONV_K - 1) + j
        r, a = off % sub, off - off % sub
        tap = g_ref[pl.ds(a, tm), :] if r == 0 else sh_ref[r - 1, pl.ds(a, tm), :]
        acc = acc + dw_ref[j:j + 1, :] * tap
    h = acc + dwb_ref[...]
    mu = jnp.mean(h, axis=-1, keepdims=True)
    d = h - mu
    var = jnp.mean(d * d, axis=-1, keepdims=True)
    o_ref[0] = _silu(d * lax.rsqrt(var + LN_EPS) * lng_ref[...] + lnb_ref[...])


def _conformer(proj, dw, dwb, lng, lnb, tm=512, hl=32):
    b, s, _ = proj.shape
    gw = GROUP_WIDTH
    vec = pl.BlockSpec((1, gw), lambda i, j: (0, 0))
    return pl.pallas_call(
        _conformer_kernel,
        grid=(b, s // tm),
        in_specs=[pl.BlockSpec((1, tm, 2 * gw), lambda i, j: (i, j, C_PD // (2 * gw))),
                  pl.BlockSpec((1, hl, 2 * gw), lambda i, j: (i, jnp.maximum(j * (tm // hl) - 1, 0), C_PD // (2 * gw))),
                  pl.BlockSpec((CONV_K, gw), lambda i, j: (0, 0)), vec, vec, vec],
        out_specs=pl.BlockSpec((1, tm, gw), lambda i, j: (i, j, 0)),
        out_shape=jax.ShapeDtypeStruct((b, s, gw), F32),
        scratch_shapes=[pltpu.VMEM((hl + tm, gw), F32), pltpu.VMEM((7, hl + tm - 8, gw), F32)],
        compiler_params=_cparams("parallel", "arbitrary"),
        name="conformer_conv",
    )(proj, proj, dw, dwb, lng, lnb)


def _out_proj_kernel(x_ref, ya_ref, yb_ref, ys_ref, c_ref, v_ref, g_ref, yd_ref, lng_ref, lnb_ref, ones_ref, w_ref,
                     o_ref):
    gw = GROUP_WIDTH
    ones = ones_ref[...]
    y = ys_ref[...] + c_ref[...] * v_ref[...]
    mu = _seg_sum(y, ones) * (1.0 / HEAD_DIM)
    d = y - mu
    var = _seg_sum(d * d, ones) * (1.0 / HEAD_DIM)
    yc = (d * lax.rsqrt(var + RWKV_GN_EPS) * lng_ref[...] + lnb_ref[...]) * g_ref[...]
    acc = x_ref[...]
    for n, part in enumerate((ya_ref[...], yb_ref[...], yc, yd_ref[...])):
        acc = acc + _dot(part.astype(BF16), w_ref[n * gw:(n + 1) * gw, :])
    o_ref[...] = acc


def _out_proj(x2, parts, vecs, ones, w, tm=512):
    m, d = x2.shape
    gw = GROUP_WIDTH
    blk = pl.BlockSpec((tm, gw), lambda i: (i, 0))
    vec = pl.BlockSpec((1, gw), lambda i: (0, 0))
    return pl.pallas_call(
        _out_proj_kernel,
        grid=(m // tm,),
        in_specs=[pl.BlockSpec((tm, d), lambda i: (i, 0))] + [blk] * len(parts) + [vec] * len(vecs)
                 + [pl.BlockSpec(ones.shape, lambda i: (0, 0)), pl.BlockSpec(w.shape, lambda i: (0, 0))],
        out_specs=pl.BlockSpec((tm, d), lambda i: (i, 0)),
        out_shape=jax.ShapeDtypeStruct((m, d), F32),
        compiler_params=_cparams("parallel"),
        name="out_proj",
    )(x2, *parts, *vecs, ones, w)


def _ffn_kernel(x_ref, halo_ref, gain_ref, upg_ref, upv_ref, dwg_ref, dwv_ref, bg_ref, bv_ref, down_ref, fin_ref,
                o_ref, h_ref, acc_ref, ug_ref, uv_ref, *, tiles_per_seq, final_norm):
    j = pl.program_id(1)
    tm = x_ref.shape[0]
    hl = halo_ref.shape[0]

    def norm(x):
        return (x * lax.rsqrt(jnp.mean(x * x, axis=-1, keepdims=True) + RMS_EPS) * gain_ref[...]).astype(BF16)

    @pl.when(j == 0)
    def _():
        first = pl.program_id(0) % tiles_per_seq == 0
        h_ref[0:hl, :] = jnp.where(first, jnp.zeros((hl, x_ref.shape[1]), BF16), norm(halo_ref[...]))
        h_ref[hl:hl + tm, :] = norm(x_ref[...])
        acc_ref[...] = jnp.zeros(acc_ref.shape, F32)

    h = h_ref[...]

    def conv(up_ref, dw_ref, b_ref, u_ref):
        u_ref[...] = _dot(h, up_ref[...])
        out = b_ref[...]
        for tap in range(FFN_CONV_K):
            out = out + dw_ref[tap:tap + 1, :] * u_ref[pl.ds(hl - (FFN_CONV_K - 1) + tap, tm), :]
        return out

    act = _silu(conv(upg_ref, dwg_ref, bg_ref, ug_ref)) * conv(upv_ref, dwv_ref, bv_ref, uv_ref)
    acc_ref[...] += _dot(act.astype(BF16), down_ref[...])

    @pl.when(j == pl.num_programs(1) - 1)
    def _():
        y = x_ref[...] + acc_ref[...]
        if final_norm:
            y = y * lax.rsqrt(jnp.mean(y * y, axis=-1, keepdims=True) + RMS_EPS) * fin_ref[...]
        o_ref[...] = y


def _ffn(x2, gain, up, dw, dwb, down, fin, seq_len, final_norm, tm=512, tn=1408, hl=16):
    m, d = x2.shape
    dff = down.shape[0]
    nj = dff // tn
    kern = functools.partial(_ffn_kernel, tiles_per_seq=seq_len // tm, final_norm=final_norm)
    return pl.pallas_call(
        kern,
        grid=(m // tm, nj),
        in_specs=[pl.BlockSpec((tm, d), lambda i, j: (i, 0)),
                  pl.BlockSpec((hl, d), lambda i, j: (jnp.maximum(i * (tm // hl) - 1, 0), 0)),
                  pl.BlockSpec((1, d), lambda i, j: (0, 0)),
                  pl.BlockSpec((d, tn), lambda i, j: (0, j)),
                  pl.BlockSpec((d, tn), lambda i, j: (0, j + nj)),
                  pl.BlockSpec((FFN_CONV_K, tn), lambda i, j: (0, j)),
                  pl.BlockSpec((FFN_CONV_K, tn), lambda i, j: (0, j + nj)),
                  pl.BlockSpec((1, tn), lambda i, j: (0, j)),
                  pl.BlockSpec((1, tn), lambda i, j: (0, j + nj)),
                  pl.BlockSpec((tn, d), lambda i, j: (j, 0)),
                  pl.BlockSpec((1, d), lambda i, j: (0, 0))],
        out_specs=pl.BlockSpec((tm, d), lambda i, j: (i, 0)),
        out_shape=jax.ShapeDtypeStruct((m, d), F32),
        scratch_shapes=[pltpu.VMEM((hl + tm, d), BF16), pltpu.VMEM((tm, d), F32),
                        pltpu.VMEM((hl + tm, tn), F32), pltpu.VMEM((hl + tm, tn), F32)],
        compiler_params=_cparams("parallel", "arbitrary"),
        name="conv_ffn",
    )(x2, x2, gain, up, up, dw, dw, dwb, dwb, down, fin)


def _block_ones():
    i = np.arange(GROUP_WIDTH) // HEAD_DIM
    return jnp.asarray(i[:, None] == i[None, :], BF16)


def _diag_mask():
    v = np.arange(HEAD_DIM)
    l = np.arange(GROUP_WIDTH) % HEAD_DIM
    return jnp.asarray(v[:, None] == l[None, :], F32)


def _importance_map(n_cmp, n_blk):
    gt = np.zeros((n_blk, n_cmp), np.float32)
    per = SLC_BLOCK // CMP_STRIDE
    for c in range(n_blk * per):
        for i in (c - 1, c):
            if 0 <= i < n_cmp - 1:
                gt[c // per, i] += 1.0
    return jnp.asarray(gt, BF16)


def kernel(x, norm_mix, w_in, cmp_pos, cmp_k_w1, cmp_k_w2, cmp_v_w1, cmp_v_w2, beta_nsa, beta_dil, rwkv_mu, rwkv_w0, rwkv_w_up, rwkv_a0, rwkv_a_up, rwkv_g_up, rwkv_k_k, rwkv_k_a, rwkv_r_k, rwkv_ln_g, rwkv_ln_b, conv_dw, conv_dw_b, conv_ln_g, conv_ln_b, w_out, norm_ffn, ffn_up, ffn_dw, ffn_dw_b, ffn_down, norm_final):
    bsz, seq, d = x.shape
    depth = w_in.shape[0]
    gw = GROUP_WIDTH
    n_cmp = seq // CMP_STRIDE
    n_blk = seq // SLC_BLOCK
    half = CMP_STRIDE * HEAD_DIM

    cols, kv_cols = _proj_columns()
    w_in_p = w_in[:, :, cols].astype(BF16)
    w_kv_p = w_in[:, :, kv_cols].astype(BF16)
    w_out_b = w_out.astype(BF16)
    ffn_up_b = ffn_up.astype(BF16)
    ffn_down_b = ffn_down.astype(BF16)
    ones = _block_ones()
    diag = _diag_mask()
    gt = _importance_map(n_cmp, n_blk)
    row = lambda a: a.reshape(1, -1)

    x2 = x.reshape(bsz * seq, d)
    for l in range(depth):
        proj = _proj_in(x2, row(norm_mix[l]), w_in_p[l], tn=N_PROJ // 2, tm=1024).reshape(bsz, seq, N_PROJ)
        kv = _proj_in(x2, row(norm_mix[l]), w_kv_p[l], tn=N_KV).reshape(bsz, seq, N_KV)

        kh = kv[:, :, KV_KC:KV_KC + HEAD_DIM].reshape(bsz, n_cmp, half)
        vh = kv[:, :, KV_VC:KV_VC + HEAD_DIM].reshape(bsz, n_cmp, half)
        pos = cmp_pos[l].reshape(1, 2 * half)
        k_cmp, v_cmp = _compress(kh, vh, pos[:, :half], pos[:, half:],
                                 cmp_k_w1[l].astype(BF16), cmp_k_w2[l].astype(BF16),
                                 cmp_v_w1[l].astype(BF16), cmp_v_w2[l].astype(BF16))
        ocmp, sel_t = _nsa_cmp(proj, k_cmp, jnp.swapaxes(v_cmp, 1, 2), gt)
        y_a = _nsa_flash(proj, kv, ocmp, sel_t, row(beta_nsa[l]))

        y_b = _dilated(proj, row(beta_dil[l]))

        rp = dict(mu=row(rwkv_mu[l]), w0=row(rwkv_w0[l]), w_up=rwkv_w_up[l].astype(BF16), a0=row(rwkv_a0[l]),
                  a_up=rwkv_a_up[l].astype(BF16), g_up=rwkv_g_up[l].astype(BF16), k_k=row(rwkv_k_k[l]),
                  k_a=row(rwkv_k_a[l]), r_k=row(rwkv_r_k[l]))
        kk, w, b, k2, yr, vkr, v, g, c = _rwkv_prep(proj, rp, ones)
        y_scan = _rwkv_scan(kk, w, b, k2, yr, vkr, v, ones, diag)

        y_d = _conformer(proj, conv_dw[l], row(conv_dw_b[l]), row(conv_ln_g[l]), row(conv_ln_b[l]))

        flat = lambda a: a.reshape(bsz * seq, gw)
        parts = [flat(a) for a in (y_a, y_b, y_scan, c, v, g, y_d)]
        vecs = [row(rwkv_ln_g[l]), row(rwkv_ln_b[l])]
        x2 = _out_proj(x2, parts, vecs, ones, w_out_b[l])

        x2 = _ffn(x2, row(norm_ffn[l]), ffn_up_b[l], ffn_dw[l], row(ffn_dw_b[l]), ffn_down_b[l],
                  row(norm_final), seq, final_norm=(l == depth - 1))
    return x2.reshape(bsz, seq, d)
```

```python
import functools

import jax
import jax.numpy as jnp
import numpy as np
from jax import lax
from jax.experimental import pallas as pl
from jax.experimental.pallas import tpu as pltpu

F32 = jnp.float32
BF16 = jnp.bfloat16

HEAD_DIM = 64
GROUP_HEADS = 4
GROUP_WIDTH = HEAD_DIM * GROUP_HEADS
CMP_STRIDE = 16
CMP_BLOCK = 32
SLC_BLOCK = 64
N_SEL_BLOCKS = 16
NSA_WINDOW = 512
DILATIONS = (1, 4, 16)
TILE = 128
CONV_K = 31
FFN_CONV_K = 3
RMS_EPS = 1e-6
LN_EPS = 1e-5
RWKV_GN_EPS = 64e-5
NEG = -1e30
FORCE_SCORE = 1e9
SCALE = HEAD_DIM ** -0.5
LOG2_E = 1.4426950408889634
SCAN_UNROLL = 8
SLC_CHUNK = 4
V_ROWS = HEAD_DIM + 16

C_PC, C_PD, C_QD, C_QN = 0, 1024, 1536, 2304
N_PROJ = 2560
KV_KC, KV_VC, KV_KS, KV_VS, KV_KW, KV_VW, KV_GATES = 0, 64, 128, 192, 256, 320, 384
N_KV = 512
DIL_SPAN = 2048

VMEM_LIMIT = 48 * 1024 * 1024


def _cparams(*sem):
    return pltpu.CompilerParams(dimension_semantics=sem, vmem_limit_bytes=VMEM_LIMIT)


def _proj_columns():
    gw, hd = GROUP_WIDTH, HEAD_DIM
    a0 = 0
    q = np.arange(a0, a0 + gw)
    kc, vc, ks, vs, kw, vw = [np.arange(a0 + gw + i * hd, a0 + gw + (i + 1) * hd) for i in range(6)]
    gate0 = a0 + gw + 6 * hd
    gates = np.arange(gate0, gate0 + 3 * GROUP_HEADS)
    b0 = gate0 + 3 * GROUP_HEADS
    qd, kd, vd = [np.arange(b0 + i * gw, b0 + (i + 1) * gw) for i in range(3)]
    c0 = b0 + 3 * gw
    pc = np.arange(c0, c0 + 4 * gw)
    d0 = c0 + 4 * gw
    pd = np.arange(d0, d0 + 2 * gw)
    cols = np.concatenate([pc, pd, qd, kd, vd, q])
    kv_cols = np.concatenate([kc, vc, ks, vs, kw, vw, gates, np.zeros(N_KV - KV_GATES - gates.shape[0], np.int64)])
    assert cols.shape[0] == N_PROJ and kv_cols.shape[0] == N_KV
    return cols, kv_cols


def _dot_nt(a, b):
    return lax.dot_general(a, b, (((1,), (1,)), ((), ())), preferred_element_type=F32)


def _dot(a, b):
    return jnp.dot(a, b, preferred_element_type=F32)


def _seg_sum(x, ones_bf16):
    hi = x.astype(BF16)
    lo = (x - hi.astype(F32)).astype(BF16)
    return _dot(hi, ones_bf16) + _dot(lo, ones_bf16)


def _sigmoid(x):
    return 1.0 / (1.0 + jnp.exp(-x))


def _silu(x):
    return x * _sigmoid(x)


def _proj_in_kernel(x_ref, g_ref, w_ref, o_ref, h_ref):
    @pl.when(pl.program_id(1) == 0)
    def _():
        x = x_ref[...]
        ms = jnp.mean(x * x, axis=-1, keepdims=True)
        h_ref[...] = (x * lax.rsqrt(ms + RMS_EPS) * g_ref[...]).astype(BF16)

    o_ref[...] = _dot(h_ref[...], w_ref[...])


def _proj_in(x2, gain, w, tn, tm=512):
    m, d = x2.shape
    n = w.shape[1]
    assert n % tn == 0
    return pl.pallas_call(
        _proj_in_kernel,
        grid=(m // tm, n // tn),
        in_specs=[pl.BlockSpec((tm, d), lambda i, j: (i, 0)),
                  pl.BlockSpec((1, d), lambda i, j: (0, 0)),
                  pl.BlockSpec((d, tn), lambda i, j: (0, j))],
        out_specs=pl.BlockSpec((tm, tn), lambda i, j: (i, j)),
        out_shape=jax.ShapeDtypeStruct((m, n), F32),
        scratch_shapes=[pltpu.VMEM((tm, d), BF16)],
        compiler_params=_cparams("parallel", "arbitrary"),
        name=f"proj_in_{n}",
    )(x2, gain, w)


def _compress_kernel(kh_ref, vh_ref, plo_ref, phi_ref, kw1_ref, kw2_ref, vw1_ref, vw2_ref, ko_ref, vo_ref):
    half = kh_ref.shape[2]
    n = kh_ref.shape[1]

    def mlp(xh, w1_ref, w2_ref):
        a = _dot((xh + plo_ref[...]).astype(BF16), w1_ref[0:half, :])
        bm = _dot((xh + phi_ref[...]).astype(BF16), w1_ref[half:2 * half, :])
        hid = a + pltpu.roll(bm, n - 1, axis=0)
        return _dot(_silu(hid).astype(BF16), w2_ref[...])

    ko_ref[0] = mlp(kh_ref[0], kw1_ref, kw2_ref)
    vo_ref[0] = mlp(vh_ref[0], vw1_ref, vw2_ref)


def _compress(kh, vh, pos_lo, pos_hi, kw1, kw2, vw1, vw2):
    b, n, half = kh.shape
    hid = kw1.shape[1]
    full = lambda shape: pl.BlockSpec(shape, lambda i: (0,) * len(shape))
    return pl.pallas_call(
        _compress_kernel,
        grid=(b,),
        in_specs=[pl.BlockSpec((1, n, half), lambda i: (i, 0, 0)),
                  pl.BlockSpec((1, n, half), lambda i: (i, 0, 0)),
                  full((1, half)), full((1, half)),
                  full((2 * half, hid)), full((hid, HEAD_DIM)),
                  full((2 * half, hid)), full((hid, HEAD_DIM))],
        out_specs=[pl.BlockSpec((1, n, HEAD_DIM), lambda i: (i, 0, 0))] * 2,
        out_shape=[jax.ShapeDtypeStruct((b, n, HEAD_DIM), F32)] * 2,
        compiler_params=_cparams("parallel"),
        name="nsa_compress",
    )(kh, vh, pos_lo, pos_hi, kw1, kw2, vw1, vw2)


def _nsa_cmp_kernel(q_ref, kc_ref, vct_ref, gt_ref, ocmp_ref, sel_ref, score_ref):
    t0 = pl.program_id(1) * TILE
    n_cmp = kc_ref.shape[1]
    n_blk = gt_ref.shape[0]
    heads = range(GROUP_HEADS)
    q_t = (q_ref[0] * SCALE).T.astype(BF16)
    qs_t = jnp.concatenate([q_t[HEAD_DIM * h:HEAD_DIM * (h + 1), :] for h in heads], axis=1)
    kc = kc_ref[0].astype(BF16)
    vc_t = vct_ref[0].astype(BF16)
    ci = lax.broadcasted_iota(jnp.int32, (n_cmp, TILE), 0)
    tq = t0 + lax.broadcasted_iota(jnp.int32, (n_cmp, TILE), 1)
    bias = jnp.where(jnp.where(ci < n_cmp - 1, CMP_STRIDE * ci + CMP_BLOCK - 1, 1 << 30) <= tq, 0.0, NEG)
    bias = jnp.concatenate([bias] * GROUP_HEADS, axis=1)
    s = _dot(kc, qs_t) + bias
    e = jnp.exp(s - jnp.max(s, axis=0, keepdims=True))
    p = jnp.where(bias == 0.0, e / jnp.sum(e, axis=0, keepdims=True), 0.0)
    imp_t = functools.reduce(lambda a, b: a + b, [p[:, TILE * h:TILE * (h + 1)] for h in heads])
    o_t = _dot(vc_t, p.astype(BF16))
    ocmp_ref[0] = jnp.concatenate([o_t[:, TILE * h:TILE * (h + 1)] for h in heads], axis=0).T

    hi = imp_t.astype(BF16)
    lo = (imp_t - hi.astype(F32)).astype(BF16)
    blk = _dot(gt_ref[...], hi) + _dot(gt_ref[...], lo)
    j = lax.broadcasted_iota(jnp.int32, (n_blk, TILE), 0)
    t = t0 + lax.broadcasted_iota(jnp.int32, (n_blk, TILE), 1)
    cur = lax.shift_right_logical(t, 6)
    forced = jnp.where(j == 0, 1, jnp.where(j == cur, 1, jnp.where(j == cur - 1, 1, 0)))
    score = jnp.where(j * SLC_BLOCK <= t, jnp.where(forced == 1, FORCE_SCORE, blk), NEG)
    score_ref[...] = score

    def count(jp, rank):
        row = score_ref[pl.ds(jp, 1), :]
        return rank + jnp.where(row > score, 1, jnp.where(row == score, jnp.where(j > jp, 1, 0), 0))

    n_valid = jnp.minimum((t0 + TILE) // SLC_BLOCK, n_blk)
    rank = lax.fori_loop(0, n_valid, count, jnp.zeros((n_blk, TILE), jnp.int32))
    sel_ref[0] = jnp.where(rank < N_SEL_BLOCKS, 0.0, NEG)


def _nsa_cmp(proj, k_cmp, v_cmp_t, gt):
    b, s, _ = proj.shape
    n_cmp = k_cmp.shape[1]
    n_blk = gt.shape[0]
    return pl.pallas_call(
        _nsa_cmp_kernel,
        grid=(b, s // TILE),
        in_specs=[pl.BlockSpec((1, TILE, GROUP_WIDTH), lambda i, j: (i, j, C_QN // GROUP_WIDTH)),
                  pl.BlockSpec((1, n_cmp, HEAD_DIM), lambda i, j: (i, 0, 0)),
                  pl.BlockSpec((1, HEAD_DIM, n_cmp), lambda i, j: (i, 0, 0)),
                  pl.BlockSpec(gt.shape, lambda i, j: (0, 0))],
        out_specs=[pl.BlockSpec((1, TILE, GROUP_WIDTH), lambda i, j: (i, j, 0)),
                   pl.BlockSpec((1, n_blk, TILE), lambda i, j: (i, 0, j))],
        out_shape=[jax.ShapeDtypeStruct((b, s, GROUP_WIDTH), F32),
                   jax.ShapeDtypeStruct((b, n_blk, s), F32)],
        scratch_shapes=[pltpu.VMEM((n_blk, TILE), F32)],
        compiler_params=_cparams("parallel", "parallel"),
        name="nsa_cmp",
    )(proj, k_cmp, v_cmp_t, gt)


def _nsa_flash_kernel(q_ref, ocmp_ref, sel_ref, gl_ref, ge_ref, ks_ref, vs_ref, kw_ref, vw_ref, beta_ref,
                      o_ref, sa_ref, sb_ref, sw_ref):
    qi = pl.program_id(1)
    cols = GROUP_HEADS * TILE
    q_t = (q_ref[0] * (SCALE * LOG2_E)).T
    qs_t = jnp.concatenate([q_t[HEAD_DIM * h:HEAD_DIM * (h + 1), :] for h in range(GROUP_HEADS)], axis=1)
    sel4 = jnp.concatenate([sel_ref[0]] * GROUP_HEADS, axis=1)
    w = jnp.concatenate([qs_t, sel4], axis=0).astype(BF16)
    q_pos = qi * TILE + lax.broadcasted_iota(jnp.int32, (1, TILE), 1)

    def keys(ref, start, n):
        return ref[0, pl.ds(start, n)].reshape(n * TILE, 2 * HEAD_DIM)

    def values_t(ref, start, n):
        v = ref[0, pl.ds(start, n)]
        return jnp.concatenate([v[u] for u in range(n)], axis=1)

    def band_bias(start, n, window):
        key_pos = start * TILE + lax.broadcasted_iota(jnp.int32, (n * TILE, TILE), 0)
        dist = q_pos - key_pos
        ok = jnp.where(dist >= 0, 1, 0) if window is None else jnp.where(dist >= 0, jnp.where(dist <= window, 1, 0), 0)
        return jnp.concatenate([jnp.where(ok == 1, 0.0, NEG)] * GROUP_HEADS, axis=1)

    def col_max(s):
        return jnp.max(s, axis=0, keepdims=True)

    def finish(acc):
        o_t = acc[0:HEAD_DIM, :] / acc[HEAD_DIM:HEAD_DIM + 1, :]
        return jnp.concatenate([o_t[:, TILE * h:TILE * (h + 1)] for h in range(GROUP_HEADS)], axis=0).T

    n_win = NSA_WINDOW // TILE + 1
    first = jnp.maximum(qi - (n_win - 1), 0)
    sw_ref[...] = _dot(keys(kw_ref, first, n_win), w) + band_bias(first, n_win, NSA_WINDOW)
    p = jnp.exp2(sw_ref[...] - col_max(sw_ref[...])).astype(BF16)
    o_win = finish(_dot(values_t(vw_ref, first, n_win), p))

    def scores(c):
        return _dot(keys(ks_ref, c * SLC_CHUNK, SLC_CHUNK), w) + band_bias(c * SLC_CHUNK, SLC_CHUNK, None)

    def prefetch(ref, c, extra=0.0):
        s = scores(c) + extra
        ref[...] = s
        return col_max(s)

    def consume(ref, c, m, cm, acc):
        m_new = jnp.maximum(m, cm)
        p = jnp.exp2(ref[...] - m_new).astype(BF16)
        return m_new, jnp.exp2(m - m_new) * acc + _dot(values_t(vs_ref, c * SLC_CHUNK, SLC_CHUNK), p)

    def two_chunks(j, carry):
        m, cm_a, acc = carry
        cm_b = prefetch(sb_ref, 2 * j + 1)
        m, acc = consume(sa_ref, 2 * j, m, cm_a, acc)
        cm_a = prefetch(sa_ref, 2 * j + 2)
        m, acc = consume(sb_ref, 2 * j + 1, m, cm_b, acc)
        return m, cm_a, acc

    n_full = qi // SLC_CHUNK
    n_pairs = n_full // 2
    m, cm_a, acc = lax.fori_loop(0, n_pairs, two_chunks,
                                 (jnp.full((1, cols), NEG, F32), prefetch(sa_ref, 0), jnp.zeros((V_ROWS, cols), F32)))
    odd = 2 * n_pairs + 1

    def last_two():
        cm_b = prefetch(sb_ref, odd)
        m1, acc1 = consume(sa_ref, 2 * n_pairs, m, cm_a, acc)
        return consume(sb_ref, odd, m1, cm_b, acc1)[1]

    def last_one():
        return consume(sa_ref, 2 * n_pairs, m, cm_a, acc)[1]

    o_slc = finish(lax.cond(odd <= n_full, last_two, last_one))

    sg = _sigmoid(gl_ref[0])
    sg_hi = sg.astype(BF16)
    sg_lo = (sg - sg_hi.astype(F32)).astype(BF16)
    gate = lambda j: _dot(sg_hi, ge_ref[j]) + _dot(sg_lo, ge_ref[j])
    y = gate(0) * ocmp_ref[0] + gate(1) * o_slc + gate(2) * o_win
    ms = jnp.mean(y * y, axis=-1, keepdims=True)
    o_ref[0] = y * lax.rsqrt(ms + RMS_EPS) * beta_ref[...]


def _key_tiles(k, onehot):
    b, s, _ = k.shape
    aug = jnp.zeros((b, s, HEAD_DIM), BF16) if onehot is None else jnp.broadcast_to(onehot, (b, s, HEAD_DIM))
    return jnp.concatenate([k.astype(BF16), aug], axis=-1).reshape(b, s // TILE, TILE, 2 * HEAD_DIM)


def _value_tiles_t(v):
    b, s, _ = v.shape
    v_t = v.astype(BF16).reshape(b, s // TILE, TILE, HEAD_DIM).transpose(0, 1, 3, 2)
    pad = jnp.zeros((b, s // TILE, V_ROWS - HEAD_DIM, TILE), BF16).at[:, :, 0, :].set(1.0)
    return jnp.concatenate([v_t, pad], axis=2)


def _nsa_flash(proj, kv, ocmp, sel_t, beta):
    b, s, _ = proj.shape
    gw = GROUP_WIDTH
    n_blk = sel_t.shape[1]
    assert n_blk == HEAD_DIM, "the selection rows share the 128-deep contraction with the 64 head dims"
    piece = lambda c: kv[:, :, c:c + HEAD_DIM]
    blk_of_key = np.arange(s)[:, None] // SLC_BLOCK == np.arange(n_blk)[None, :]
    ks = _key_tiles(piece(KV_KS), jnp.asarray(blk_of_key, BF16))
    kw = _key_tiles(piece(KV_KW), None)
    vs = _value_tiles_t(piece(KV_VS))
    vw = _value_tiles_t(piece(KV_VW))
    ge = np.zeros((3, TILE, gw), np.float32)
    for j in range(3):
        for h in range(GROUP_HEADS):
            ge[j, 3 * h + j, HEAD_DIM * h:HEAD_DIM * (h + 1)] = 1.0
    gate_expand = jnp.asarray(ge, BF16)
    nt = s // TILE
    col = lambda c: pl.BlockSpec((1, TILE, gw), lambda i, j: (i, j, c // gw))
    ktile = pl.BlockSpec((1, nt, TILE, 2 * HEAD_DIM), lambda i, j: (i, 0, 0, 0))
    vtile = pl.BlockSpec((1, nt, V_ROWS, TILE), lambda i, j: (i, 0, 0, 0))
    assert nt % SLC_CHUNK == 0 and nt >= NSA_WINDOW // TILE + 1
    return pl.pallas_call(
        _nsa_flash_kernel,
        grid=(b, nt),
        in_specs=[col(C_QN),
                  pl.BlockSpec((1, TILE, gw), lambda i, j: (i, j, 0)),
                  pl.BlockSpec((1, n_blk, TILE), lambda i, j: (i, 0, j)),
                  pl.BlockSpec((1, TILE, TILE), lambda i, j: (i, j, KV_GATES // TILE)),
                  pl.BlockSpec(gate_expand.shape, lambda i, j: (0, 0, 0)),
                  ktile, vtile, ktile, vtile,
                  pl.BlockSpec((1, gw), lambda i, j: (0, 0))],
        out_specs=pl.BlockSpec((1, TILE, gw), lambda i, j: (i, j, 0)),
        out_shape=jax.ShapeDtypeStruct((b, s, gw), F32),
        scratch_shapes=[pltpu.VMEM((SLC_CHUNK * TILE, GROUP_HEADS * TILE), F32),
                        pltpu.VMEM((SLC_CHUNK * TILE, GROUP_HEADS * TILE), F32),
                        pltpu.VMEM((NSA_WINDOW + TILE, GROUP_HEADS * TILE), F32)],
        compiler_params=_cparams("parallel", "arbitrary"),
        name="nsa_flash",
    )(proj, ocmp, sel_t, kv, gate_expand, ks, vs, kw, vw, beta)


def _dilated_kernel(q0_ref, q1_ref, k0_ref, k1_ref, v0_ref, v1_ref, beta_ref, o_ref, og_ref, lg_ref):
    gw = GROUP_WIDTH
    cols = GROUP_HEADS * TILE
    p0 = pl.program_id(1) * DIL_SPAN

    def load(refs, first, n, rows):
        window = pl.ds(pl.multiple_of(first, TILE), n)
        return jnp.concatenate([ref.at[0, window][rows, :] for ref in refs], axis=1)

    def store(ref, g, first, n, rows, val):
        window = pl.ds(pl.multiple_of(first, TILE), n)
        for u in range(2):
            ref.at[g, u, window][rows, :] = val[:, TILE * u:TILE * (u + 1)]

    row_head = lax.shift_right_logical(lax.broadcasted_iota(jnp.int32, (gw, TILE), 0), 6)
    rel = (lax.broadcasted_iota(jnp.int32, (2 * TILE, TILE), 1)
           - lax.broadcasted_iota(jnp.int32, (2 * TILE, TILE), 0))

    for g, dil in enumerate(DILATIONS):
        n_t = DIL_SPAN // (TILE * dil)

        def body(i, carry, r, g=g, dil=dil):
            sub0 = p0 // dil + i * TILE
            ksub0 = jnp.maximum(sub0 - TILE, 0)
            rows_q = pl.ds(r, TILE, stride=dil)
            rows_k = pl.ds(r, 2 * TILE, stride=dil)
            keys = functools.partial(load, first=dil * ksub0, n=2 * dil * TILE, rows=rows_k)
            q_t = (load((q0_ref, q1_ref), dil * sub0, dil * TILE, rows_q) * (SCALE * LOG2_E)).T
            wq = jnp.concatenate([jnp.where(row_head == h, q_t, 0.0) for h in range(GROUP_HEADS)],
                                 axis=1).astype(BF16)
            dist = rel + (sub0 - ksub0)
            bias = jnp.where(dist >= 0, jnp.where(dist <= TILE, 0.0, NEG), NEG)
            bias = jnp.concatenate([bias] * GROUP_HEADS, axis=1)
            s = _dot(keys((k0_ref, k1_ref)).astype(BF16), wq) + bias
            m = jnp.max(s, axis=0, keepdims=True)
            p = jnp.exp2(s - m)
            l = jnp.sum(p, axis=0, keepdims=True)
            p = p.astype(BF16)
            v_t = keys((v0_ref, v1_ref)).T.astype(BF16)
            lse = m + jnp.log2(l)
            o_t, l_t = [], []
            for h in range(GROUP_HEADS):
                qs = slice(TILE * h, TILE * (h + 1))
                o_t.append(_dot(v_t[HEAD_DIM * h:HEAD_DIM * (h + 1), :], p[:, qs]) / l[:, qs])
                l_t.append(jnp.broadcast_to(lse[:, qs], (HEAD_DIM, TILE)))
            store(og_ref, g, dil * i * TILE, dil * TILE, rows_q, jnp.concatenate(o_t, axis=0).T)
            store(lg_ref, g, dil * i * TILE, dil * TILE, rows_q, jnp.concatenate(l_t, axis=0).T)
            return carry

        for r in range(dil):
            if n_t == 1:
                body(0, 0, r)
            else:
                lax.fori_loop(0, n_t, functools.partial(body, r=r), 0, unroll=4)

    piece = 256

    def combine(c, carry):
        rows = pl.ds(pl.multiple_of(c * piece, piece), piece)
        both = lambda ref, g: jnp.concatenate([ref[g, 0, rows, :], ref[g, 1, rows, :]], axis=1)
        ls = [both(lg_ref, g) for g in range(len(DILATIONS))]
        mx = functools.reduce(jnp.maximum, ls)
        es = [jnp.exp2(x - mx) for x in ls]
        den = functools.reduce(lambda a, b: a + b, es)
        y = functools.reduce(lambda a, b: a + b, [(e / den) * both(og_ref, g) for g, e in enumerate(es)])
        o_ref[0, rows, :] = y * lax.rsqrt(jnp.mean(y * y, axis=-1, keepdims=True) + RMS_EPS) * beta_ref[...]
        return carry

    lax.fori_loop(0, DIL_SPAN // piece, combine, 0)


def _dilated(proj, beta):
    b, s, _ = proj.shape
    gw = GROUP_WIDTH
    assert s % DIL_SPAN == 0 and s >= 2 * TILE * max(DILATIONS)
    lane_block = lambda n: pl.BlockSpec((1, s, TILE), lambda i, j: (i, 0, C_QD // TILE + n))
    return pl.pallas_call(
        _dilated_kernel,
        grid=(b, s // DIL_SPAN),
        in_specs=[lane_block(n) for n in range(3 * gw // TILE)] + [pl.BlockSpec((1, gw), lambda i, j: (0, 0))],
        out_specs=pl.BlockSpec((1, DIL_SPAN, gw), lambda i, j: (i, j, 0)),
        out_shape=jax.ShapeDtypeStruct((b, s, gw), F32),
        scratch_shapes=[pltpu.VMEM((len(DILATIONS), gw // TILE, DIL_SPAN, TILE), F32)] * 2,
        compiler_params=_cparams("parallel", "arbitrary"),
        name="dilated",
    )(*([proj] * (3 * gw // TILE)), beta)


def _rwkv_prep_kernel(cur_ref, halo_ref, mu_ref, w0_ref, wup_ref, a0_ref, aup_ref, gup_ref, kk_w_ref, ka_ref,
                      rk_ref, ones_ref,
                      kk_o, w_o, b_o, k_o, yr_o, vkr_o, v_o, g_o, c_o):
    gw = GROUP_WIDTH
    tm = cur_ref.shape[1]
    ones = ones_ref[...]
    x = cur_ref[0]
    last = halo_ref[0][7:8, :]
    last = jnp.where(pl.program_id(1) > 0, last, 0.0)
    rid = lax.broadcasted_iota(jnp.int32, (tm, 1), 0)
    xprev = jnp.where(rid == 0, last, pltpu.roll(x, 1, axis=0))
    xs = x + (xprev - x) * mu_ref[...]
    r, k, v = xs[:, 0:gw], xs[:, gw:2 * gw], xs[:, 2 * gw:3 * gw]
    wd, ad, gd = xs[:, 3 * gw:3 * gw + 64], xs[:, 3 * gw + 64:3 * gw + 128], xs[:, 3 * gw + 128:4 * gw]
    z = -(w0_ref[...] + _dot(jnp.tanh(wd).astype(BF16), wup_ref[...]))
    softplus = jnp.maximum(z, 0.0) + jnp.log1p(jnp.exp(-jnp.abs(z)))
    w = jnp.exp(-jnp.exp(-softplus - 0.5))
    a = _sigmoid(a0_ref[...] + _dot(ad.astype(BF16), aup_ref[...]))
    g = _dot(_sigmoid(gd).astype(BF16), gup_ref[...])
    kk = k * kk_w_ref[...]
    kk = kk * lax.rsqrt(_seg_sum(kk * kk, ones) + 1e-12)
    k2 = k * (1.0 + (a - 1.0) * ka_ref[...])
    b = kk * a
    br = _seg_sum(b * r, ones)
    kr = _seg_sum(k2 * r, ones)
    kk_o[0] = kk
    w_o[0] = w
    b_o[0] = b
    k_o[0] = k2
    yr_o[0] = w * r - kk * br
    vkr_o[0] = v * kr
    v_o[0] = v
    g_o[0] = g
    c_o[0] = _seg_sum(r * k2 * rk_ref[...], ones)


def _rwkv_prep(proj, p, ones, tm=256):
    b, s, _ = proj.shape
    gw = GROUP_WIDTH
    vec = lambda n: pl.BlockSpec((1, n), lambda i, j: (0, 0))
    mat = lambda a: pl.BlockSpec(a.shape, lambda i, j: (0, 0))
    out = pl.BlockSpec((1, tm, gw), lambda i, j: (i, j, 0))
    return pl.pallas_call(
        _rwkv_prep_kernel,
        grid=(b, s // tm),
        in_specs=[pl.BlockSpec((1, tm, 4 * gw), lambda i, j: (i, j, C_PC // (4 * gw))),
                  pl.BlockSpec((1, 8, 4 * gw), lambda i, j: (i, jnp.maximum(j * (tm // 8) - 1, 0), C_PC // (4 * gw))),
                  vec(4 * gw), vec(gw), mat(p["w_up"]), vec(gw), mat(p["a_up"]), mat(p["g_up"]),
                  vec(gw), vec(gw), vec(gw), mat(ones)],
        out_specs=[out] * 9,
        out_shape=[jax.ShapeDtypeStruct((b, s, gw), F32)] * 9,
        compiler_params=_cparams("parallel", "arbitrary"),
        name="rwkv_prep",
    )(proj, proj, p["mu"], p["w0"], p["w_up"], p["a0"], p["a_up"], p["g_up"], p["k_k"], p["k_a"], p["r_k"], ones)


def _rwkv_scan_kernel(kk_ref, w_ref, b_ref, k_ref, yr_ref, vkr_ref, v_ref, ones_ref, diag_ref,
                      y_ref, state_ref, yacc_ref):
    nb, tb = kk_ref.shape[0], kk_ref.shape[1]

    @pl.when(pl.program_id(0) == 0)
    def _():
        state_ref[...] = jnp.zeros(state_ref.shape, F32)
        yacc_ref[...] = jnp.zeros(yacc_ref.shape, F32)

    ones = ones_ref[...]
    diag = diag_ref[...].astype(BF16)
    lane = jnp.bitwise_and(lax.broadcasted_iota(jnp.int32, (HEAD_DIM, GROUP_WIDTH), 1), HEAD_DIM - 1)

    def group(c, carry):
        def step(tt, carry2):
            t = c * HEAD_DIM + tt
            here = lane == tt
            row = lambda ref, i: ref[i, pl.ds(t, 1), :]
            st = [state_ref[i] for i in range(nb)]
            sb = [st[i].astype(BF16) for i in range(nb)]
            rb = lambda ref, i: row(ref, i).astype(BF16)
            reduce = lambda parts: _dot(jnp.concatenate(parts, axis=0), ones)
            red_v = reduce([rb(v_ref, i) * diag for i in range(nb)])
            red_s = reduce([sb[i] * rb(kk_ref, i) for i in range(nb)])
            red_y = reduce([sb[i] * rb(yr_ref, i) + rb(vkr_ref, i) * diag for i in range(nb)])
            part = lambda red, n: red[n * HEAD_DIM:(n + 1) * HEAD_DIM, :]
            for i in range(nb):
                sa, vcol, ycol = part(red_s, i), part(red_v, i), part(red_y, i)
                state_ref[i] = st[i] * row(w_ref, i) - sa * row(b_ref, i) + vcol * row(k_ref, i)
                yacc_ref[i] = jnp.where(here, ycol, yacc_ref[i])
            return carry2

        lax.fori_loop(0, HEAD_DIM, step, 0, unroll=SCAN_UNROLL)
        rows = pl.ds(pl.multiple_of(c * HEAD_DIM, HEAD_DIM), HEAD_DIM)
        for i in range(nb):
            full_t = jnp.concatenate([yacc_ref[i], jnp.zeros((HEAD_DIM, GROUP_WIDTH), F32)], axis=0).T
            y_ref[i, rows, :] = jnp.concatenate(
                [full_t[HEAD_DIM * h:HEAD_DIM * (h + 1), 0:HEAD_DIM] for h in range(GROUP_HEADS)], axis=1)
        return carry

    lax.fori_loop(0, tb // HEAD_DIM, group, 0)


def _rwkv_scan(kk, w, b, k, yr, vkr, v, ones, diag, tb=128):
    nb, s, gw = kk.shape
    inp = pl.BlockSpec((nb, tb, gw), lambda j: (0, j, 0))
    const = lambda a: pl.BlockSpec(a.shape, lambda j: (0, 0))
    return pl.pallas_call(
        _rwkv_scan_kernel,
        grid=(s // tb,),
        in_specs=[inp] * 7 + [const(ones), const(diag)],
        out_specs=inp,
        out_shape=jax.ShapeDtypeStruct((nb, s, gw), F32),
        scratch_shapes=[pltpu.VMEM((nb, HEAD_DIM, gw), F32), pltpu.VMEM((nb, HEAD_DIM, gw), F32)],
        compiler_params=_cparams("arbitrary"),
        name="rwkv_scan",
    )(kk, w, b, k, yr, vkr, v, ones, diag)


def _conformer_kernel(cur_ref, halo_ref, dw_ref, dwb_ref, lng_ref, lnb_ref, o_ref, g_ref, sh_ref):
    gw = GROUP_WIDTH
    tm = cur_ref.shape[1]
    hl = halo_ref.shape[1]
    sub = 8
    glu = lambda x: x[:, 0:gw] * _sigmoid(x[:, gw:2 * gw])
    g_ref[0:hl, :] = jnp.where(pl.program_id(1) > 0, glu(halo_ref[0]), 0.0)
    g_ref[hl:hl + tm, :] = glu(cur_ref[0])
    for r in range(1, sub):
        sh_ref[r - 1] = g_ref[pl.ds(r, sh_ref.shape[1]), :]
    acc = jnp.zeros((tm, gw), F32)
    for j in range(CONV_K):
        off = hl - (CONV_K - 1) + j
        r, a = off % sub, off - off % sub
        tap = g_ref[pl.ds(a, tm), :] if r == 0 else sh_ref[r - 1, pl.ds(a, tm), :]
        acc = acc + dw_ref[j:j + 1, :] * tap
    h = acc + dwb_ref[...]
    mu = jnp.mean(h, axis=-1, keepdims=True)
    d = h - mu
    var = jnp.mean(d * d, axis=-1, keepdims=True)
    o_ref[0] = _silu(d * lax.rsqrt(var + LN_EPS) * lng_ref[...] + lnb_ref[...])


def _conformer(proj, dw, dwb, lng, lnb, tm=512, hl=32):
    b, s, _ = proj.shape
    gw = GROUP_WIDTH
    vec = pl.BlockSpec((1, gw), lambda i, j: (0, 0))
    return pl.pallas_call(
        _conformer_kernel,
        grid=(b, s // tm),
        in_specs=[pl.BlockSpec((1, tm, 2 * gw), lambda i, j: (i, j, C_PD // (2 * gw))),
                  pl.BlockSpec((1, hl, 2 * gw), lambda i, j: (i, jnp.maximum(j * (tm // hl) - 1, 0), C_PD // (2 * gw))),
                  pl.BlockSpec((CONV_K, gw), lambda i, j: (0, 0)), vec, vec, vec],
        out_specs=pl.BlockSpec((1, tm, gw), lambda i, j: (i, j, 0)),
        out_shape=jax.ShapeDtypeStruct((b, s, gw), F32),
        scratch_shapes=[pltpu.VMEM((hl + tm, gw), F32), pltpu.VMEM((7, hl + tm - 8, gw), F32)],
        compiler_params=_cparams("parallel", "arbitrary"),
        name="conformer_conv",
    )(proj, proj, dw, dwb, lng, lnb)


def _out_proj_kernel(x_ref, ya_ref, yb_ref, ys_ref, c_ref, v_ref, g_ref, yd_ref, lng_ref, lnb_ref, ones_ref, w_ref,
                     o_ref):
    gw = GROUP_WIDTH
    ones = ones_ref[...]
    y = ys_ref[...] + c_ref[...] * v_ref[...]
    mu = _seg_sum(y, ones) * (1.0 / HEAD_DIM)
    d = y - mu
    var = _seg_sum(d * d, ones) * (1.0 / HEAD_DIM)
    yc = (d * lax.rsqrt(var + RWKV_GN_EPS) * lng_ref[...] + lnb_ref[...]) * g_ref[...]
    acc = x_ref[...]
    for n, part in enumerate((ya_ref[...], yb_ref[...], yc, yd_ref[...])):
        acc = acc + _dot(part.astype(BF16), w_ref[n * gw:(n + 1) * gw, :])
    o_ref[...] = acc


def _out_proj(x2, parts, vecs, ones, w, tm=512):
    m, d = x2.shape
    gw = GROUP_WIDTH
    blk = pl.BlockSpec((tm, gw), lambda i: (i, 0))
    vec = pl.BlockSpec((1, gw), lambda i: (0, 0))
    return pl.pallas_call(
        _out_proj_kernel,
        grid=(m // tm,),
        in_specs=[pl.BlockSpec((tm, d), lambda i: (i, 0))] + [blk] * len(parts) + [vec] * len(vecs)
                 + [pl.BlockSpec(ones.shape, lambda i: (0, 0)), pl.BlockSpec(w.shape, lambda i: (0, 0))],
        out_specs=pl.BlockSpec((tm, d), lambda i: (i, 0)),
        out_shape=jax.ShapeDtypeStruct((m, d), F32),
        compiler_params=_cparams("parallel"),
        name="out_proj",
    )(x2, *parts, *vecs, ones, w)


def _ffn_kernel(x_ref, halo_ref, gain_ref, upg_ref, upv_ref, dwg_ref, dwv_ref, bg_ref, bv_ref, down_ref, fin_ref,
                o_ref, h_ref, acc_ref, ug_ref, uv_ref, *, tiles_per_seq, final_norm):
    j = pl.program_id(1)
    tm = x_ref.shape[0]
    hl = halo_ref.shape[0]

    def norm(x):
        return (x * lax.rsqrt(jnp.mean(x * x, axis=-1, keepdims=True) + RMS_EPS) * gain_ref[...]).astype(BF16)

    @pl.when(j == 0)
    def _():
        first = pl.program_id(0) % tiles_per_seq == 0
        h_ref[0:hl, :] = jnp.where(first, jnp.zeros((hl, x_ref.shape[1]), BF16), norm(halo_ref[...]))
        h_ref[hl:hl + tm, :] = norm(x_ref[...])
        acc_ref[...] = jnp.zeros(acc_ref.shape, F32)

    h = h_ref[...]

    def conv(up_ref, dw_ref, b_ref, u_ref):
        u_ref[...] = _dot(h, up_ref[...])
        out = b_ref[...]
        for tap in range(FFN_CONV_K):
            out = out + dw_ref[tap:tap + 1, :] * u_ref[pl.ds(hl - (FFN_CONV_K - 1) + tap, tm), :]
        return out

    act = _silu(conv(upg_ref, dwg_ref, bg_ref, ug_ref)) * conv(upv_ref, dwv_ref, bv_ref, uv_ref)
    acc_ref[...] += _dot(act.astype(BF16), down_ref[...])

    @pl.when(j == pl.num_programs(1) - 1)
    def _():
        y = x_ref[...] + acc_ref[...]
        if final_norm:
            y = y * lax.rsqrt(jnp.mean(y * y, axis=-1, keepdims=True) + RMS_EPS) * fin_ref[...]
        o_ref[...] = y


def _ffn(x2, gain, up, dw, dwb, down, fin, seq_len, final_norm, tm=512, tn=1408, hl=16):
    m, d = x2.shape
    dff = down.shape[0]
    nj = dff // tn
    kern = functools.partial(_ffn_kernel, tiles_per_seq=seq_len // tm, final_norm=final_norm)
    return pl.pallas_call(
        kern,
        grid=(m // tm, nj),
        in_specs=[pl.BlockSpec((tm, d), lambda i, j: (i, 0)),
                  pl.BlockSpec((hl, d), lambda i, j: (jnp.maximum(i * (tm // hl) - 1, 0), 0)),
                  pl.BlockSpec((1, d), lambda i, j: (0, 0)),
                  pl.BlockSpec((d, tn), lambda i, j: (0, j)),
                  pl.BlockSpec((d, tn), lambda i, j: (0, j + nj)),
                  pl.BlockSpec((FFN_CONV_K, tn), lambda i, j: (0, j)),
                  pl.BlockSpec((FFN_CONV_K, tn), lambda i, j: (0, j + nj)),
                  pl.BlockSpec((1, tn), lambda i, j: (0, j)),
                  pl.BlockSpec((1, tn), lambda i, j: (0, j + nj)),
                  pl.BlockSpec((tn, d), lambda i, j: (j, 0)),
                  pl.BlockSpec((1, d), lambda i, j: (0, 0))],
        out_specs=pl.BlockSpec((tm, d), lambda i, j: (i, 0)),
        out_shape=jax.ShapeDtypeStruct((m, d), F32),
        scratch_shapes=[pltpu.VMEM((hl + tm, d), BF16), pltpu.VMEM((tm, d), F32),
                        pltpu.VMEM((hl + tm, tn), F32), pltpu.VMEM((hl + tm, tn), F32)],
        compiler_params=_cparams("parallel", "arbitrary"),
        name="conv_ffn",
    )(x2, x2, gain, up, up, dw, dw, dwb, dwb, down, fin)


def _block_ones():
    i = np.arange(GROUP_WIDTH) // HEAD_DIM
    return jnp.asarray(i[:, None] == i[None, :], BF16)


def _diag_mask():
    v = np.arange(HEAD_DIM)
    l = np.arange(GROUP_WIDTH) % HEAD_DIM
    return jnp.asarray(v[:, None] == l[None, :], F32)


def _importance_map(n_cmp, n_blk):
    gt = np.zeros((n_blk, n_cmp), np.float32)
    per = SLC_BLOCK // CMP_STRIDE
    for c in range(n_blk * per):
        for i in (c - 1, c):
            if 0 <= i < n_cmp - 1:
                gt[c // per, i] += 1.0
    return jnp.asarray(gt, BF16)


def kernel(x, norm_mix, w_in, cmp_pos, cmp_k_w1, cmp_k_w2, cmp_v_w1, cmp_v_w2, beta_nsa, beta_dil, rwkv_mu, rwkv_w0, rwkv_w_up, rwkv_a0, rwkv_a_up, rwkv_g_up, rwkv_k_k, rwkv_k_a, rwkv_r_k, rwkv_ln_g, rwkv_ln_b, conv_dw, conv_dw_b, conv_ln_g, conv_ln_b, w_out, norm_ffn, ffn_up, ffn_dw, ffn_dw_b, ffn_down, norm_final):
    bsz, seq, d = x.shape
    depth = w_in.shape[0]
    gw = GROUP_WIDTH
    n_cmp = seq // CMP_STRIDE
    n_blk = seq // SLC_BLOCK
    half = CMP_STRIDE * HEAD_DIM

    cols, kv_cols = _proj_columns()
    w_in_p = w_in[:, :, cols].astype(BF16)
    w_kv_p = w_in[:, :, kv_cols].astype(BF16)
    w_out_b = w_out.astype(BF16)
    ffn_up_b = ffn_up.astype(BF16)
    ffn_down_b = ffn_down.astype(BF16)
    ones = _block_ones()
    diag = _diag_mask()
    gt = _importance_map(n_cmp, n_blk)
    row = lambda a: a.reshape(1, -1)

    x2 = x.reshape(bsz * seq, d)
    for l in range(depth):
        proj = _proj_in(x2, row(norm_mix[l]), w_in_p[l], tn=N_PROJ // 2, tm=1024).reshape(bsz, seq, N_PROJ)
        kv = _proj_in(x2, row(norm_mix[l]), w_kv_p[l], tn=N_KV).reshape(bsz, seq, N_KV)

        kh = kv[:, :, KV_KC:KV_KC + HEAD_DIM].reshape(bsz, n_cmp, half)
        vh = kv[:, :, KV_VC:KV_VC + HEAD_DIM].reshape(bsz, n_cmp, half)
        pos = cmp_pos[l].reshape(1, 2 * half)
        k_cmp, v_cmp = _compress(kh, vh, pos[:, :half], pos[:, half:],
                                 cmp_k_w1[l].astype(BF16), cmp_k_w2[l].astype(BF16),
                                 cmp_v_w1[l].astype(BF16), cmp_v_w2[l].astype(BF16))
        ocmp, sel_t = _nsa_cmp(proj, k_cmp, jnp.swapaxes(v_cmp, 1, 2), gt)
        y_a = _nsa_flash(proj, kv, ocmp, sel_t, row(beta_nsa[l]))

        y_b = _dilated(proj, row(beta_dil[l]))

        rp = dict(mu=row(rwkv_mu[l]), w0=row(rwkv_w0[l]), w_up=rwkv_w_up[l].astype(BF16), a0=row(rwkv_a0[l]),
                  a_up=rwkv_a_up[l].astype(BF16), g_up=rwkv_g_up[l].astype(BF16), k_k=row(rwkv_k_k[l]),
                  k_a=row(rwkv_k_a[l]), r_k=row(rwkv_r_k[l]))
        kk, w, b, k2, yr, vkr, v, g, c = _rwkv_prep(proj, rp, ones)
        y_scan = _rwkv_scan(kk, w, b, k2, yr, vkr, v, ones, diag)

        y_d = _conformer(proj, conv_dw[l], row(conv_dw_b[l]), row(conv_ln_g[l]), row(conv_ln_b[l]))

        flat = lambda a: a.reshape(bsz * seq, gw)
        parts = [flat(a) for a in (y_a, y_b, y_scan, c, v, g, y_d)]
        vecs = [row(rwkv_ln_g[l]), row(rwkv_ln_b[l])]
        x2 = _out_proj(x2, parts, vecs, ones, w_out_b[l])

        x2 = _ffn(x2, row(norm_ffn[l]), ffn_up_b[l], ffn_dw[l], row(ffn_dw_b[l]), ffn_down_b[l],
                  row(norm_final), seq, final_norm=(l == depth - 1))
    return x2.reshape(bsz, seq, d)
```

```python
import functools

import jax
import jax.numpy as jnp
import numpy as np
from jax import lax
from jax.experimental import pallas as pl
from jax.experimental.pallas import tpu as pltpu

F32 = jnp.float32
BF16 = jnp.bfloat16

HEAD_DIM = 64
GROUP_HEADS = 4
GROUP_WIDTH = HEAD_DIM * GROUP_HEADS
CMP_STRIDE = 16
CMP_BLOCK = 32
SLC_BLOCK = 64
N_SEL_BLOCKS = 16
NSA_WINDOW = 512
DILATIONS = (1, 4, 16)
TILE = 128
CONV_K = 31
FFN_CONV_K = 3
RMS_EPS = 1e-6
LN_EPS = 1e-5
RWKV_GN_EPS = 64e-5
NEG = -1e30
FORCE_SCORE = 1e9
SCALE = HEAD_DIM ** -0.5
LOG2_E = 1.4426950408889634
SCAN_UNROLL = 8
SLC_CHUNK = 4
V_ROWS = HEAD_DIM + 16

C_PC, C_PD, C_QD, C_QN = 0, 1024, 1536, 2304
N_PROJ = 2560
KV_KC, KV_VC, KV_KS, KV_VS, KV_KW, KV_VW, KV_GATES = 0, 64, 128, 192, 256, 320, 384
N_KV = 512
DIL_SPAN = 2048

VMEM_LIMIT = 48 * 1024 * 1024


def _cparams(*sem):
    return pltpu.CompilerParams(dimension_semantics=sem, vmem_limit_bytes=VMEM_LIMIT)


def _proj_columns():
    gw, hd = GROUP_WIDTH, HEAD_DIM
    a0 = 0
    q = np.arange(a0, a0 + gw)
    kc, vc, ks, vs, kw, vw = [np.arange(a0 + gw + i * hd, a0 + gw + (i + 1) * hd) for i in range(6)]
    gate0 = a0 + gw + 6 * hd
    gates = np.arange(gate0, gate0 + 3 * GROUP_HEADS)
    b0 = gate0 + 3 * GROUP_HEADS
    qd, kd, vd = [np.arange(b0 + i * gw, b0 + (i + 1) * gw) for i in range(3)]
    c0 = b0 + 3 * gw
    pc = np.arange(c0, c0 + 4 * gw)
    d0 = c0 + 4 * gw
    pd = np.arange(d0, d0 + 2 * gw)
    cols = np.concatenate([pc, pd, qd, kd, vd, q])
    kv_cols = np.concatenate([kc, vc, ks, vs, kw, vw, gates, np.zeros(N_KV - KV_GATES - gates.shape[0], np.int64)])
    assert cols.shape[0] == N_PROJ and kv_cols.shape[0] == N_KV
    return cols, kv_cols


def _dot_nt(a, b):
    return lax.dot_general(a, b, (((1,), (1,)), ((), ())), preferred_element_type=F32)


def _dot(a, b):
    return jnp.dot(a, b, preferred_element_type=F32)


def _seg_sum(x, ones_bf16):
    hi = x.astype(BF16)
    lo = (x - hi.astype(F32)).astype(BF16)
    return _dot(hi, ones_bf16) + _dot(lo, ones_bf16)


def _sigmoid(x):
    return 1.0 / (1.0 + jnp.exp(-x))


def _silu(x):
    return x * _sigmoid(x)


def _proj_in_kernel(x_ref, g_ref, w_ref, wkv_ref, o_ref, okv_ref, h_ref):
    @pl.when(pl.program_id(1) == 0)
    def _():
        x = x_ref[...]
        ms = jnp.mean(x * x, axis=-1, keepdims=True)
        h_ref[...] = (x * lax.rsqrt(ms + RMS_EPS) * g_ref[...]).astype(BF16)
        okv_ref[...] = _dot(h_ref[...], wkv_ref[...])

    o_ref[...] = _dot(h_ref[...], w_ref[...])


def _proj_in(x2, gain, w, w_kv, tn, tm=1024):
    m, d = x2.shape
    n, n_kv = w.shape[1], w_kv.shape[1]
    assert n % tn == 0
    return pl.pallas_call(
        _proj_in_kernel,
        grid=(m // tm, n // tn),
        in_specs=[pl.BlockSpec((tm, d), lambda i, j: (i, 0)),
                  pl.BlockSpec((1, d), lambda i, j: (0, 0)),
                  pl.BlockSpec((d, tn), lambda i, j: (0, j)),
                  pl.BlockSpec((d, n_kv), lambda i, j: (0, 0))],
        out_specs=[pl.BlockSpec((tm, tn), lambda i, j: (i, j)),
                   pl.BlockSpec((tm, n_kv), lambda i, j: (i, 0))],
        out_shape=[jax.ShapeDtypeStruct((m, n), F32), jax.ShapeDtypeStruct((m, n_kv), F32)],
        scratch_shapes=[pltpu.VMEM((tm, d), BF16)],
        compiler_params=_cparams("parallel", "arbitrary"),
        name="proj_in",
    )(x2, gain, w, w_kv)


def _compress_kernel(kh_ref, vh_ref, plo_ref, phi_ref, kw1_ref, kw2_ref, vw1_ref, vw2_ref, ko_ref, vo_ref):
    half = kh_ref.shape[2]
    n = kh_ref.shape[1]

    def mlp(xh, w1_ref, w2_ref):
        a = _dot((xh + plo_ref[...]).astype(BF16), w1_ref[0:half, :])
        bm = _dot((xh + phi_ref[...]).astype(BF16), w1_ref[half:2 * half, :])
        hid = a + pltpu.roll(bm, n - 1, axis=0)
        return _dot(_silu(hid).astype(BF16), w2_ref[...])

    ko_ref[0] = mlp(kh_ref[0], kw1_ref, kw2_ref)
    vo_ref[0] = mlp(vh_ref[0], vw1_ref, vw2_ref)


def _compress(kh, vh, pos_lo, pos_hi, kw1, kw2, vw1, vw2):
    b, n, half = kh.shape
    hid = kw1.shape[1]
    full = lambda shape: pl.BlockSpec(shape, lambda i: (0,) * len(shape))
    return pl.pallas_call(
        _compress_kernel,
        grid=(b,),
        in_specs=[pl.BlockSpec((1, n, half), lambda i: (i, 0, 0)),
                  pl.BlockSpec((1, n, half), lambda i: (i, 0, 0)),
                  full((1, half)), full((1, half)),
                  full((2 * half, hid)), full((hid, HEAD_DIM)),
                  full((2 * half, hid)), full((hid, HEAD_DIM))],
        out_specs=[pl.BlockSpec((1, n, HEAD_DIM), lambda i: (i, 0, 0))] * 2,
        out_shape=[jax.ShapeDtypeStruct((b, n, HEAD_DIM), F32)] * 2,
        compiler_params=_cparams("parallel"),
        name="nsa_compress",
    )(kh, vh, pos_lo, pos_hi, kw1, kw2, vw1, vw2)


def _cmp_select(qs_t, t0, kc_ref, vct_ref, gt_ref, score_ref):
    n_cmp = kc_ref.shape[1]
    n_blk = gt_ref.shape[0]
    heads = range(GROUP_HEADS)
    kc = kc_ref[0].astype(BF16)
    vc_t = vct_ref[0].astype(BF16)
    ci = lax.broadcasted_iota(jnp.int32, (n_cmp, TILE), 0)
    tq = t0 + lax.broadcasted_iota(jnp.int32, (n_cmp, TILE), 1)
    bias = jnp.where(jnp.where(ci < n_cmp - 1, CMP_STRIDE * ci + CMP_BLOCK - 1, 1 << 30) <= tq, 0.0, NEG)
    bias = jnp.concatenate([bias] * GROUP_HEADS, axis=1)
    s = _dot(kc, qs_t) + bias
    e = jnp.exp2(s - jnp.max(s, axis=0, keepdims=True))
    p = jnp.where(bias == 0.0, e / jnp.sum(e, axis=0, keepdims=True), 0.0)
    imp_t = functools.reduce(lambda a, b: a + b, [p[:, TILE * h:TILE * (h + 1)] for h in heads])
    o_t = _dot(vc_t, p.astype(BF16))

    hi = imp_t.astype(BF16)
    lo = (imp_t - hi.astype(F32)).astype(BF16)
    blk = _dot(gt_ref[...], hi) + _dot(gt_ref[...], lo)
    j = lax.broadcasted_iota(jnp.int32, (n_blk, TILE), 0)
    t = t0 + lax.broadcasted_iota(jnp.int32, (n_blk, TILE), 1)
    cur = lax.shift_right_logical(t, 6)
    forced = jnp.where(j == 0, 1, jnp.where(j == cur, 1, jnp.where(j == cur - 1, 1, 0)))
    score = jnp.where(j * SLC_BLOCK <= t, jnp.where(forced == 1, FORCE_SCORE, blk), NEG)
    score_ref[...] = score

    def count(jp, rank):
        row = score_ref[pl.ds(jp, 1), :]
        return rank + jnp.where(row > score, 1, jnp.where(row == score, jnp.where(j > jp, 1, 0), 0))

    n_valid = jnp.minimum((t0 + TILE) // SLC_BLOCK, n_blk)
    rank = lax.fori_loop(0, n_valid, count, jnp.zeros((n_blk, TILE), jnp.int32))
    return o_t, jnp.where(rank < N_SEL_BLOCKS, 0.0, NEG)


def _nsa_kernel(q_ref, kc_ref, vct_ref, gt_ref, gl_ref, ge_ref, ks_ref, vs_ref, kw_ref, vw_ref, beta_ref,
                o_ref, sa_ref, sb_ref, sw_ref, score_ref):
    qi = pl.program_id(1)
    cols = GROUP_HEADS * TILE
    heads = range(GROUP_HEADS)
    q_t = (q_ref[0] * (SCALE * LOG2_E)).T
    qs_t = jnp.concatenate([q_t[HEAD_DIM * h:HEAD_DIM * (h + 1), :] for h in heads], axis=1).astype(BF16)
    q_pos = qi * TILE + lax.broadcasted_iota(jnp.int32, (1, TILE), 1)
    from_t = lambda x_t: jnp.concatenate([x_t[:, TILE * h:TILE * (h + 1)] for h in heads], axis=0).T

    def keys(ref, start, n):
        return ref[0, pl.ds(start, n)].reshape(n * TILE, 2 * HEAD_DIM)

    def values_t(ref, start, n):
        v = ref[0, pl.ds(start, n)]
        return jnp.concatenate([v[u] for u in range(n)], axis=1)

    def band_bias(start, n, window):
        key_pos = start * TILE + lax.broadcasted_iota(jnp.int32, (n * TILE, TILE), 0)
        dist = q_pos - key_pos
        ok = jnp.where(dist >= 0, 1, 0) if window is None else jnp.where(dist >= 0, jnp.where(dist <= window, 1, 0), 0)
        return jnp.concatenate([jnp.where(ok == 1, 0.0, NEG)] * GROUP_HEADS, axis=1)

    def col_max(s):
        return jnp.max(s, axis=0, keepdims=True)

    def finish(acc):
        return from_t(acc[0:HEAD_DIM, :] / acc[HEAD_DIM:HEAD_DIM + 1, :])

    n_win = NSA_WINDOW // TILE + 1
    first = jnp.maximum(qi - (n_win - 1), 0)
    w_win = jnp.concatenate([qs_t, jnp.zeros((HEAD_DIM, cols), BF16)], axis=0)
    sw_ref[...] = _dot(keys(kw_ref, first, n_win), w_win) + band_bias(first, n_win, NSA_WINDOW)
    p = jnp.exp2(sw_ref[...] - col_max(sw_ref[...])).astype(BF16)
    o_win = finish(_dot(values_t(vw_ref, first, n_win), p))

    ocmp_t, sel = _cmp_select(qs_t, qi * TILE, kc_ref, vct_ref, gt_ref, score_ref)
    w = jnp.concatenate([qs_t, jnp.concatenate([sel] * GROUP_HEADS, axis=1).astype(BF16)], axis=0)

    def scores(c):
        return _dot(keys(ks_ref, c * SLC_CHUNK, SLC_CHUNK), w) + band_bias(c * SLC_CHUNK, SLC_CHUNK, None)

    def prefetch(ref, c, extra=0.0):
        s = scores(c) + extra
        ref[...] = s
        return col_max(s)

    def consume(ref, c, m, cm, acc):
        m_new = jnp.maximum(m, cm)
        p = jnp.exp2(ref[...] - m_new).astype(BF16)
        return m_new, jnp.exp2(m - m_new) * acc + _dot(values_t(vs_ref, c * SLC_CHUNK, SLC_CHUNK), p)

    def two_chunks(j, carry):
        m, cm_a, acc = carry
        cm_b = prefetch(sb_ref, 2 * j + 1)
        m, acc = consume(sa_ref, 2 * j, m, cm_a, acc)
        cm_a = prefetch(sa_ref, 2 * j + 2)
        m, acc = consume(sb_ref, 2 * j + 1, m, cm_b, acc)
        return m, cm_a, acc

    n_full = qi // SLC_CHUNK
    n_pairs = n_full // 2
    m, cm_a, acc = lax.fori_loop(0, n_pairs, two_chunks,
                                 (jnp.full((1, cols), NEG, F32), prefetch(sa_ref, 0), jnp.zeros((V_ROWS, cols), F32)))
    odd = 2 * n_pairs + 1

    def last_two():
        cm_b = prefetch(sb_ref, odd)
        m1, acc1 = consume(sa_ref, 2 * n_pairs, m, cm_a, acc)
        return consume(sb_ref, odd, m1, cm_b, acc1)[1]

    def last_one():
        return consume(sa_ref, 2 * n_pairs, m, cm_a, acc)[1]

    o_slc = finish(lax.cond(odd <= n_full, last_two, last_one))

    sg = _sigmoid(gl_ref[0])
    sg_hi = sg.astype(BF16)
    sg_lo = (sg - sg_hi.astype(F32)).astype(BF16)
    gate = lambda j: _dot(sg_hi, ge_ref[j]) + _dot(sg_lo, ge_ref[j])
    y = gate(0) * from_t(ocmp_t) + gate(1) * o_slc + gate(2) * o_win
    ms = jnp.mean(y * y, axis=-1, keepdims=True)
    o_ref[0] = y * lax.rsqrt(ms + RMS_EPS) * beta_ref[...]


def _key_tiles(k, onehot):
    b, s, _ = k.shape
    aug = jnp.zeros((b, s, HEAD_DIM), BF16) if onehot is None else jnp.broadcast_to(onehot, (b, s, HEAD_DIM))
    return jnp.concatenate([k.astype(BF16), aug], axis=-1).reshape(b, s // TILE, TILE, 2 * HEAD_DIM)


def _value_tiles_t(v):
    b, s, _ = v.shape
    v_t = v.astype(BF16).reshape(b, s // TILE, TILE, HEAD_DIM).transpose(0, 1, 3, 2)
    pad = jnp.zeros((b, s // TILE, V_ROWS - HEAD_DIM, TILE), BF16).at[:, :, 0, :].set(1.0)
    return jnp.concatenate([v_t, pad], axis=2)


def _nsa(proj, kv, k_cmp, v_cmp_t, gt, beta):
    b, s, _ = proj.shape
    gw = GROUP_WIDTH
    n_cmp = k_cmp.shape[1]
    n_blk = gt.shape[0]
    assert n_blk == HEAD_DIM, "the selection rows share the 128-deep contraction with the 64 head dims"
    piece = lambda c: kv[:, :, c:c + HEAD_DIM]
    blk_of_key = np.arange(s)[:, None] // SLC_BLOCK == np.arange(n_blk)[None, :]
    ks = _key_tiles(piece(KV_KS), jnp.asarray(blk_of_key, BF16))
    kw = _key_tiles(piece(KV_KW), None)
    vs = _value_tiles_t(piece(KV_VS))
    vw = _value_tiles_t(piece(KV_VW))
    ge = np.zeros((3, TILE, gw), np.float32)
    for j in range(3):
        for h in range(GROUP_HEADS):
            ge[j, 3 * h + j, HEAD_DIM * h:HEAD_DIM * (h + 1)] = 1.0
    gate_expand = jnp.asarray(ge, BF16)
    nt = s // TILE
    col = lambda c: pl.BlockSpec((1, TILE, gw), lambda i, j: (i, j, c // gw))
    ktile = pl.BlockSpec((1, nt, TILE, 2 * HEAD_DIM), lambda i, j: (i, 0, 0, 0))
    vtile = pl.BlockSpec((1, nt, V_ROWS, TILE), lambda i, j: (i, 0, 0, 0))
    assert nt % SLC_CHUNK == 0 and nt >= NSA_WINDOW // TILE + 1
    return pl.pallas_call(
        _nsa_kernel,
        grid=(b, nt),
        in_specs=[col(C_QN),
                  pl.BlockSpec((1, n_cmp, HEAD_DIM), lambda i, j: (i, 0, 0)),
                  pl.BlockSpec((1, HEAD_DIM, n_cmp), lambda i, j: (i, 0, 0)),
                  pl.BlockSpec(gt.shape, lambda i, j: (0, 0)),
                  pl.BlockSpec((1, TILE, TILE), lambda i, j: (i, j, KV_GATES // TILE)),
                  pl.BlockSpec(gate_expand.shape, lambda i, j: (0, 0, 0)),
                  ktile, vtile, ktile, vtile,
                  pl.BlockSpec((1, gw), lambda i, j: (0, 0))],
        out_specs=pl.BlockSpec((1, TILE, gw), lambda i, j: (i, j, 0)),
        out_shape=jax.ShapeDtypeStruct((b, s, gw), F32),
        scratch_shapes=[pltpu.VMEM((SLC_CHUNK * TILE, GROUP_HEADS * TILE), F32),
                        pltpu.VMEM((SLC_CHUNK * TILE, GROUP_HEADS * TILE), F32),
                        pltpu.VMEM((NSA_WINDOW + TILE, GROUP_HEADS * TILE), F32),
                        pltpu.VMEM((n_blk, TILE), F32)],
        compiler_params=_cparams("parallel", "arbitrary"),
        name="nsa",
    )(proj, k_cmp, v_cmp_t, gt, kv, gate_expand, ks, vs, kw, vw, beta)


def _dilated_kernel(q0_ref, q1_ref, k0_ref, k1_ref, v0_ref, v1_ref, beta_ref, o_ref, og_ref, lg_ref):
    gw = GROUP_WIDTH
    cols = GROUP_HEADS * TILE
    p0 = pl.program_id(1) * DIL_SPAN

    def load(refs, first, n, rows):
        window = pl.ds(pl.multiple_of(first, TILE), n)
        return jnp.concatenate([ref.at[0, window][rows, :] for ref in refs], axis=1)

    def store(ref, g, first, n, rows, val):
        window = pl.ds(pl.multiple_of(first, TILE), n)
        for u in range(2):
            ref.at[g, u, window][rows, :] = val[:, TILE * u:TILE * (u + 1)]

    row_head = lax.shift_right_logical(lax.broadcasted_iota(jnp.int32, (gw, TILE), 0), 6)
    rel = (lax.broadcasted_iota(jnp.int32, (2 * TILE, TILE), 1)
           - lax.broadcasted_iota(jnp.int32, (2 * TILE, TILE), 0))

    for g, dil in enumerate(DILATIONS):
        n_t = DIL_SPAN // (TILE * dil)

        def body(i, carry, r, g=g, dil=dil):
            sub0 = p0 // dil + i * TILE
            ksub0 = jnp.maximum(sub0 - TILE, 0)
            rows_q = pl.ds(r, TILE, stride=dil)
            rows_k = pl.ds(r, 2 * TILE, stride=dil)
            keys = functools.partial(load, first=dil * ksub0, n=2 * dil * TILE, rows=rows_k)
            q_t = (load((q0_ref, q1_ref), dil * sub0, dil * TILE, rows_q) * (SCALE * LOG2_E)).T
            wq = jnp.concatenate([jnp.where(row_head == h, q_t, 0.0) for h in range(GROUP_HEADS)],
                                 axis=1).astype(BF16)
            dist = rel + (sub0 - ksub0)
            bias = jnp.where(dist >= 0, jnp.where(dist <= TILE, 0.0, NEG), NEG)
            bias = jnp.concatenate([bias] * GROUP_HEADS, axis=1)
            s = _dot(keys((k0_ref, k1_ref)).astype(BF16), wq) + bias
            m = jnp.max(s, axis=0, keepdims=True)
            p = jnp.exp2(s - m)
            l = jnp.sum(p, axis=0, keepdims=True)
            p = p.astype(BF16)
            v_t = keys((v0_ref, v1_ref)).T.astype(BF16)
            lse = m + jnp.log2(l)
            o_t, l_t = [], []
            for h in range(GROUP_HEADS):
                qs = slice(TILE * h, TILE * (h + 1))
                o_t.append(_dot(v_t[HEAD_DIM * h:HEAD_DIM * (h + 1), :], p[:, qs]) / l[:, qs])
                l_t.append(jnp.broadcast_to(lse[:, qs], (HEAD_DIM, TILE)))
            store(og_ref, g, dil * i * TILE, dil * TILE, rows_q, jnp.concatenate(o_t, axis=0).T)
            store(lg_ref, g, dil * i * TILE, dil * TILE, rows_q, jnp.concatenate(l_t, axis=0).T)
            return carry

        for r in range(dil):
            if n_t == 1:
                body(0, 0, r)
            else:
                lax.fori_loop(0, n_t, functools.partial(body, r=r), 0, unroll=4)

    piece = 256

    def combine(c, carry):
        rows = pl.ds(pl.multiple_of(c * piece, piece), piece)
        both = lambda ref, g: jnp.concatenate([ref[g, 0, rows, :], ref[g, 1, rows, :]], axis=1)
        ls = [both(lg_ref, g) for g in range(len(DILATIONS))]
        mx = functools.reduce(jnp.maximum, ls)
        es = [jnp.exp2(x - mx) for x in ls]
        den = functools.reduce(lambda a, b: a + b, es)
        y = functools.reduce(lambda a, b: a + b, [(e / den) * both(og_ref, g) for g, e in enumerate(es)])
        o_ref[0, rows, :] = y * lax.rsqrt(jnp.mean(y * y, axis=-1, keepdims=True) + RMS_EPS) * beta_ref[...]
        return carry

    lax.fori_loop(0, DIL_SPAN // piece, combine, 0)


def _dilated(proj, beta):
    b, s, _ = proj.shape
    gw = GROUP_WIDTH
    assert s % DIL_SPAN == 0 and s >= 2 * TILE * max(DILATIONS)
    lane_block = lambda n: pl.BlockSpec((1, s, TILE), lambda i, j: (i, 0, C_QD // TILE + n))
    return pl.pallas_call(
        _dilated_kernel,
        grid=(b, s // DIL_SPAN),
        in_specs=[lane_block(n) for n in range(3 * gw // TILE)] + [pl.BlockSpec((1, gw), lambda i, j: (0, 0))],
        out_specs=pl.BlockSpec((1, DIL_SPAN, gw), lambda i, j: (i, j, 0)),
        out_shape=jax.ShapeDtypeStruct((b, s, gw), F32),
        scratch_shapes=[pltpu.VMEM((len(DILATIONS), gw // TILE, DIL_SPAN, TILE), F32)] * 2,
        compiler_params=_cparams("parallel", "arbitrary"),
        name="dilated",
    )(*([proj] * (3 * gw // TILE)), beta)


def _rwkv_prep_kernel(cur_ref, halo_ref, mu_ref, w0_ref, wup_ref, a0_ref, aup_ref, gup_ref, kk_w_ref, ka_ref,
                      rk_ref, ones_ref,
                      kk_o, w_o, b_o, k_o, yr_o, vkr_o, v_o, g_o, c_o):
    gw = GROUP_WIDTH
    tm = cur_ref.shape[1]
    ones = ones_ref[...]
    x = cur_ref[0]
    last = halo_ref[0][7:8, :]
    last = jnp.where(pl.program_id(1) > 0, last, 0.0)
    rid = lax.broadcasted_iota(jnp.int32, (tm, 1), 0)
    xprev = jnp.where(rid == 0, last, pltpu.roll(x, 1, axis=0))
    xs = x + (xprev - x) * mu_ref[...]
    r, k, v = xs[:, 0:gw], xs[:, gw:2 * gw], xs[:, 2 * gw:3 * gw]
    wd, ad, gd = xs[:, 3 * gw:3 * gw + 64], xs[:, 3 * gw + 64:3 * gw + 128], xs[:, 3 * gw + 128:4 * gw]
    z = -(w0_ref[...] + _dot(jnp.tanh(wd).astype(BF16), wup_ref[...]))
    softplus = jnp.maximum(z, 0.0) + jnp.log1p(jnp.exp(-jnp.abs(z)))
    w = jnp.exp(-jnp.exp(-softplus - 0.5))
    a = _sigmoid(a0_ref[...] + _dot(ad.astype(BF16), aup_ref[...]))
    g = _dot(_sigmoid(gd).astype(BF16), gup_ref[...])
    kk = k * kk_w_ref[...]
    kk = kk * lax.rsqrt(_seg_sum(kk * kk, ones) + 1e-12)
    k2 = k * (1.0 + (a - 1.0) * ka_ref[...])
    b = kk * a
    br = _seg_sum(b * r, ones)
    kr = _seg_sum(k2 * r, ones)
    kk_o[0] = kk
    w_o[0] = w
    b_o[0] = b
    k_o[0] = k2
    yr_o[0] = w * r - kk * br
    vkr_o[0] = v * kr
    v_o[0] = v
    g_o[0] = g
    c_o[0] = _seg_sum(r * k2 * rk_ref[...], ones)


def _rwkv_prep(proj, p, ones, tm=256):
    b, s, _ = proj.shape
    gw = GROUP_WIDTH
    vec = lambda n: pl.BlockSpec((1, n), lambda i, j: (0, 0))
    mat = lambda a: pl.BlockSpec(a.shape, lambda i, j: (0, 0))
    out = pl.BlockSpec((1, tm, gw), lambda i, j: (i, j, 0))
    return pl.pallas_call(
        _rwkv_prep_kernel,
        grid=(b, s // tm),
        in_specs=[pl.BlockSpec((1, tm, 4 * gw), lambda i, j: (i, j, C_PC // (4 * gw))),
                  pl.BlockSpec((1, 8, 4 * gw), lambda i, j: (i, jnp.maximum(j * (tm // 8) - 1, 0), C_PC // (4 * gw))),
                  vec(4 * gw), vec(gw), mat(p["w_up"]), vec(gw), mat(p["a_up"]), mat(p["g_up"]),
                  vec(gw), vec(gw), vec(gw), mat(ones)],
        out_specs=[out] * 9,
        out_shape=[jax.ShapeDtypeStruct((b, s, gw), F32)] * 9,
        compiler_params=_cparams("parallel", "arbitrary"),
        name="rwkv_prep",
    )(proj, proj, p["mu"], p["w0"], p["w_up"], p["a0"], p["a_up"], p["g_up"], p["k_k"], p["k_a"], p["r_k"], ones)


def _rwkv_scan_kernel(kk_ref, w_ref, b_ref, k_ref, yr_ref, vkr_ref, v_ref, ones_ref, diag_ref,
                      y_ref, state_ref, yacc_ref):
    nb, tb = kk_ref.shape[0], kk_ref.shape[1]

    @pl.when(pl.program_id(0) == 0)
    def _():
        state_ref[...] = jnp.zeros(state_ref.shape, F32)
        yacc_ref[...] = jnp.zeros(yacc_ref.shape, F32)

    ones = ones_ref[...]
    diag = diag_ref[...].astype(BF16)
    lane = jnp.bitwise_and(lax.broadcasted_iota(jnp.int32, (HEAD_DIM, GROUP_WIDTH), 1), HEAD_DIM - 1)

    def group(c, carry):
        def step(tt, carry2):
            t = c * HEAD_DIM + tt
            here = lane == tt
            row = lambda ref, i: ref[i, pl.ds(t, 1), :]
            st = [state_ref[i] for i in range(nb)]
            sb = [st[i].astype(BF16) for i in range(nb)]
            rb = lambda ref, i: row(ref, i).astype(BF16)
            reduce = lambda parts: _dot(jnp.concatenate(parts, axis=0), ones)
            red_v = reduce([rb(v_ref, i) * diag for i in range(nb)])
            red_s = reduce([sb[i] * rb(kk_ref, i) for i in range(nb)])
            red_y = reduce([sb[i] * rb(yr_ref, i) + rb(vkr_ref, i) * diag for i in range(nb)])
            part = lambda red, n: red[n * HEAD_DIM:(n + 1) * HEAD_DIM, :]
            for i in range(nb):
                sa, vcol, ycol = part(red_s, i), part(red_v, i), part(red_y, i)
                state_ref[i] = st[i] * row(w_ref, i) - sa * row(b_ref, i) + vcol * row(k_ref, i)
                yacc_ref[i] = jnp.where(here, ycol, yacc_ref[i])
            return carry2

        lax.fori_loop(0, HEAD_DIM, step, 0, unroll=SCAN_UNROLL)
        rows = pl.ds(pl.multiple_of(c * HEAD_DIM, HEAD_DIM), HEAD_DIM)
        for i in range(nb):
            full_t = jnp.concatenate([yacc_ref[i], jnp.zeros((HEAD_DIM, GROUP_WIDTH), F32)], axis=0).T
            y_ref[i, rows, :] = jnp.concatenate(
                [full_t[HEAD_DIM * h:HEAD_DIM * (h + 1), 0:HEAD_DIM] for h in range(GROUP_HEADS)], axis=1)
        return carry

    lax.fori_loop(0, tb // HEAD_DIM, group, 0)


def _rwkv_scan(kk, w, b, k, yr, vkr, v, ones, diag, tb=128):
    nb, s, gw = kk.shape
    inp = pl.BlockSpec((nb, tb, gw), lambda j: (0, j, 0))
    const = lambda a: pl.BlockSpec(a.shape, lambda j: (0, 0))
    return pl.pallas_call(
        _rwkv_scan_kernel,
        grid=(s // tb,),
        in_specs=[inp] * 7 + [const(ones), const(diag)],
        out_specs=inp,
        out_shape=jax.ShapeDtypeStruct((nb, s, gw), F32),
        scratch_shapes=[pltpu.VMEM((nb, HEAD_DIM, gw), F32), pltpu.VMEM((nb, HEAD_DIM, gw), F32)],
        compiler_params=_cparams("arbitrary"),
        name="rwkv_scan",
    )(kk, w, b, k, yr, vkr, v, ones, diag)


def _conformer_kernel(cur_ref, halo_ref, dw_ref, dwb_ref, lng_ref, lnb_ref, o_ref, g_ref, sh_ref):
    gw = GROUP_WIDTH
    tm = cur_ref.shape[1]
    hl = halo_ref.shape[1]
    sub = 8
    glu = lambda x: x[:, 0:gw] * _sigmoid(x[:, gw:2 * gw])
    g_ref[0:hl, :] = jnp.where(pl.program_id(1) > 0, glu(halo_ref[0]), 0.0)
    g_ref[hl:hl + tm, :] = glu(cur_ref[0])
    for r in range(1, sub):
        sh_ref[r - 1] = g_ref[pl.ds(r, sh_ref.shape[1]), :]
    acc = jnp.zeros((tm, gw), F32)
    for j in range(CONV_K):
        off = hl - (CONV_K - 1) + j
        r, a = off % sub, off - off % sub
        tap = g_ref[pl.ds(a, tm), :] if r == 0 else sh_ref[r - 1, pl.ds(a, tm), :]
        acc = acc + dw_ref[j:j + 1, :] * tap
    h = acc + dwb_ref[...]
    mu = jnp.mean(h, axis=-1, keepdims=True)
    d = h - mu
    var = jnp.mean(d * d, axis=-1, keepdims=True)
    o_ref[0] = _silu(d * lax.rsqrt(var + LN_EPS) * lng_ref[...] + lnb_ref[...])


def _conformer(proj, dw, dwb, lng, lnb, tm=512, hl=32):
    b, s, _ = proj.shape
    gw = GROUP_WIDTH
    vec = pl.BlockSpec((1, gw), lambda i, j: (0, 0))
    return pl.pallas_call(
        _conformer_kernel,
        grid=(b, s // tm),
        in_specs=[pl.BlockSpec((1, tm, 2 * gw), lambda i, j: (i, j, C_PD // (2 * gw))),
                  pl.BlockSpec((1, hl, 2 * gw), lambda i, j: (i, jnp.maximum(j * (tm // hl) - 1, 0), C_PD // (2 * gw))),
                  pl.BlockSpec((CONV_K, gw), lambda i, j: (0, 0)), vec, vec, vec],
        out_specs=pl.BlockSpec((1, tm, gw), lambda i, j: (i, j, 0)),
        out_shape=jax.ShapeDtypeStruct((b, s, gw), F32),
        scratch_shapes=[pltpu.VMEM((hl + tm, gw), F32), pltpu.VMEM((7, hl + tm - 8, gw), F32)],
        compiler_params=_cparams("parallel", "arbitrary"),
        name="conformer_conv",
    )(proj, proj, dw, dwb, lng, lnb)


def _out_proj_kernel(x_ref, ya_ref, yb_ref, ys_ref, c_ref, v_ref, g_ref, yd_ref, lng_ref, lnb_ref, ones_ref, w_ref,
                     o_ref):
    gw = GROUP_WIDTH
    ones = ones_ref[...]
    y = ys_ref[...] + c_ref[...] * v_ref[...]
    mu = _seg_sum(y, ones) * (1.0 / HEAD_DIM)
    d = y - mu
    var = _seg_sum(d * d, ones) * (1.0 / HEAD_DIM)
    yc = (d * lax.rsqrt(var + RWKV_GN_EPS) * lng_ref[...] + lnb_ref[...]) * g_ref[...]
    acc = x_ref[...]
    for n, part in enumerate((ya_ref[...], yb_ref[...], yc, yd_ref[...])):
        acc = acc + _dot(part.astype(BF16), w_ref[n * gw:(n + 1) * gw, :])
    o_ref[...] = acc


def _out_proj(x2, parts, vecs, ones, w, tm=512):
    m, d = x2.shape
    gw = GROUP_WIDTH
    blk = pl.BlockSpec((tm, gw), lambda i: (i, 0))
    vec = pl.BlockSpec((1, gw), lambda i: (0, 0))
    return pl.pallas_call(
        _out_proj_kernel,
        grid=(m // tm,),
        in_specs=[pl.BlockSpec((tm, d), lambda i: (i, 0))] + [blk] * len(parts) + [vec] * len(vecs)
                 + [pl.BlockSpec(ones.shape, lambda i: (0, 0)), pl.BlockSpec(w.shape, lambda i: (0, 0))],
        out_specs=pl.BlockSpec((tm, d), lambda i: (i, 0)),
        out_shape=jax.ShapeDtypeStruct((m, d), F32),
        compiler_params=_cparams("parallel"),
        name="out_proj",
    )(x2, *parts, *vecs, ones, w)


def _ffn_kernel(x_ref, halo_ref, gain_ref, upg_ref, upv_ref, dwg_ref, dwv_ref, bg_ref, bv_ref, down_ref, fin_ref,
                o_ref, h_ref, acc_ref, ug_ref, uv_ref, *, tiles_per_seq, final_norm):
    j = pl.program_id(1)
    tm = x_ref.shape[0]
    hl = halo_ref.shape[0]

    def norm(x):
        return (x * lax.rsqrt(jnp.mean(x * x, axis=-1, keepdims=True) + RMS_EPS) * gain_ref[...]).astype(BF16)

    @pl.when(j == 0)
    def _():
        first = pl.program_id(0) % tiles_per_seq == 0
        h_ref[0:hl, :] = jnp.where(first, jnp.zeros((hl, x_ref.shape[1]), BF16), norm(halo_ref[...]))
        h_ref[hl:hl + tm, :] = norm(x_ref[...])
        acc_ref[...] = jnp.zeros(acc_ref.shape, F32)

    h = h_ref[...]

    def conv(up_ref, dw_ref, b_ref, u_ref):
        u_ref[...] = _dot(h, up_ref[...])
        out = b_ref[...]
        for tap in range(FFN_CONV_K):
            out = out + dw_ref[tap:tap + 1, :] * u_ref[pl.ds(hl - (FFN_CONV_K - 1) + tap, tm), :]
        return out

    act = _silu(conv(upg_ref, dwg_ref, bg_ref, ug_ref)) * conv(upv_ref, dwv_ref, bv_ref, uv_ref)
    acc_ref[...] += _dot(act.astype(BF16), down_ref[...])

    @pl.when(j == pl.num_programs(1) - 1)
    def _():
        y = x_ref[...] + acc_ref[...]
        if final_norm:
            y = y * lax.rsqrt(jnp.mean(y * y, axis=-1, keepdims=True) + RMS_EPS) * fin_ref[...]
        o_ref[...] = y


def _ffn(x2, gain, up, dw, dwb, down, fin, seq_len, final_norm, tm=512, tn=1408, hl=16):
    m, d = x2.shape
    dff = down.shape[0]
    nj = dff // tn
    kern = functools.partial(_ffn_kernel, tiles_per_seq=seq_len // tm, final_norm=final_norm)
    return pl.pallas_call(
        kern,
        grid=(m // tm, nj),
        in_specs=[pl.BlockSpec((tm, d), lambda i, j: (i, 0)),
                  pl.BlockSpec((hl, d), lambda i, j: (jnp.maximum(i * (tm // hl) - 1, 0), 0)),
                  pl.BlockSpec((1, d), lambda i, j: (0, 0)),
                  pl.BlockSpec((d, tn), lambda i, j: (0, j)),
                  pl.BlockSpec((d, tn), lambda i, j: (0, j + nj)),
                  pl.BlockSpec((FFN_CONV_K, tn), lambda i, j: (0, j)),
                  pl.BlockSpec((FFN_CONV_K, tn), lambda i, j: (0, j + nj)),
                  pl.BlockSpec((1, tn), lambda i, j: (0, j)),
                  pl.BlockSpec((1, tn), lambda i, j: (0, j + nj)),
                  pl.BlockSpec((tn, d), lambda i, j: (j, 0)),
                  pl.BlockSpec((1, d), lambda i, j: (0, 0))],
        out_specs=pl.BlockSpec((tm, d), lambda i, j: (i, 0)),
        out_shape=jax.ShapeDtypeStruct((m, d), F32),
        scratch_shapes=[pltpu.VMEM((hl + tm, d), BF16), pltpu.VMEM((tm, d), F32),
                        pltpu.VMEM((hl + tm, tn), F32), pltpu.VMEM((hl + tm, tn), F32)],
        compiler_params=_cparams("parallel", "arbitrary"),
        name="conv_ffn",
    )(x2, x2, gain, up, up, dw, dw, dwb, dwb, down, fin)


def _block_ones():
    i = np.arange(GROUP_WIDTH) // HEAD_DIM
    return jnp.asarray(i[:, None] == i[None, :], BF16)


def _diag_mask():
    v = np.arange(HEAD_DIM)
    l = np.arange(GROUP_WIDTH) % HEAD_DIM
    return jnp.asarray(v[:, None] == l[None, :], F32)


def _importance_map(n_cmp, n_blk):
    gt = np.zeros((n_blk, n_cmp), np.float32)
    per = SLC_BLOCK // CMP_STRIDE
    for c in range(n_blk * per):
        for i in (c - 1, c):
            if 0 <= i < n_cmp - 1:
                gt[c // per, i] += 1.0
    return jnp.asarray(gt, BF16)


def kernel(x, norm_mix, w_in, cmp_pos, cmp_k_w1, cmp_k_w2, cmp_v_w1, cmp_v_w2, beta_nsa, beta_dil, rwkv_mu, rwkv_w0, rwkv_w_up, rwkv_a0, rwkv_a_up, rwkv_g_up, rwkv_k_k, rwkv_k_a, rwkv_r_k, rwkv_ln_g, rwkv_ln_b, conv_dw, conv_dw_b, conv_ln_g, conv_ln_b, w_out, norm_ffn, ffn_up, ffn_dw, ffn_dw_b, ffn_down, norm_final):
    bsz, seq, d = x.shape
    depth = w_in.shape[0]
    gw = GROUP_WIDTH
    n_cmp = seq // CMP_STRIDE
    n_blk = seq // SLC_BLOCK
    half = CMP_STRIDE * HEAD_DIM

    cols, kv_cols = _proj_columns()
    w_in_p = w_in[:, :, cols].astype(BF16)
    w_kv_p = w_in[:, :, kv_cols].astype(BF16)
    w_out_b = w_out.astype(BF16)
    ffn_up_b = ffn_up.astype(BF16)
    ffn_down_b = ffn_down.astype(BF16)
    ones = _block_ones()
    diag = _diag_mask()
    gt = _importance_map(n_cmp, n_blk)
    row = lambda a: a.reshape(1, -1)

    x2 = x.reshape(bsz * seq, d)
    for l in range(depth):
        proj, kv = _proj_in(x2, row(norm_mix[l]), w_in_p[l], w_kv_p[l], tn=N_PROJ // 2)
        proj, kv = proj.reshape(bsz, seq, N_PROJ), kv.reshape(bsz, seq, N_KV)

        kh = kv[:, :, KV_KC:KV_KC + HEAD_DIM].reshape(bsz, n_cmp, half)
        vh = kv[:, :, KV_VC:KV_VC + HEAD_DIM].reshape(bsz, n_cmp, half)
        pos = cmp_pos[l].reshape(1, 2 * half)
        k_cmp, v_cmp = _compress(kh, vh, pos[:, :half], pos[:, half:],
                                 cmp_k_w1[l].astype(BF16), cmp_k_w2[l].astype(BF16),
                                 cmp_v_w1[l].astype(BF16), cmp_v_w2[l].astype(BF16))
        y_a = _nsa(proj, kv, k_cmp, jnp.swapaxes(v_cmp, 1, 2), gt, row(beta_nsa[l]))

        y_b = _dilated(proj, row(beta_dil[l]))

        rp = dict(mu=row(rwkv_mu[l]), w0=row(rwkv_w0[l]), w_up=rwkv_w_up[l].astype(BF16), a0=row(rwkv_a0[l]),
                  a_up=rwkv_a_up[l].astype(BF16), g_up=rwkv_g_up[l].astype(BF16), k_k=row(rwkv_k_k[l]),
                  k_a=row(rwkv_k_a[l]), r_k=row(rwkv_r_k[l]))
        kk, w, b, k2, yr, vkr, v, g, c = _rwkv_prep(proj, rp, ones)
        y_scan = _rwkv_scan(kk, w, b, k2, yr, vkr, v, ones, diag)

        y_d = _conformer(proj, conv_dw[l], row(conv_dw_b[l]), row(conv_ln_g[l]), row(conv_ln_b[l]))

        flat = lambda a: a.reshape(bsz * seq, gw)
        parts = [flat(a) for a in (y_a, y_b, y_scan, c, v, g, y_d)]
        vecs = [row(rwkv_ln_g[l]), row(rwkv_ln_b[l])]
        x2 = _out_proj(x2, parts, vecs, ones, w_out_b[l])

        x2 = _ffn(x2, row(norm_ffn[l]), ffn_up_b[l], ffn_dw[l], row(ffn_dw_b[l]), ffn_down_b[l],
                  row(norm_final), seq, final_norm=(l == depth - 1))
    return x2.reshape(bsz, seq, d)
```

```python
import functools

import jax
import jax.numpy as jnp
import numpy as np
from jax import lax
from jax.experimental import pallas as pl
from jax.experimental.pallas import tpu as pltpu

F32 = jnp.float32
BF16 = jnp.bfloat16

HEAD_DIM = 64
GROUP_HEADS = 4
GROUP_WIDTH = HEAD_DIM * GROUP_HEADS
CMP_STRIDE = 16
CMP_BLOCK = 32
SLC_BLOCK = 64
N_SEL_BLOCKS = 16
NSA_WINDOW = 512
DILATIONS = (1, 4, 16)
TILE = 128
CONV_K = 31
FFN_CONV_K = 3
RMS_EPS = 1e-6
LN_EPS = 1e-5
RWKV_GN_EPS = 64e-5
NEG = -1e30
FORCE_SCORE = 1e9
SCALE = HEAD_DIM ** -0.5
LOG2_E = 1.4426950408889634
SCAN_UNROLL = 8
SLC_CHUNK = 4
V_ROWS = HEAD_DIM + 16

C_PC, C_PD, C_QD, C_QN = 0, 1024, 1536, 2304
N_PROJ = 2560
KV_KC, KV_VC, KV_KS, KV_VS, KV_KW, KV_VW, KV_GATES = 0, 64, 128, 192, 256, 320, 384
N_KV = 512
DIL_SPAN = 2048

VMEM_LIMIT = 48 * 1024 * 1024


def _cparams(*sem):
    return pltpu.CompilerParams(dimension_semantics=sem, vmem_limit_bytes=VMEM_LIMIT)


def _proj_columns():
    gw, hd = GROUP_WIDTH, HEAD_DIM
    a0 = 0
    q = np.arange(a0, a0 + gw)
    kc, vc, ks, vs, kw, vw = [np.arange(a0 + gw + i * hd, a0 + gw + (i + 1) * hd) for i in range(6)]
    gate0 = a0 + gw + 6 * hd
    gates = np.arange(gate0, gate0 + 3 * GROUP_HEADS)
    b0 = gate0 + 3 * GROUP_HEADS
    qd, kd, vd = [np.arange(b0 + i * gw, b0 + (i + 1) * gw) for i in range(3)]
    c0 = b0 + 3 * gw
    pc = np.arange(c0, c0 + 4 * gw)
    d0 = c0 + 4 * gw
    pd = np.arange(d0, d0 + 2 * gw)
    cols = np.concatenate([pc, pd, qd, kd, vd, q])
    kv_cols = np.concatenate([kc, vc, ks, vs, kw, vw, gates, np.zeros(N_KV - KV_GATES - gates.shape[0], np.int64)])
    assert cols.shape[0] == N_PROJ and kv_cols.shape[0] == N_KV
    return cols, kv_cols


def _dot(a, b):
    return jnp.dot(a, b, preferred_element_type=F32)


def _seg_sum(x, ones_bf16):
    hi = x.astype(BF16)
    lo = (x - hi.astype(F32)).astype(BF16)
    return _dot(hi, ones_bf16) + _dot(lo, ones_bf16)


def _sigmoid(x):
    return 1.0 / (1.0 + jnp.exp(-x))


def _silu(x):
    return x * _sigmoid(x)


def _proj_in_kernel(x_ref, g_ref, w_ref, wkv_ref, o_ref, okv_ref, h_ref):
    @pl.when(pl.program_id(1) == 0)
    def _():
        x = x_ref[...]
        ms = jnp.mean(x * x, axis=-1, keepdims=True)
        h_ref[...] = (x * lax.rsqrt(ms + RMS_EPS) * g_ref[...]).astype(BF16)
        okv_ref[...] = _dot(h_ref[...], wkv_ref[...])

    o_ref[...] = _dot(h_ref[...], w_ref[...])


def _proj_in(x2, gain, w, w_kv, tn, tm=1024):
    m, d = x2.shape
    n, n_kv = w.shape[1], w_kv.shape[1]
    assert n % tn == 0
    return pl.pallas_call(
        _proj_in_kernel,
        grid=(m // tm, n // tn),
        in_specs=[pl.BlockSpec((tm, d), lambda i, j: (i, 0)),
                  pl.BlockSpec((1, d), lambda i, j: (0, 0)),
                  pl.BlockSpec((d, tn), lambda i, j: (0, j)),
                  pl.BlockSpec((d, n_kv), lambda i, j: (0, 0))],
        out_specs=[pl.BlockSpec((tm, tn), lambda i, j: (i, j)),
                   pl.BlockSpec((tm, n_kv), lambda i, j: (i, 0))],
        out_shape=[jax.ShapeDtypeStruct((m, n), F32), jax.ShapeDtypeStruct((m, n_kv), F32)],
        scratch_shapes=[pltpu.VMEM((tm, d), BF16)],
        compiler_params=_cparams("parallel", "arbitrary"),
        name="proj_in",
    )(x2, gain, w, w_kv)


def _compress_kernel(kh_ref, vh_ref, plo_ref, phi_ref, kw1_ref, kw2_ref, vw1_ref, vw2_ref, ko_ref, vo_ref):
    half = kh_ref.shape[2]
    n = kh_ref.shape[1]

    def mlp(xh, w1_ref, w2_ref):
        a = _dot((xh + plo_ref[...]).astype(BF16), w1_ref[0:half, :])
        bm = _dot((xh + phi_ref[...]).astype(BF16), w1_ref[half:2 * half, :])
        hid = a + pltpu.roll(bm, n - 1, axis=0)
        return _dot(_silu(hid).astype(BF16), w2_ref[...])

    ko_ref[0] = mlp(kh_ref[0], kw1_ref, kw2_ref)
    vo_ref[0] = mlp(vh_ref[0], vw1_ref, vw2_ref)


def _compress(kh, vh, pos_lo, pos_hi, kw1, kw2, vw1, vw2):
    b, n, half = kh.shape
    hid = kw1.shape[1]
    full = lambda shape: pl.BlockSpec(shape, lambda i: (0,) * len(shape))
    return pl.pallas_call(
        _compress_kernel,
        grid=(b,),
        in_specs=[pl.BlockSpec((1, n, half), lambda i: (i, 0, 0)),
                  pl.BlockSpec((1, n, half), lambda i: (i, 0, 0)),
                  full((1, half)), full((1, half)),
                  full((2 * half, hid)), full((hid, HEAD_DIM)),
                  full((2 * half, hid)), full((hid, HEAD_DIM))],
        out_specs=[pl.BlockSpec((1, n, HEAD_DIM), lambda i: (i, 0, 0))] * 2,
        out_shape=[jax.ShapeDtypeStruct((b, n, HEAD_DIM), F32)] * 2,
        compiler_params=_cparams("parallel"),
        name="nsa_compress",
    )(kh, vh, pos_lo, pos_hi, kw1, kw2, vw1, vw2)


def _cmp_select(qs_t, t0, kc_ref, vct_ref, gt_ref, score_ref):
    n_cmp = kc_ref.shape[1]
    n_blk = gt_ref.shape[0]
    heads = range(GROUP_HEADS)
    kc = kc_ref[0].astype(BF16)
    vc_t = vct_ref[0].astype(BF16)
    ci = lax.broadcasted_iota(jnp.int32, (n_cmp, TILE), 0)
    tq = t0 + lax.broadcasted_iota(jnp.int32, (n_cmp, TILE), 1)
    bias = jnp.where(jnp.where(ci < n_cmp - 1, CMP_STRIDE * ci + CMP_BLOCK - 1, 1 << 30) <= tq, 0.0, NEG)
    bias = jnp.concatenate([bias] * GROUP_HEADS, axis=1)
    s = _dot(kc, qs_t) + bias
    e = jnp.exp2(s - jnp.max(s, axis=0, keepdims=True))
    p = jnp.where(bias == 0.0, e / jnp.sum(e, axis=0, keepdims=True), 0.0)
    imp_t = functools.reduce(lambda a, b: a + b, [p[:, TILE * h:TILE * (h + 1)] for h in heads])
    o_t = _dot(vc_t, p.astype(BF16))

    hi = imp_t.astype(BF16)
    lo = (imp_t - hi.astype(F32)).astype(BF16)
    blk = _dot(gt_ref[...], hi) + _dot(gt_ref[...], lo)
    j = lax.broadcasted_iota(jnp.int32, (n_blk, TILE), 0)
    t = t0 + lax.broadcasted_iota(jnp.int32, (n_blk, TILE), 1)
    cur = lax.shift_right_logical(t, 6)
    forced = jnp.where(j == 0, 1, jnp.where(j == cur, 1, jnp.where(j == cur - 1, 1, 0)))
    score = jnp.where(j * SLC_BLOCK <= t, jnp.where(forced == 1, FORCE_SCORE, blk), NEG)
    score_ref[...] = score

    def count(pair, rank):
        for jp in (2 * pair, 2 * pair + 1):
            row = score_ref[pl.ds(jp, 1), :]
            rank = rank + jnp.where(row > score, 1, jnp.where(row == score, jnp.where(j > jp, 1, 0), 0))
        return rank

    n_valid = jnp.minimum((t0 + TILE) // SLC_BLOCK, n_blk)
    rank = lax.fori_loop(0, n_valid // 2, count, jnp.zeros((n_blk, TILE), jnp.int32))
    return o_t, jnp.where(rank < N_SEL_BLOCKS, 0.0, NEG)


def _nsa_kernel(q_ref, kc_ref, vct_ref, gt_ref, gl_ref, ge_ref, ks_ref, vs_ref, kw_ref, vw_ref, beta_ref,
                o_ref, sa_ref, sb_ref, sw_ref, score_ref):
    qi = pl.program_id(1)
    cols = GROUP_HEADS * TILE
    heads = range(GROUP_HEADS)
    q_t = (q_ref[0] * (SCALE * LOG2_E)).T
    qs_t = jnp.concatenate([q_t[HEAD_DIM * h:HEAD_DIM * (h + 1), :] for h in heads], axis=1).astype(BF16)
    q_pos = qi * TILE + lax.broadcasted_iota(jnp.int32, (1, TILE), 1)
    from_t = lambda x_t: jnp.concatenate([x_t[:, TILE * h:TILE * (h + 1)] for h in heads], axis=0).T

    def keys(ref, start, n):
        return ref[0, pl.ds(start, n)].reshape(n * TILE, 2 * HEAD_DIM)

    def values_t(ref, start, n):
        v = ref[0, pl.ds(start, n)]
        return jnp.concatenate([v[u] for u in range(n)], axis=1)

    def band_bias(start, n, window):
        key_pos = start * TILE + lax.broadcasted_iota(jnp.int32, (n * TILE, TILE), 0)
        dist = q_pos - key_pos
        ok = jnp.where(dist >= 0, 1, 0) if window is None else jnp.where(dist >= 0, jnp.where(dist <= window, 1, 0), 0)
        return jnp.concatenate([jnp.where(ok == 1, 0.0, NEG)] * GROUP_HEADS, axis=1)

    def col_max(s):
        return jnp.max(s, axis=0, keepdims=True)

    def finish(acc):
        return from_t(acc[0:HEAD_DIM, :] / acc[HEAD_DIM:HEAD_DIM + 1, :])

    n_win = NSA_WINDOW // TILE + 1
    first = jnp.maximum(qi - (n_win - 1), 0)
    w_win = jnp.concatenate([qs_t, jnp.zeros((HEAD_DIM, cols), BF16)], axis=0)
    sw_ref[...] = _dot(keys(kw_ref, first, n_win), w_win) + band_bias(first, n_win, NSA_WINDOW)
    p = jnp.exp2(sw_ref[...] - col_max(sw_ref[...])).astype(BF16)
    o_win = finish(_dot(values_t(vw_ref, first, n_win), p))

    ocmp_t, sel = _cmp_select(qs_t, qi * TILE, kc_ref, vct_ref, gt_ref, score_ref)
    w = jnp.concatenate([qs_t, jnp.concatenate([sel] * GROUP_HEADS, axis=1).astype(BF16)], axis=0)

    def prefetch(ref, c, may_hold_diagonal=True):
        s = _dot(keys(ks_ref, c * SLC_CHUNK, SLC_CHUNK), w)
        if may_hold_diagonal:
            s = s + band_bias(c * SLC_CHUNK, SLC_CHUNK, None)
        ref[...] = s
        return col_max(s)

    def consume(ref, c, m, cm, acc):
        m_new = jnp.maximum(m, cm)
        p = jnp.exp2(ref[...] - m_new).astype(BF16)
        return m_new, jnp.exp2(m - m_new) * acc + _dot(values_t(vs_ref, c * SLC_CHUNK, SLC_CHUNK), p)

    def two_chunks(j, carry):
        m, cm_a, acc = carry
        cm_b = prefetch(sb_ref, 2 * j + 1, may_hold_diagonal=False)
        m, acc = consume(sa_ref, 2 * j, m, cm_a, acc)
        cm_a = prefetch(sa_ref, 2 * j + 2)
        m, acc = consume(sb_ref, 2 * j + 1, m, cm_b, acc)
        return m, cm_a, acc

    n_full = qi // SLC_CHUNK
    n_pairs = n_full // 2
    m, cm_a, acc = lax.fori_loop(0, n_pairs, two_chunks,
                                 (jnp.full((1, cols), NEG, F32), prefetch(sa_ref, 0), jnp.zeros((V_ROWS, cols), F32)))
    odd = 2 * n_pairs + 1

    def last_two():
        cm_b = prefetch(sb_ref, odd)
        m1, acc1 = consume(sa_ref, 2 * n_pairs, m, cm_a, acc)
        return consume(sb_ref, odd, m1, cm_b, acc1)[1]

    def last_one():
        return consume(sa_ref, 2 * n_pairs, m, cm_a, acc)[1]

    o_slc = finish(lax.cond(odd <= n_full, last_two, last_one))

    sg = _sigmoid(gl_ref[0])
    sg_hi = sg.astype(BF16)
    sg_lo = (sg - sg_hi.astype(F32)).astype(BF16)
    gate = lambda j: _dot(sg_hi, ge_ref[j]) + _dot(sg_lo, ge_ref[j])
    y = gate(0) * from_t(ocmp_t) + gate(1) * o_slc + gate(2) * o_win
    ms = jnp.mean(y * y, axis=-1, keepdims=True)
    o_ref[0] = y * lax.rsqrt(ms + RMS_EPS) * beta_ref[...]


def _key_tiles(k, onehot):
    b, s, _ = k.shape
    aug = jnp.zeros((b, s, HEAD_DIM), BF16) if onehot is None else jnp.broadcast_to(onehot, (b, s, HEAD_DIM))
    return jnp.concatenate([k.astype(BF16), aug], axis=-1).reshape(b, s // TILE, TILE, 2 * HEAD_DIM)


def _value_tiles_t(v):
    b, s, _ = v.shape
    v_t = v.astype(BF16).reshape(b, s // TILE, TILE, HEAD_DIM).transpose(0, 1, 3, 2)
    pad = jnp.zeros((b, s // TILE, V_ROWS - HEAD_DIM, TILE), BF16).at[:, :, 0, :].set(1.0)
    return jnp.concatenate([v_t, pad], axis=2)


def _nsa(proj, kv, k_cmp, v_cmp_t, gt, beta):
    b, s, _ = proj.shape
    gw = GROUP_WIDTH
    n_cmp = k_cmp.shape[1]
    n_blk = gt.shape[0]
    assert n_blk == HEAD_DIM, "the selection rows share the 128-deep contraction with the 64 head dims"
    piece = lambda c: kv[:, :, c:c + HEAD_DIM]
    blk_of_key = np.arange(s)[:, None] // SLC_BLOCK == np.arange(n_blk)[None, :]
    ks = _key_tiles(piece(KV_KS), jnp.asarray(blk_of_key, BF16))
    kw = _key_tiles(piece(KV_KW), None)
    vs = _value_tiles_t(piece(KV_VS))
    vw = _value_tiles_t(piece(KV_VW))
    ge = np.zeros((3, TILE, gw), np.float32)
    for j in range(3):
        for h in range(GROUP_HEADS):
            ge[j, 3 * h + j, HEAD_DIM * h:HEAD_DIM * (h + 1)] = 1.0
    gate_expand = jnp.asarray(ge, BF16)
    nt = s // TILE
    col = lambda c: pl.BlockSpec((1, TILE, gw), lambda i, j: (i, j, c // gw))
    ktile = pl.BlockSpec((1, nt, TILE, 2 * HEAD_DIM), lambda i, j: (i, 0, 0, 0))
    vtile = pl.BlockSpec((1, nt, V_ROWS, TILE), lambda i, j: (i, 0, 0, 0))
    assert nt % SLC_CHUNK == 0 and nt >= NSA_WINDOW // TILE + 1
    return pl.pallas_call(
        _nsa_kernel,
        grid=(b, nt),
        in_specs=[col(C_QN),
                  pl.BlockSpec((1, n_cmp, HEAD_DIM), lambda i, j: (i, 0, 0)),
                  pl.BlockSpec((1, HEAD_DIM, n_cmp), lambda i, j: (i, 0, 0)),
                  pl.BlockSpec(gt.shape, lambda i, j: (0, 0)),
                  pl.BlockSpec((1, TILE, TILE), lambda i, j: (i, j, KV_GATES // TILE)),
                  pl.BlockSpec(gate_expand.shape, lambda i, j: (0, 0, 0)),
                  ktile, vtile, ktile, vtile,
                  pl.BlockSpec((1, gw), lambda i, j: (0, 0))],
        out_specs=pl.BlockSpec((1, TILE, gw), lambda i, j: (i, j, 0)),
        out_shape=jax.ShapeDtypeStruct((b, s, gw), F32),
        scratch_shapes=[pltpu.VMEM((SLC_CHUNK * TILE, GROUP_HEADS * TILE), F32),
                        pltpu.VMEM((SLC_CHUNK * TILE, GROUP_HEADS * TILE), F32),
                        pltpu.VMEM((NSA_WINDOW + TILE, GROUP_HEADS * TILE), F32),
                        pltpu.VMEM((n_blk, TILE), F32)],
        compiler_params=_cparams("parallel", "arbitrary"),
        name="nsa",
    )(proj, k_cmp, v_cmp_t, gt, kv, gate_expand, ks, vs, kw, vw, beta)


def _dilated_kernel(q0_ref, q1_ref, k0_ref, k1_ref, v0_ref, v1_ref, beta_ref, o_ref, og_ref, lg_ref):
    gw = GROUP_WIDTH
    cols = GROUP_HEADS * TILE
    p0 = pl.program_id(1) * DIL_SPAN

    def load(refs, first, n, rows):
        window = pl.ds(pl.multiple_of(first, TILE), n)
        return jnp.concatenate([ref.at[0, window][rows, :] for ref in refs], axis=1)

    def store(ref, g, first, n, rows, val):
        window = pl.ds(pl.multiple_of(first, TILE), n)
        for u in range(2):
            ref.at[g, u, window][rows, :] = val[:, TILE * u:TILE * (u + 1)]

    row_head = lax.shift_right_logical(lax.broadcasted_iota(jnp.int32, (gw, TILE), 0), 6)
    rel = (lax.broadcasted_iota(jnp.int32, (2 * TILE, TILE), 1)
           - lax.broadcasted_iota(jnp.int32, (2 * TILE, TILE), 0))

    for g, dil in enumerate(DILATIONS):
        n_t = DIL_SPAN // (TILE * dil)

        def body(i, carry, r, g=g, dil=dil):
            sub0 = p0 // dil + i * TILE
            ksub0 = jnp.maximum(sub0 - TILE, 0)
            rows_q = pl.ds(r, TILE, stride=dil)
            rows_k = pl.ds(r, 2 * TILE, stride=dil)
            keys = functools.partial(load, first=dil * ksub0, n=2 * dil * TILE, rows=rows_k)
            q_t = (load((q0_ref, q1_ref), dil * sub0, dil * TILE, rows_q) * (SCALE * LOG2_E)).T
            wq = jnp.concatenate([jnp.where(row_head == h, q_t, 0.0) for h in range(GROUP_HEADS)],
                                 axis=1).astype(BF16)
            dist = rel + (sub0 - ksub0)
            bias = jnp.where(dist >= 0, jnp.where(dist <= TILE, 0.0, NEG), NEG)
            bias = jnp.concatenate([bias] * GROUP_HEADS, axis=1)
            s = _dot(keys((k0_ref, k1_ref)).astype(BF16), wq) + bias
            m = jnp.max(s, axis=0, keepdims=True)
            p = jnp.exp2(s - m)
            l = jnp.sum(p, axis=0, keepdims=True)
            p = p.astype(BF16)
            v_t = keys((v0_ref, v1_ref)).T.astype(BF16)
            lse = m + jnp.log2(l)
            o_t, l_t = [], []
            for h in range(GROUP_HEADS):
                qs = slice(TILE * h, TILE * (h + 1))
                o_t.append(_dot(v_t[HEAD_DIM * h:HEAD_DIM * (h + 1), :], p[:, qs]) / l[:, qs])
                l_t.append(jnp.broadcast_to(lse[:, qs], (HEAD_DIM, TILE)))
            store(og_ref, g, dil * i * TILE, dil * TILE, rows_q, jnp.concatenate(o_t, axis=0).T)
            store(lg_ref, g, dil * i * TILE, dil * TILE, rows_q, jnp.concatenate(l_t, axis=0).T)
            return carry

        for r in range(dil):
            if n_t == 1:
                body(0, 0, r)
            else:
                lax.fori_loop(0, n_t, functools.partial(body, r=r), 0, unroll=4)

    piece = 256

    def combine(c, carry):
        rows = pl.ds(pl.multiple_of(c * piece, piece), piece)
        both = lambda ref, g: jnp.concatenate([ref[g, 0, rows, :], ref[g, 1, rows, :]], axis=1)
        ls = [both(lg_ref, g) for g in range(len(DILATIONS))]
        mx = functools.reduce(jnp.maximum, ls)
        es = [jnp.exp2(x - mx) for x in ls]
        den = functools.reduce(lambda a, b: a + b, es)
        y = functools.reduce(lambda a, b: a + b, [(e / den) * both(og_ref, g) for g, e in enumerate(es)])
        o_ref[0, rows, :] = y * lax.rsqrt(jnp.mean(y * y, axis=-1, keepdims=True) + RMS_EPS) * beta_ref[...]
        return carry

    lax.fori_loop(0, DIL_SPAN // piece, combine, 0)


def _dilated(proj, beta):
    b, s, _ = proj.shape
    gw = GROUP_WIDTH
    assert s % DIL_SPAN == 0 and s >= 2 * TILE * max(DILATIONS)
    lane_block = lambda n: pl.BlockSpec((1, s, TILE), lambda i, j: (i, 0, C_QD // TILE + n))
    return pl.pallas_call(
        _dilated_kernel,
        grid=(b, s // DIL_SPAN),
        in_specs=[lane_block(n) for n in range(3 * gw // TILE)] + [pl.BlockSpec((1, gw), lambda i, j: (0, 0))],
        out_specs=pl.BlockSpec((1, DIL_SPAN, gw), lambda i, j: (i, j, 0)),
        out_shape=jax.ShapeDtypeStruct((b, s, gw), F32),
        scratch_shapes=[pltpu.VMEM((len(DILATIONS), gw // TILE, DIL_SPAN, TILE), F32)] * 2,
        compiler_params=_cparams("parallel", "arbitrary"),
        name="dilated",
    )(*([proj] * (3 * gw // TILE)), beta)


def _rwkv_prep_kernel(cur_ref, halo_ref, mu_ref, w0_ref, wup_ref, a0_ref, aup_ref, gup_ref, kk_w_ref, ka_ref,
                      rk_ref, ones_ref,
                      kk_o, w_o, b_o, k_o, yr_o, vkr_o, v_o, g_o, c_o):
    gw = GROUP_WIDTH
    tm = cur_ref.shape[1]
    ones = ones_ref[...]
    x = cur_ref[0]
    last = halo_ref[0][7:8, :]
    last = jnp.where(pl.program_id(1) > 0, last, 0.0)
    rid = lax.broadcasted_iota(jnp.int32, (tm, 1), 0)
    xprev = jnp.where(rid == 0, last, pltpu.roll(x, 1, axis=0))
    xs = x + (xprev - x) * mu_ref[...]
    r, k, v = xs[:, 0:gw], xs[:, gw:2 * gw], xs[:, 2 * gw:3 * gw]
    wd, ad, gd = xs[:, 3 * gw:3 * gw + 64], xs[:, 3 * gw + 64:3 * gw + 128], xs[:, 3 * gw + 128:4 * gw]
    z = -(w0_ref[...] + _dot(jnp.tanh(wd).astype(BF16), wup_ref[...]))
    softplus = jnp.maximum(z, 0.0) + jnp.log1p(jnp.exp(-jnp.abs(z)))
    w = jnp.exp(-jnp.exp(-softplus - 0.5))
    a = _sigmoid(a0_ref[...] + _dot(ad.astype(BF16), aup_ref[...]))
    g = _dot(_sigmoid(gd).astype(BF16), gup_ref[...])
    kk = k * kk_w_ref[...]
    kk = kk * lax.rsqrt(_seg_sum(kk * kk, ones) + 1e-12)
    k2 = k * (1.0 + (a - 1.0) * ka_ref[...])
    b = kk * a
    br = _seg_sum(b * r, ones)
    kr = _seg_sum(k2 * r, ones)
    kk_o[0] = kk
    w_o[0] = w
    b_o[0] = b
    k_o[0] = k2
    yr_o[0] = w * r - kk * br
    vkr_o[0] = v * kr
    v_o[0] = v
    g_o[0] = g
    c_o[0] = _seg_sum(r * k2 * rk_ref[...], ones)


def _rwkv_prep(proj, p, ones, tm=256):
    b, s, _ = proj.shape
    gw = GROUP_WIDTH
    vec = lambda n: pl.BlockSpec((1, n), lambda i, j: (0, 0))
    mat = lambda a: pl.BlockSpec(a.shape, lambda i, j: (0, 0))
    out = pl.BlockSpec((1, tm, gw), lambda i, j: (i, j, 0))
    return pl.pallas_call(
        _rwkv_prep_kernel,
        grid=(b, s // tm),
        in_specs=[pl.BlockSpec((1, tm, 4 * gw), lambda i, j: (i, j, C_PC // (4 * gw))),
                  pl.BlockSpec((1, 8, 4 * gw), lambda i, j: (i, jnp.maximum(j * (tm // 8) - 1, 0), C_PC // (4 * gw))),
                  vec(4 * gw), vec(gw), mat(p["w_up"]), vec(gw), mat(p["a_up"]), mat(p["g_up"]),
                  vec(gw), vec(gw), vec(gw), mat(ones)],
        out_specs=[out] * 9,
        out_shape=[jax.ShapeDtypeStruct((b, s, gw), F32)] * 9,
        compiler_params=_cparams("parallel", "arbitrary"),
        name="rwkv_prep",
    )(proj, proj, p["mu"], p["w0"], p["w_up"], p["a0"], p["a_up"], p["g_up"], p["k_k"], p["k_a"], p["r_k"], ones)


def _rwkv_scan_kernel(kk_ref, w_ref, b_ref, k_ref, yr_ref, vkr_ref, v_ref, ones_ref, diag_ref,
                      y_ref, state_ref, yacc_ref):
    nb, tb = kk_ref.shape[0], kk_ref.shape[1]

    @pl.when(pl.program_id(0) == 0)
    def _():
        state_ref[...] = jnp.zeros(state_ref.shape, F32)
        yacc_ref[...] = jnp.zeros(yacc_ref.shape, F32)

    ones = ones_ref[...]
    diag = diag_ref[...].astype(BF16)
    lane = jnp.bitwise_and(lax.broadcasted_iota(jnp.int32, (HEAD_DIM, GROUP_WIDTH), 1), HEAD_DIM - 1)

    def group(c, carry):
        def step(tt, carry2):
            t = c * HEAD_DIM + tt
            here = lane == tt
            row = lambda ref, i: ref[i, pl.ds(t, 1), :]
            st = [state_ref[i] for i in range(nb)]
            sb = [st[i].astype(BF16) for i in range(nb)]
            rb = lambda ref, i: row(ref, i).astype(BF16)
            reduce = lambda parts: _dot(jnp.concatenate(parts, axis=0), ones)
            red_v = reduce([rb(v_ref, i) * diag for i in range(nb)])
            red_s = reduce([sb[i] * rb(kk_ref, i) for i in range(nb)])
            red_y = reduce([sb[i] * rb(yr_ref, i) + rb(vkr_ref, i) * diag for i in range(nb)])
            part = lambda red, n: red[n * HEAD_DIM:(n + 1) * HEAD_DIM, :]
            for i in range(nb):
                sa, vcol, ycol = part(red_s, i), part(red_v, i), part(red_y, i)
                state_ref[i] = st[i] * row(w_ref, i) - sa * row(b_ref, i) + vcol * row(k_ref, i)
                yacc_ref[i] = jnp.where(here, ycol, yacc_ref[i])
            return carry2

        lax.fori_loop(0, HEAD_DIM, step, 0, unroll=SCAN_UNROLL)
        rows = pl.ds(pl.multiple_of(c * HEAD_DIM, HEAD_DIM), HEAD_DIM)
        for i in range(nb):
            full_t = jnp.concatenate([yacc_ref[i], jnp.zeros((HEAD_DIM, GROUP_WIDTH), F32)], axis=0).T
            y_ref[i, rows, :] = jnp.concatenate(
                [full_t[HEAD_DIM * h:HEAD_DIM * (h + 1), 0:HEAD_DIM] for h in range(GROUP_HEADS)], axis=1)
        return carry

    lax.fori_loop(0, tb // HEAD_DIM, group, 0)


def _rwkv_scan(kk, w, b, k, yr, vkr, v, ones, diag, tb=128):
    nb, s, gw = kk.shape
    inp = pl.BlockSpec((nb, tb, gw), lambda j: (0, j, 0))
    const = lambda a: pl.BlockSpec(a.shape, lambda j: (0, 0))
    return pl.pallas_call(
        _rwkv_scan_kernel,
        grid=(s // tb,),
        in_specs=[inp] * 7 + [const(ones), const(diag)],
        out_specs=inp,
        out_shape=jax.ShapeDtypeStruct((nb, s, gw), F32),
        scratch_shapes=[pltpu.VMEM((nb, HEAD_DIM, gw), F32), pltpu.VMEM((nb, HEAD_DIM, gw), F32)],
        compiler_params=_cparams("arbitrary"),
        name="rwkv_scan",
    )(kk, w, b, k, yr, vkr, v, ones, diag)


def _conformer_kernel(cur_ref, halo_ref, dw_ref, dwb_ref, lng_ref, lnb_ref, o_ref, g_ref, sh_ref):
    gw = GROUP_WIDTH
    tm = cur_ref.shape[1]
    hl = halo_ref.shape[1]
    sub = 8
    glu = lambda x: x[:, 0:gw] * _sigmoid(x[:, gw:2 * gw])
    g_ref[0:hl, :] = jnp.where(pl.program_id(1) > 0, glu(halo_ref[0]), 0.0)
    g_ref[hl:hl + tm, :] = glu(cur_ref[0])
    for r in range(1, sub):
        sh_ref[r - 1] = g_ref[pl.ds(r, sh_ref.shape[1]), :]
    acc = jnp.zeros((tm, gw), F32)
    for j in range(CONV_K):
        off = hl - (CONV_K - 1) + j
        r, a = off % sub, off - off % sub
        tap = g_ref[pl.ds(a, tm), :] if r == 0 else sh_ref[r - 1, pl.ds(a, tm), :]
        acc = acc + dw_ref[j:j + 1, :] * tap
    h = acc + dwb_ref[...]
    mu = jnp.mean(h, axis=-1, keepdims=True)
    d = h - mu
    var = jnp.mean(d * d, axis=-1, keepdims=True)
    o_ref[0] = _silu(d * lax.rsqrt(var + LN_EPS) * lng_ref[...] + lnb_ref[...])


def _conformer(proj, dw, dwb, lng, lnb, tm=512, hl=32):
    b, s, _ = proj.shape
    gw = GROUP_WIDTH
    vec = pl.BlockSpec((1, gw), lambda i, j: (0, 0))
    return pl.pallas_call(
        _conformer_kernel,
        grid=(b, s // tm),
        in_specs=[pl.BlockSpec((1, tm, 2 * gw), lambda i, j: (i, j, C_PD // (2 * gw))),
                  pl.BlockSpec((1, hl, 2 * gw), lambda i, j: (i, jnp.maximum(j * (tm // hl) - 1, 0), C_PD // (2 * gw))),
                  pl.BlockSpec((CONV_K, gw), lambda i, j: (0, 0)), vec, vec, vec],
        out_specs=pl.BlockSpec((1, tm, gw), lambda i, j: (i, j, 0)),
        out_shape=jax.ShapeDtypeStruct((b, s, gw), F32),
        scratch_shapes=[pltpu.VMEM((hl + tm, gw), F32), pltpu.VMEM((7, hl + tm - 8, gw), F32)],
        compiler_params=_cparams("parallel", "arbitrary"),
        name="conformer_conv",
    )(proj, proj, dw, dwb, lng, lnb)


def _out_proj_kernel(x_ref, ya_ref, yb_ref, ys_ref, c_ref, v_ref, g_ref, yd_ref, lng_ref, lnb_ref, ones_ref, w_ref,
                     o_ref):
    gw = GROUP_WIDTH
    ones = ones_ref[...]
    y = ys_ref[...] + c_ref[...] * v_ref[...]
    mu = _seg_sum(y, ones) * (1.0 / HEAD_DIM)
    d = y - mu
    var = _seg_sum(d * d, ones) * (1.0 / HEAD_DIM)
    yc = (d * lax.rsqrt(var + RWKV_GN_EPS) * lng_ref[...] + lnb_ref[...]) * g_ref[...]
    acc = x_ref[...]
    for n, part in enumerate((ya_ref[...], yb_ref[...], yc, yd_ref[...])):
        acc = acc + _dot(part.astype(BF16), w_ref[n * gw:(n + 1) * gw, :])
    o_ref[...] = acc


def _out_proj(x2, parts, vecs, ones, w, tm=512):
    m, d = x2.shape
    gw = GROUP_WIDTH
    blk = pl.BlockSpec((tm, gw), lambda i: (i, 0))
    vec = pl.BlockSpec((1, gw), lambda i: (0, 0))
    return pl.pallas_call(
        _out_proj_kernel,
        grid=(m // tm,),
        in_specs=[pl.BlockSpec((tm, d), lambda i: (i, 0))] + [blk] * len(parts) + [vec] * len(vecs)
                 + [pl.BlockSpec(ones.shape, lambda i: (0, 0)), pl.BlockSpec(w.shape, lambda i: (0, 0))],
        out_specs=pl.BlockSpec((tm, d), lambda i: (i, 0)),
        out_shape=jax.ShapeDtypeStruct((m, d), F32),
        compiler_params=_cparams("parallel"),
        name="out_proj",
    )(x2, *parts, *vecs, ones, w)


def _ffn_kernel(x_ref, halo_ref, gain_ref, upg_ref, upv_ref, dwg_ref, dwv_ref, bg_ref, bv_ref, down_ref, fin_ref,
                o_ref, h_ref, acc_ref, ug_ref, uv_ref, *, tiles_per_seq, final_norm):
    j = pl.program_id(1)
    tm = x_ref.shape[0]
    hl = halo_ref.shape[0]

    def norm(x):
        return (x * lax.rsqrt(jnp.mean(x * x, axis=-1, keepdims=True) + RMS_EPS) * gain_ref[...]).astype(BF16)

    @pl.when(j == 0)
    def _():
        first = pl.program_id(0) % tiles_per_seq == 0
        h_ref[0:hl, :] = jnp.where(first, jnp.zeros((hl, x_ref.shape[1]), BF16), norm(halo_ref[...]))
        h_ref[hl:hl + tm, :] = norm(x_ref[...])
        acc_ref[...] = jnp.zeros(acc_ref.shape, F32)

    h = h_ref[...]

    def conv(up_ref, dw_ref, b_ref, u_ref):
        u_ref[...] = _dot(h, up_ref[...])
        out = b_ref[...]
        for tap in range(FFN_CONV_K):
            out = out + dw_ref[tap:tap + 1, :] * u_ref[pl.ds(hl - (FFN_CONV_K - 1) + tap, tm), :]
        return out

    act = _silu(conv(upg_ref, dwg_ref, bg_ref, ug_ref)) * conv(upv_ref, dwv_ref, bv_ref, uv_ref)
    acc_ref[...] += _dot(act.astype(BF16), down_ref[...])

    @pl.when(j == pl.num_programs(1) - 1)
    def _():
        y = x_ref[...] + acc_ref[...]
        if final_norm:
            y = y * lax.rsqrt(jnp.mean(y * y, axis=-1, keepdims=True) + RMS_EPS) * fin_ref[...]
        o_ref[...] = y


def _ffn(x2, gain, up, dw, dwb, down, fin, seq_len, final_norm, tm=512, tn=1408, hl=16):
    m, d = x2.shape
    dff = down.shape[0]
    nj = dff // tn
    kern = functools.partial(_ffn_kernel, tiles_per_seq=seq_len // tm, final_norm=final_norm)
    return pl.pallas_call(
        kern,
        grid=(m // tm, nj),
        in_specs=[pl.BlockSpec((tm, d), lambda i, j: (i, 0)),
                  pl.BlockSpec((hl, d), lambda i, j: (jnp.maximum(i * (tm // hl) - 1, 0), 0)),
                  pl.BlockSpec((1, d), lambda i, j: (0, 0)),
                  pl.BlockSpec((d, tn), lambda i, j: (0, j)),
                  pl.BlockSpec((d, tn), lambda i, j: (0, j + nj)),
                  pl.BlockSpec((FFN_CONV_K, tn), lambda i, j: (0, j)),
                  pl.BlockSpec((FFN_CONV_K, tn), lambda i, j: (0, j + nj)),
                  pl.BlockSpec((1, tn), lambda i, j: (0, j)),
                  pl.BlockSpec((1, tn), lambda i, j: (0, j + nj)),
                  pl.BlockSpec((tn, d), lambda i, j: (j, 0)),
                  pl.BlockSpec((1, d), lambda i, j: (0, 0))],
        out_specs=pl.BlockSpec((tm, d), lambda i, j: (i, 0)),
        out_shape=jax.ShapeDtypeStruct((m, d), F32),
        scratch_shapes=[pltpu.VMEM((hl + tm, d), BF16), pltpu.VMEM((tm, d), F32),
                        pltpu.VMEM((hl + tm, tn), F32), pltpu.VMEM((hl + tm, tn), F32)],
        compiler_params=_cparams("parallel", "arbitrary"),
        name="conv_ffn",
    )(x2, x2, gain, up, up, dw, dw, dwb, dwb, down, fin)


def _block_ones():
    i = np.arange(GROUP_WIDTH) // HEAD_DIM
    return jnp.asarray(i[:, None] == i[None, :], BF16)


def _diag_mask():
    v = np.arange(HEAD_DIM)
    l = np.arange(GROUP_WIDTH) % HEAD_DIM
    return jnp.asarray(v[:, None] == l[None, :], F32)


def _importance_map(n_cmp, n_blk):
    gt = np.zeros((n_blk, n_cmp), np.float32)
    per = SLC_BLOCK // CMP_STRIDE
    for c in range(n_blk * per):
        for i in (c - 1, c):
            if 0 <= i < n_cmp - 1:
                gt[c // per, i] += 1.0
    return jnp.asarray(gt, BF16)


def kernel(x, norm_mix, w_in, cmp_pos, cmp_k_w1, cmp_k_w2, cmp_v_w1, cmp_v_w2, beta_nsa, beta_dil, rwkv_mu, rwkv_w0, rwkv_w_up, rwkv_a0, rwkv_a_up, rwkv_g_up, rwkv_k_k, rwkv_k_a, rwkv_r_k, rwkv_ln_g, rwkv_ln_b, conv_dw, conv_dw_b, conv_ln_g, conv_ln_b, w_out, norm_ffn, ffn_up, ffn_dw, ffn_dw_b, ffn_down, norm_final):
    bsz, seq, d = x.shape
    depth = w_in.shape[0]
    gw = GROUP_WIDTH
    n_cmp = seq // CMP_STRIDE
    n_blk = seq // SLC_BLOCK
    half = CMP_STRIDE * HEAD_DIM

    cols, kv_cols = _proj_columns()
    w_in_p = w_in[:, :, cols].astype(BF16)
    w_kv_p = w_in[:, :, kv_cols].astype(BF16)
    w_out_b = w_out.astype(BF16)
    ffn_up_b = ffn_up.astype(BF16)
    ffn_down_b = ffn_down.astype(BF16)
    ones = _block_ones()
    diag = _diag_mask()
    gt = _importance_map(n_cmp, n_blk)
    row = lambda a: a.reshape(1, -1)

    x2 = x.reshape(bsz * seq, d)
    for l in range(depth):
        proj, kv = _proj_in(x2, row(norm_mix[l]), w_in_p[l], w_kv_p[l], tn=N_PROJ // 2)
        proj, kv = proj.reshape(bsz, seq, N_PROJ), kv.reshape(bsz, seq, N_KV)

        kh = kv[:, :, KV_KC:KV_KC + HEAD_DIM].reshape(bsz, n_cmp, half)
        vh = kv[:, :, KV_VC:KV_VC + HEAD_DIM].reshape(bsz, n_cmp, half)
        pos = cmp_pos[l].reshape(1, 2 * half)
        k_cmp, v_cmp = _compress(kh, vh, pos[:, :half], pos[:, half:],
                                 cmp_k_w1[l].astype(BF16), cmp_k_w2[l].astype(BF16),
                                 cmp_v_w1[l].astype(BF16), cmp_v_w2[l].astype(BF16))
        y_a = _nsa(proj, kv, k_cmp, jnp.swapaxes(v_cmp, 1, 2), gt, row(beta_nsa[l]))

        y_b = _dilated(proj, row(beta_dil[l]))

        rp = dict(mu=row(rwkv_mu[l]), w0=row(rwkv_w0[l]), w_up=rwkv_w_up[l].astype(BF16), a0=row(rwkv_a0[l]),
                  a_up=rwkv_a_up[l].astype(BF16), g_up=rwkv_g_up[l].astype(BF16), k_k=row(rwkv_k_k[l]),
                  k_a=row(rwkv_k_a[l]), r_k=row(rwkv_r_k[l]))
        kk, w, b, k2, yr, vkr, v, g, c = _rwkv_prep(proj, rp, ones)
        y_scan = _rwkv_scan(kk, w, b, k2, yr, vkr, v, ones, diag)

        y_d = _conformer(proj, conv_dw[l], row(conv_dw_b[l]), row(conv_ln_g[l]), row(conv_ln_b[l]))

        flat = lambda a: a.reshape(bsz * seq, gw)
        parts = [flat(a) for a in (y_a, y_b, y_scan, c, v, g, y_d)]
        vecs = [row(rwkv_ln_g[l]), row(rwkv_ln_b[l])]
        x2 = _out_proj(x2, parts, vecs, ones, w_out_b[l])

        x2 = _ffn(x2, row(norm_ffn[l]), ffn_up_b[l], ffn_dw[l], row(ffn_dw_b[l]), ffn_down_b[l],
                  row(norm_final), seq, final_norm=(l == depth - 1))
    return x2.reshape(bsz, seq, d)
```

```python
import functools

import jax
import jax.numpy as jnp
import numpy as np
from jax import lax
from jax.experimental import pallas as pl
from jax.experimental.pallas import tpu as pltpu

F32 = jnp.float32
BF16 = jnp.bfloat16

HEAD_DIM = 64
GROUP_HEADS = 4
GROUP_WIDTH = HEAD_DIM * GROUP_HEADS
CMP_STRIDE = 16
CMP_BLOCK = 32
SLC_BLOCK = 64
N_SEL_BLOCKS = 16
NSA_WINDOW = 512
DILATIONS = (1, 4, 16)
TILE = 128
CONV_K = 31
FFN_CONV_K = 3
RMS_EPS = 1e-6
LN_EPS = 1e-5
RWKV_GN_EPS = 64e-5
NEG = -1e30
FORCE_SCORE = 1e9
SCALE = HEAD_DIM ** -0.5
LOG2_E = 1.4426950408889634
SCAN_UNROLL = 8
SLC_CHUNK = 4
V_ROWS = HEAD_DIM + 16

C_PC, C_PD, C_QD, C_QN = 0, 1024, 1536, 2304
N_PROJ = 2560
KV_KC, KV_VC, KV_KS, KV_VS, KV_KW, KV_VW, KV_GATES = 0, 64, 128, 192, 256, 320, 384
N_KV = 512
DIL_SPAN = 2048

VMEM_LIMIT = 48 * 1024 * 1024


def _cparams(*sem):
    return pltpu.CompilerParams(dimension_semantics=sem, vmem_limit_bytes=VMEM_LIMIT)


def _proj_columns():
    gw, hd = GROUP_WIDTH, HEAD_DIM
    a0 = 0
    q = np.arange(a0, a0 + gw)
    kc, vc, ks, vs, kw, vw = [np.arange(a0 + gw + i * hd, a0 + gw + (i + 1) * hd) for i in range(6)]
    gate0 = a0 + gw + 6 * hd
    gates = np.arange(gate0, gate0 + 3 * GROUP_HEADS)
    b0 = gate0 + 3 * GROUP_HEADS
    qd, kd, vd = [np.arange(b0 + i * gw, b0 + (i + 1) * gw) for i in range(3)]
    c0 = b0 + 3 * gw
    pc = np.arange(c0, c0 + 4 * gw)
    d0 = c0 + 4 * gw
    pd = np.arange(d0, d0 + 2 * gw)
    cols = np.concatenate([pc, pd, qd, kd, vd, q])
    kv_cols = np.concatenate([kc, vc, ks, vs, kw, vw, gates, np.zeros(N_KV - KV_GATES - gates.shape[0], np.int64)])
    assert cols.shape[0] == N_PROJ and kv_cols.shape[0] == N_KV
    return cols, kv_cols


def _dot(a, b):
    return jnp.dot(a, b, preferred_element_type=F32)


def _seg_sum(x, ones_bf16):
    hi = x.astype(BF16)
    lo = (x - hi.astype(F32)).astype(BF16)
    return _dot(hi, ones_bf16) + _dot(lo, ones_bf16)


def _sigmoid(x):
    return 1.0 / (1.0 + jnp.exp(-x))


def _silu(x):
    return x * _sigmoid(x)


def _proj_in_kernel(x_ref, g_ref, w_ref, wkv_ref, o_ref, okv_ref, h_ref):
    @pl.when(pl.program_id(1) == 0)
    def _():
        x = x_ref[...]
        ms = jnp.mean(x * x, axis=-1, keepdims=True)
        h_ref[...] = (x * lax.rsqrt(ms + RMS_EPS) * g_ref[...]).astype(BF16)
        okv_ref[...] = _dot(h_ref[...], wkv_ref[...])

    o_ref[...] = _dot(h_ref[...], w_ref[...])


def _proj_in(x2, gain, w, w_kv, tn, tm=1024):
    m, d = x2.shape
    n, n_kv = w.shape[1], w_kv.shape[1]
    assert n % tn == 0
    return pl.pallas_call(
        _proj_in_kernel,
        grid=(m // tm, n // tn),
        in_specs=[pl.BlockSpec((tm, d), lambda i, j: (i, 0)),
                  pl.BlockSpec((1, d), lambda i, j: (0, 0)),
                  pl.BlockSpec((d, tn), lambda i, j: (0, j)),
                  pl.BlockSpec((d, n_kv), lambda i, j: (0, 0))],
        out_specs=[pl.BlockSpec((tm, tn), lambda i, j: (i, j)),
                   pl.BlockSpec((tm, n_kv), lambda i, j: (i, 0))],
        out_shape=[jax.ShapeDtypeStruct((m, n), F32), jax.ShapeDtypeStruct((m, n_kv), F32)],
        scratch_shapes=[pltpu.VMEM((tm, d), BF16)],
        compiler_params=_cparams("parallel", "arbitrary"),
        name="proj_in",
    )(x2, gain, w, w_kv)


def _compress_kernel(kh_ref, vh_ref, plo_ref, phi_ref, kw1_ref, kw2_ref, vw1_ref, vw2_ref, ko_ref, vo_ref):
    half = kh_ref.shape[2]
    n = kh_ref.shape[1]

    def mlp(xh, w1_ref, w2_ref):
        a = _dot((xh + plo_ref[...]).astype(BF16), w1_ref[0:half, :])
        bm = _dot((xh + phi_ref[...]).astype(BF16), w1_ref[half:2 * half, :])
        hid = a + pltpu.roll(bm, n - 1, axis=0)
        return _dot(_silu(hid).astype(BF16), w2_ref[...])

    ko_ref[0] = mlp(kh_ref[0], kw1_ref, kw2_ref)
    vo_ref[0] = mlp(vh_ref[0], vw1_ref, vw2_ref)


def _compress(kh, vh, pos_lo, pos_hi, kw1, kw2, vw1, vw2):
    b, n, half = kh.shape
    hid = kw1.shape[1]
    full = lambda shape: pl.BlockSpec(shape, lambda i: (0,) * len(shape))
    return pl.pallas_call(
        _compress_kernel,
        grid=(b,),
        in_specs=[pl.BlockSpec((1, n, half), lambda i: (i, 0, 0)),
                  pl.BlockSpec((1, n, half), lambda i: (i, 0, 0)),
                  full((1, half)), full((1, half)),
                  full((2 * half, hid)), full((hid, HEAD_DIM)),
                  full((2 * half, hid)), full((hid, HEAD_DIM))],
        out_specs=[pl.BlockSpec((1, n, HEAD_DIM), lambda i: (i, 0, 0))] * 2,
        out_shape=[jax.ShapeDtypeStruct((b, n, HEAD_DIM), F32)] * 2,
        compiler_params=_cparams("parallel"),
        name="nsa_compress",
    )(kh, vh, pos_lo, pos_hi, kw1, kw2, vw1, vw2)


def _cmp_select(qs_t, t0, kc_ref, vct_ref, gt_ref, score_ref):
    n_cmp = kc_ref.shape[1]
    n_blk = gt_ref.shape[0]
    heads = range(GROUP_HEADS)
    kc = kc_ref[0].astype(BF16)
    vc_t = vct_ref[0].astype(BF16)
    ci = lax.broadcasted_iota(jnp.int32, (n_cmp, TILE), 0)
    tq = t0 + lax.broadcasted_iota(jnp.int32, (n_cmp, TILE), 1)
    bias = jnp.where(jnp.where(ci < n_cmp - 1, CMP_STRIDE * ci + CMP_BLOCK - 1, 1 << 30) <= tq, 0.0, NEG)
    bias = jnp.concatenate([bias] * GROUP_HEADS, axis=1)
    s = _dot(kc, qs_t) + bias
    e = jnp.exp2(s - jnp.max(s, axis=0, keepdims=True))
    p = jnp.where(bias == 0.0, e / jnp.sum(e, axis=0, keepdims=True), 0.0)
    imp_t = functools.reduce(lambda a, b: a + b, [p[:, TILE * h:TILE * (h + 1)] for h in heads])
    o_t = _dot(vc_t, p.astype(BF16))

    hi = imp_t.astype(BF16)
    lo = (imp_t - hi.astype(F32)).astype(BF16)
    blk = _dot(gt_ref[...], hi) + _dot(gt_ref[...], lo)
    j = lax.broadcasted_iota(jnp.int32, (n_blk, TILE), 0)
    t = t0 + lax.broadcasted_iota(jnp.int32, (n_blk, TILE), 1)
    cur = lax.shift_right_logical(t, 6)
    forced = jnp.where(j == 0, 1, jnp.where(j == cur, 1, jnp.where(j == cur - 1, 1, 0)))
    score = jnp.where(j * SLC_BLOCK <= t, jnp.where(forced == 1, FORCE_SCORE, blk), NEG)
    score_ref[...] = score

    def count(pair, rank):
        for jp in (2 * pair, 2 * pair + 1):
            row = score_ref[pl.ds(jp, 1), :]
            rank = rank + jnp.where(row > score, 1, jnp.where(row == score, jnp.where(j > jp, 1, 0), 0))
        return rank

    n_valid = jnp.minimum((t0 + TILE) // SLC_BLOCK, n_blk)
    rank = lax.fori_loop(0, n_valid // 2, count, jnp.zeros((n_blk, TILE), jnp.int32))
    return o_t, jnp.where(rank < N_SEL_BLOCKS, 0.0, NEG)


def _nsa_kernel(q_ref, kc_ref, vct_ref, gt_ref, gl_ref, ge_ref, ks_ref, vs_ref, kw_ref, vw_ref, beta_ref,
                o_ref, sa_ref, sb_ref, sw_ref, score_ref):
    qi = pl.program_id(1)
    cols = GROUP_HEADS * TILE
    heads = range(GROUP_HEADS)
    q_t = (q_ref[0] * (SCALE * LOG2_E)).T
    qs_t = jnp.concatenate([q_t[HEAD_DIM * h:HEAD_DIM * (h + 1), :] for h in heads], axis=1).astype(BF16)
    q_pos = qi * TILE + lax.broadcasted_iota(jnp.int32, (1, TILE), 1)
    from_t = lambda x_t: jnp.concatenate([x_t[:, TILE * h:TILE * (h + 1)] for h in heads], axis=0).T

    def keys(ref, start, n):
        return ref[0, pl.ds(start, n)].reshape(n * TILE, 2 * HEAD_DIM)

    def values_t(ref, start, n):
        v = ref[0, pl.ds(start, n)]
        return jnp.concatenate([v[u] for u in range(n)], axis=1)

    def band_bias(start, n, window):
        key_pos = start * TILE + lax.broadcasted_iota(jnp.int32, (n * TILE, TILE), 0)
        dist = q_pos - key_pos
        ok = jnp.where(dist >= 0, 1, 0) if window is None else jnp.where(dist >= 0, jnp.where(dist <= window, 1, 0), 0)
        return jnp.concatenate([jnp.where(ok == 1, 0.0, NEG)] * GROUP_HEADS, axis=1)

    def col_max(s):
        return jnp.max(s, axis=0, keepdims=True)

    def finish(acc):
        return from_t(acc[0:HEAD_DIM, :] / acc[HEAD_DIM:HEAD_DIM + 1, :])

    n_win = NSA_WINDOW // TILE + 1
    first = jnp.maximum(qi - (n_win - 1), 0)
    w_win = jnp.concatenate([qs_t, jnp.zeros((HEAD_DIM, cols), BF16)], axis=0)
    sw_ref[...] = _dot(keys(kw_ref, first, n_win), w_win) + band_bias(first, n_win, NSA_WINDOW)
    p = jnp.exp2(sw_ref[...] - col_max(sw_ref[...])).astype(BF16)
    o_win = finish(_dot(values_t(vw_ref, first, n_win), p))

    ocmp_t, sel = _cmp_select(qs_t, qi * TILE, kc_ref, vct_ref, gt_ref, score_ref)
    w = jnp.concatenate([qs_t, jnp.concatenate([sel] * GROUP_HEADS, axis=1).astype(BF16)], axis=0)

    def prefetch(ref, c, may_hold_diagonal=True):
        s = _dot(keys(ks_ref, c * SLC_CHUNK, SLC_CHUNK), w)
        if may_hold_diagonal:
            s = s + band_bias(c * SLC_CHUNK, SLC_CHUNK, None)
        ref[...] = s
        return col_max(s)

    def consume(ref, c, m, cm, acc):
        m_new = jnp.maximum(m, cm)
        p = jnp.exp2(ref[...] - m_new).astype(BF16)
        return m_new, jnp.exp2(m - m_new) * acc + _dot(values_t(vs_ref, c * SLC_CHUNK, SLC_CHUNK), p)

    def two_chunks(j, carry):
        m, cm_a, acc = carry
        cm_b = prefetch(sb_ref, 2 * j + 1, may_hold_diagonal=False)
        m, acc = consume(sa_ref, 2 * j, m, cm_a, acc)
        cm_a = prefetch(sa_ref, 2 * j + 2)
        m, acc = consume(sb_ref, 2 * j + 1, m, cm_b, acc)
        return m, cm_a, acc

    n_full = qi // SLC_CHUNK
    n_pairs = n_full // 2
    m, cm_a, acc = lax.fori_loop(0, n_pairs, two_chunks,
                                 (jnp.full((1, cols), NEG, F32), prefetch(sa_ref, 0), jnp.zeros((V_ROWS, cols), F32)))
    odd = 2 * n_pairs + 1

    def last_two():
        cm_b = prefetch(sb_ref, odd)
        m1, acc1 = consume(sa_ref, 2 * n_pairs, m, cm_a, acc)
        return consume(sb_ref, odd, m1, cm_b, acc1)[1]

    def last_one():
        return consume(sa_ref, 2 * n_pairs, m, cm_a, acc)[1]

    o_slc = finish(lax.cond(odd <= n_full, last_two, last_one))

    sg = _sigmoid(gl_ref[0])
    sg_hi = sg.astype(BF16)
    sg_lo = (sg - sg_hi.astype(F32)).astype(BF16)
    gate = lambda j: _dot(sg_hi, ge_ref[j]) + _dot(sg_lo, ge_ref[j])
    y = gate(0) * from_t(ocmp_t) + gate(1) * o_slc + gate(2) * o_win
    ms = jnp.mean(y * y, axis=-1, keepdims=True)
    o_ref[0] = y * lax.rsqrt(ms + RMS_EPS) * beta_ref[...]


def _key_tiles(k, onehot):
    b, s, _ = k.shape
    aug = jnp.zeros((b, s, HEAD_DIM), BF16) if onehot is None else jnp.broadcast_to(onehot, (b, s, HEAD_DIM))
    return jnp.concatenate([k.astype(BF16), aug], axis=-1).reshape(b, s // TILE, TILE, 2 * HEAD_DIM)


def _value_tiles_t(v):
    b, s, _ = v.shape
    v_t = v.astype(BF16).reshape(b, s // TILE, TILE, HEAD_DIM).transpose(0, 1, 3, 2)
    pad = jnp.zeros((b, s // TILE, V_ROWS - HEAD_DIM, TILE), BF16).at[:, :, 0, :].set(1.0)
    return jnp.concatenate([v_t, pad], axis=2)


def _nsa(proj, kv, k_cmp, v_cmp_t, gt, beta):
    b, s, _ = proj.shape
    gw = GROUP_WIDTH
    n_cmp = k_cmp.shape[1]
    n_blk = gt.shape[0]
    assert n_blk == HEAD_DIM, "the selection rows share the 128-deep contraction with the 64 head dims"
    piece = lambda c: kv[:, :, c:c + HEAD_DIM]
    blk_of_key = np.arange(s)[:, None] // SLC_BLOCK == np.arange(n_blk)[None, :]
    ks = _key_tiles(piece(KV_KS), jnp.asarray(blk_of_key, BF16))
    kw = _key_tiles(piece(KV_KW), None)
    vs = _value_tiles_t(piece(KV_VS))
    vw = _value_tiles_t(piece(KV_VW))
    ge = np.zeros((3, TILE, gw), np.float32)
    for j in range(3):
        for h in range(GROUP_HEADS):
            ge[j, 3 * h + j, HEAD_DIM * h:HEAD_DIM * (h + 1)] = 1.0
    gate_expand = jnp.asarray(ge, BF16)
    nt = s // TILE
    col = lambda c: pl.BlockSpec((1, TILE, gw), lambda i, j: (i, j, c // gw))
    ktile = pl.BlockSpec((1, nt, TILE, 2 * HEAD_DIM), lambda i, j: (i, 0, 0, 0))
    vtile = pl.BlockSpec((1, nt, V_ROWS, TILE), lambda i, j: (i, 0, 0, 0))
    assert nt % SLC_CHUNK == 0 and nt >= NSA_WINDOW // TILE + 1
    return pl.pallas_call(
        _nsa_kernel,
        grid=(b, nt),
        in_specs=[col(C_QN),
                  pl.BlockSpec((1, n_cmp, HEAD_DIM), lambda i, j: (i, 0, 0)),
                  pl.BlockSpec((1, HEAD_DIM, n_cmp), lambda i, j: (i, 0, 0)),
                  pl.BlockSpec(gt.shape, lambda i, j: (0, 0)),
                  pl.BlockSpec((1, TILE, TILE), lambda i, j: (i, j, KV_GATES // TILE)),
                  pl.BlockSpec(gate_expand.shape, lambda i, j: (0, 0, 0)),
                  ktile, vtile, ktile, vtile,
                  pl.BlockSpec((1, gw), lambda i, j: (0, 0))],
        out_specs=pl.BlockSpec((1, TILE, gw), lambda i, j: (i, j, 0)),
        out_shape=jax.ShapeDtypeStruct((b, s, gw), F32),
        scratch_shapes=[pltpu.VMEM((SLC_CHUNK * TILE, GROUP_HEADS * TILE), F32),
                        pltpu.VMEM((SLC_CHUNK * TILE, GROUP_HEADS * TILE), F32),
                        pltpu.VMEM((NSA_WINDOW + TILE, GROUP_HEADS * TILE), F32),
                        pltpu.VMEM((n_blk, TILE), F32)],
        compiler_params=_cparams("parallel", "arbitrary"),
        name="nsa",
    )(proj, k_cmp, v_cmp_t, gt, kv, gate_expand, ks, vs, kw, vw, beta)


def _dilated_kernel(q0_ref, q1_ref, k0_ref, k1_ref, v0_ref, v1_ref, beta_ref, o_ref, og_ref, lg_ref):
    gw = GROUP_WIDTH
    cols = GROUP_HEADS * TILE
    p0 = pl.program_id(1) * DIL_SPAN

    def load(refs, first, n, rows):
        window = pl.ds(pl.multiple_of(first, TILE), n)
        return jnp.concatenate([ref.at[0, window][rows, :] for ref in refs], axis=1)

    def store(ref, g, first, n, rows, val):
        window = pl.ds(pl.multiple_of(first, TILE), n)
        for u in range(2):
            ref.at[g, u, window][rows, :] = val[:, TILE * u:TILE * (u + 1)]

    row_head = lax.shift_right_logical(lax.broadcasted_iota(jnp.int32, (gw, TILE), 0), 6)
    rel = (lax.broadcasted_iota(jnp.int32, (2 * TILE, TILE), 1)
           - lax.broadcasted_iota(jnp.int32, (2 * TILE, TILE), 0))

    for g, dil in enumerate(DILATIONS):
        n_t = DIL_SPAN // (TILE * dil)

        def body(i, carry, r, g=g, dil=dil):
            sub0 = p0 // dil + i * TILE
            ksub0 = jnp.maximum(sub0 - TILE, 0)
            rows_q = pl.ds(r, TILE, stride=dil)
            rows_k = pl.ds(r, 2 * TILE, stride=dil)
            keys = functools.partial(load, first=dil * ksub0, n=2 * dil * TILE, rows=rows_k)
            q_t = (load((q0_ref, q1_ref), dil * sub0, dil * TILE, rows_q) * (SCALE * LOG2_E)).T
            wq = jnp.concatenate([jnp.where(row_head == h, q_t, 0.0) for h in range(GROUP_HEADS)],
                                 axis=1).astype(BF16)
            dist = rel + (sub0 - ksub0)
            bias = jnp.where(dist >= 0, jnp.where(dist <= TILE, 0.0, NEG), NEG)
            bias = jnp.concatenate([bias] * GROUP_HEADS, axis=1)
            s = _dot(keys((k0_ref, k1_ref)).astype(BF16), wq) + bias
            m = jnp.max(s, axis=0, keepdims=True)
            p = jnp.exp2(s - m)
            l = jnp.sum(p, axis=0, keepdims=True)
            p = p.astype(BF16)
            v_t = keys((v0_ref, v1_ref)).T.astype(BF16)
            lse = m + jnp.log2(l)
            o_t, l_t = [], []
            for h in range(GROUP_HEADS):
                qs = slice(TILE * h, TILE * (h + 1))
                o_t.append(_dot(v_t[HEAD_DIM * h:HEAD_DIM * (h + 1), :], p[:, qs]) / l[:, qs])
                l_t.append(jnp.broadcast_to(lse[:, qs], (HEAD_DIM, TILE)))
            store(og_ref, g, dil * i * TILE, dil * TILE, rows_q, jnp.concatenate(o_t, axis=0).T)
            store(lg_ref, g, dil * i * TILE, dil * TILE, rows_q, jnp.concatenate(l_t, axis=0).T)
            return carry

        for r in range(dil):
            if n_t == 1:
                body(0, 0, r)
            else:
                lax.fori_loop(0, n_t, functools.partial(body, r=r), 0, unroll=4)

    piece = 256

    def combine(c, carry):
        rows = pl.ds(pl.multiple_of(c * piece, piece), piece)
        both = lambda ref, g: jnp.concatenate([ref[g, 0, rows, :], ref[g, 1, rows, :]], axis=1)
        ls = [both(lg_ref, g) for g in range(len(DILATIONS))]
        mx = functools.reduce(jnp.maximum, ls)
        es = [jnp.exp2(x - mx) for x in ls]
        den = functools.reduce(lambda a, b: a + b, es)
        y = functools.reduce(lambda a, b: a + b, [(e / den) * both(og_ref, g) for g, e in enumerate(es)])
        o_ref[0, rows, :] = y * lax.rsqrt(jnp.mean(y * y, axis=-1, keepdims=True) + RMS_EPS) * beta_ref[...]
        return carry

    lax.fori_loop(0, DIL_SPAN // piece, combine, 0)


def _dilated(proj, beta):
    b, s, _ = proj.shape
    gw = GROUP_WIDTH
    assert s % DIL_SPAN == 0 and s >= 2 * TILE * max(DILATIONS)
    lane_block = lambda n: pl.BlockSpec((1, s, TILE), lambda i, j: (i, 0, C_QD // TILE + n))
    return pl.pallas_call(
        _dilated_kernel,
        grid=(b, s // DIL_SPAN),
        in_specs=[lane_block(n) for n in range(3 * gw // TILE)] + [pl.BlockSpec((1, gw), lambda i, j: (0, 0))],
        out_specs=pl.BlockSpec((1, DIL_SPAN, gw), lambda i, j: (i, j, 0)),
        out_shape=jax.ShapeDtypeStruct((b, s, gw), F32),
        scratch_shapes=[pltpu.VMEM((len(DILATIONS), gw // TILE, DIL_SPAN, TILE), F32)] * 2,
        compiler_params=_cparams("parallel", "arbitrary"),
        name="dilated",
    )(*([proj] * (3 * gw // TILE)), beta)


def _rwkv_prep_kernel(cur_ref, halo_ref, mu_ref, w0_ref, wup_ref, a0_ref, aup_ref, gup_ref, kk_w_ref, ka_ref,
                      rk_ref, ones_ref,
                      kk_o, w_o, b_o, k_o, yr_o, vkr_o, v_o, g_o, c_o):
    gw = GROUP_WIDTH
    tm = cur_ref.shape[1]
    ones = ones_ref[...]
    x = cur_ref[0]
    last = halo_ref[0][7:8, :]
    last = jnp.where(pl.program_id(1) > 0, last, 0.0)
    rid = lax.broadcasted_iota(jnp.int32, (tm, 1), 0)
    xprev = jnp.where(rid == 0, last, pltpu.roll(x, 1, axis=0))
    xs = x + (xprev - x) * mu_ref[...]
    r, k, v = xs[:, 0:gw], xs[:, gw:2 * gw], xs[:, 2 * gw:3 * gw]
    wd, ad, gd = xs[:, 3 * gw:3 * gw + 64], xs[:, 3 * gw + 64:3 * gw + 128], xs[:, 3 * gw + 128:4 * gw]
    z = -(w0_ref[...] + _dot(jnp.tanh(wd).astype(BF16), wup_ref[...]))
    softplus = jnp.maximum(z, 0.0) + jnp.log1p(jnp.exp(-jnp.abs(z)))
    w = jnp.exp(-jnp.exp(-softplus - 0.5))
    a = _sigmoid(a0_ref[...] + _dot(ad.astype(BF16), aup_ref[...]))
    g = _dot(_sigmoid(gd).astype(BF16), gup_ref[...])
    kk = k * kk_w_ref[...]
    kk = kk * lax.rsqrt(_seg_sum(kk * kk, ones) + 1e-12)
    k2 = k * (1.0 + (a - 1.0) * ka_ref[...])
    b = kk * a
    br = _seg_sum(b * r, ones)
    kr = _seg_sum(k2 * r, ones)
    kk_o[0] = kk
    w_o[0] = w
    b_o[0] = b
    k_o[0] = k2
    yr_o[0] = w * r - kk * br
    vkr_o[0] = v * kr
    v_o[0] = v
    g_o[0] = g
    c_o[0] = _seg_sum(r * k2 * rk_ref[...], ones)


def _rwkv_prep(proj, p, ones, tm=512):
    b, s, _ = proj.shape
    gw = GROUP_WIDTH
    vec = lambda n: pl.BlockSpec((1, n), lambda i, j: (0, 0))
    mat = lambda a: pl.BlockSpec(a.shape, lambda i, j: (0, 0))
    out = pl.BlockSpec((1, tm, gw), lambda i, j: (i, j, 0))
    return pl.pallas_call(
        _rwkv_prep_kernel,
        grid=(b, s // tm),
        in_specs=[pl.BlockSpec((1, tm, 4 * gw), lambda i, j: (i, j, C_PC // (4 * gw))),
                  pl.BlockSpec((1, 8, 4 * gw), lambda i, j: (i, jnp.maximum(j * (tm // 8) - 1, 0), C_PC // (4 * gw))),
                  vec(4 * gw), vec(gw), mat(p["w_up"]), vec(gw), mat(p["a_up"]), mat(p["g_up"]),
                  vec(gw), vec(gw), vec(gw), mat(ones)],
        out_specs=[out] * 9,
        out_shape=[jax.ShapeDtypeStruct((b, s, gw), F32)] * 9,
        compiler_params=_cparams("parallel", "arbitrary"),
        name="rwkv_prep",
    )(proj, proj, p["mu"], p["w0"], p["w_up"], p["a0"], p["a_up"], p["g_up"], p["k_k"], p["k_a"], p["r_k"], ones)


def _rwkv_scan_kernel(kk_ref, w_ref, b_ref, k_ref, yr_ref, vkr_ref, v_ref, ones_ref, diag_ref,
                      y_ref, state_ref, yacc_ref):
    nb, tb = kk_ref.shape[0], kk_ref.shape[1]

    @pl.when(pl.program_id(0) == 0)
    def _():
        state_ref[...] = jnp.zeros(state_ref.shape, F32)
        yacc_ref[...] = jnp.zeros(yacc_ref.shape, F32)

    ones = ones_ref[...]
    diag = diag_ref[...].astype(BF16)
    lane = jnp.bitwise_and(lax.broadcasted_iota(jnp.int32, (HEAD_DIM, GROUP_WIDTH), 1), HEAD_DIM - 1)

    def group(c, carry):
        def step(tt, carry2):
            t = c * HEAD_DIM + tt
            here = lane == tt
            row = lambda ref, i: ref[i, pl.ds(t, 1), :]
            st = [state_ref[i] for i in range(nb)]
            sb = [st[i].astype(BF16) for i in range(nb)]
            rb = lambda ref, i: row(ref, i).astype(BF16)
            reduce = lambda parts: _dot(jnp.concatenate(parts, axis=0), ones)
            red_v = reduce([rb(v_ref, i) * diag for i in range(nb)])
            red_s = reduce([sb[i] * rb(kk_ref, i) for i in range(nb)])
            red_y = reduce([sb[i] * rb(yr_ref, i) + rb(vkr_ref, i) * diag for i in range(nb)])
            part = lambda red, n: red[n * HEAD_DIM:(n + 1) * HEAD_DIM, :]
            for i in range(nb):
                sa, vcol, ycol = part(red_s, i), part(red_v, i), part(red_y, i)
                state_ref[i] = st[i] * row(w_ref, i) - sa * row(b_ref, i) + vcol * row(k_ref, i)
                yacc_ref[i] = jnp.where(here, ycol, yacc_ref[i])
            return carry2

        lax.fori_loop(0, HEAD_DIM, step, 0, unroll=SCAN_UNROLL)
        rows = pl.ds(pl.multiple_of(c * HEAD_DIM, HEAD_DIM), HEAD_DIM)
        for i in range(nb):
            full_t = jnp.concatenate([yacc_ref[i], jnp.zeros((HEAD_DIM, GROUP_WIDTH), F32)], axis=0).T
            y_ref[i, rows, :] = jnp.concatenate(
                [full_t[HEAD_DIM * h:HEAD_DIM * (h + 1), 0:HEAD_DIM] for h in range(GROUP_HEADS)], axis=1)
        return carry

    lax.fori_loop(0, tb // HEAD_DIM, group, 0)


def _rwkv_scan(kk, w, b, k, yr, vkr, v, ones, diag, tb=128):
    nb, s, gw = kk.shape
    inp = pl.BlockSpec((nb, tb, gw), lambda j: (0, j, 0))
    const = lambda a: pl.BlockSpec(a.shape, lambda j: (0, 0))
    return pl.pallas_call(
        _rwkv_scan_kernel,
        grid=(s // tb,),
        in_specs=[inp] * 7 + [const(ones), const(diag)],
        out_specs=inp,
        out_shape=jax.ShapeDtypeStruct((nb, s, gw), F32),
        scratch_shapes=[pltpu.VMEM((nb, HEAD_DIM, gw), F32), pltpu.VMEM((nb, HEAD_DIM, gw), F32)],
        compiler_params=_cparams("arbitrary"),
        name="rwkv_scan",
    )(kk, w, b, k, yr, vkr, v, ones, diag)


def _conformer_kernel(cur_ref, halo_ref, dw_ref, dwb_ref, lng_ref, lnb_ref, o_ref, g_ref, sh_ref):
    gw = GROUP_WIDTH
    tm = cur_ref.shape[1]
    hl = halo_ref.shape[1]
    sub = 8
    glu = lambda x: x[:, 0:gw] * _sigmoid(x[:, gw:2 * gw])
    g_ref[0:hl, :] = jnp.where(pl.program_id(1) > 0, glu(halo_ref[0]), 0.0)
    g_ref[hl:hl + tm, :] = glu(cur_ref[0])
    for r in range(1, sub):
        sh_ref[r - 1] = g_ref[pl.ds(r, sh_ref.shape[1]), :]
    acc = jnp.zeros((tm, gw), F32)
    for j in range(CONV_K):
        off = hl - (CONV_K - 1) + j
        r, a = off % sub, off - off % sub
        tap = g_ref[pl.ds(a, tm), :] if r == 0 else sh_ref[r - 1, pl.ds(a, tm), :]
        acc = acc + dw_ref[j:j + 1, :] * tap
    h = acc + dwb_ref[...]
    mu = jnp.mean(h, axis=-1, keepdims=True)
    d = h - mu
    var = jnp.mean(d * d, axis=-1, keepdims=True)
    o_ref[0] = _silu(d * lax.rsqrt(var + LN_EPS) * lng_ref[...] + lnb_ref[...])


def _conformer(proj, dw, dwb, lng, lnb, tm=1024, hl=32):
    b, s, _ = proj.shape
    gw = GROUP_WIDTH
    vec = pl.BlockSpec((1, gw), lambda i, j: (0, 0))
    return pl.pallas_call(
        _conformer_kernel,
        grid=(b, s // tm),
        in_specs=[pl.BlockSpec((1, tm, 2 * gw), lambda i, j: (i, j, C_PD // (2 * gw))),
                  pl.BlockSpec((1, hl, 2 * gw), lambda i, j: (i, jnp.maximum(j * (tm // hl) - 1, 0), C_PD // (2 * gw))),
                  pl.BlockSpec((CONV_K, gw), lambda i, j: (0, 0)), vec, vec, vec],
        out_specs=pl.BlockSpec((1, tm, gw), lambda i, j: (i, j, 0)),
        out_shape=jax.ShapeDtypeStruct((b, s, gw), F32),
        scratch_shapes=[pltpu.VMEM((hl + tm, gw), F32), pltpu.VMEM((7, hl + tm - 8, gw), F32)],
        compiler_params=_cparams("parallel", "arbitrary"),
        name="conformer_conv",
    )(proj, proj, dw, dwb, lng, lnb)


def _out_proj_kernel(x_ref, ya_ref, yb_ref, ys_ref, c_ref, v_ref, g_ref, yd_ref, lng_ref, lnb_ref, ones_ref, w_ref,
                     o_ref):
    gw = GROUP_WIDTH
    ones = ones_ref[...]
    y = ys_ref[...] + c_ref[...] * v_ref[...]
    mu = _seg_sum(y, ones) * (1.0 / HEAD_DIM)
    d = y - mu
    var = _seg_sum(d * d, ones) * (1.0 / HEAD_DIM)
    yc = (d * lax.rsqrt(var + RWKV_GN_EPS) * lng_ref[...] + lnb_ref[...]) * g_ref[...]
    acc = x_ref[...]
    for n, part in enumerate((ya_ref[...], yb_ref[...], yc, yd_ref[...])):
        acc = acc + _dot(part.astype(BF16), w_ref[n * gw:(n + 1) * gw, :])
    o_ref[...] = acc


def _out_proj(x2, parts, vecs, ones, w, tm=1024):
    m, d = x2.shape
    gw = GROUP_WIDTH
    blk = pl.BlockSpec((tm, gw), lambda i: (i, 0))
    vec = pl.BlockSpec((1, gw), lambda i: (0, 0))
    return pl.pallas_call(
        _out_proj_kernel,
        grid=(m // tm,),
        in_specs=[pl.BlockSpec((tm, d), lambda i: (i, 0))] + [blk] * len(parts) + [vec] * len(vecs)
                 + [pl.BlockSpec(ones.shape, lambda i: (0, 0)), pl.BlockSpec(w.shape, lambda i: (0, 0))],
        out_specs=pl.BlockSpec((tm, d), lambda i: (i, 0)),
        out_shape=jax.ShapeDtypeStruct((m, d), F32),
        compiler_params=_cparams("parallel"),
        name="out_proj",
    )(x2, *parts, *vecs, ones, w)


def _ffn_kernel(x_ref, halo_ref, gain_ref, upg_ref, upv_ref, dwg_ref, dwv_ref, bg_ref, bv_ref, down_ref, fin_ref,
                o_ref, h_ref, acc_ref, ug_ref, uv_ref, *, tiles_per_seq, final_norm):
    j = pl.program_id(1)
    tm = x_ref.shape[0]
    hl = halo_ref.shape[0]

    def norm(x):
        return (x * lax.rsqrt(jnp.mean(x * x, axis=-1, keepdims=True) + RMS_EPS) * gain_ref[...]).astype(BF16)

    @pl.when(j == 0)
    def _():
        first = pl.program_id(0) % tiles_per_seq == 0
        h_ref[0:hl, :] = jnp.where(first, jnp.zeros((hl, x_ref.shape[1]), BF16), norm(halo_ref[...]))
        h_ref[hl:hl + tm, :] = norm(x_ref[...])
        acc_ref[...] = jnp.zeros(acc_ref.shape, F32)

    h = h_ref[...]

    def conv(up_ref, dw_ref, b_ref, u_ref):
        u_ref[...] = _dot(h, up_ref[...])
        out = b_ref[...]
        for tap in range(FFN_CONV_K):
            out = out + dw_ref[tap:tap + 1, :] * u_ref[pl.ds(hl - (FFN_CONV_K - 1) + tap, tm), :]
        return out

    act = _silu(conv(upg_ref, dwg_ref, bg_ref, ug_ref)) * conv(upv_ref, dwv_ref, bv_ref, uv_ref)
    acc_ref[...] += _dot(act.astype(BF16), down_ref[...])

    @pl.when(j == pl.num_programs(1) - 1)
    def _():
        y = x_ref[...] + acc_ref[...]
        if final_norm:
            y = y * lax.rsqrt(jnp.mean(y * y, axis=-1, keepdims=True) + RMS_EPS) * fin_ref[...]
        o_ref[...] = y


def _ffn(x2, gain, up, dw, dwb, down, fin, seq_len, final_norm, tm=512, tn=1408, hl=16):
    m, d = x2.shape
    dff = down.shape[0]
    nj = dff // tn
    kern = functools.partial(_ffn_kernel, tiles_per_seq=seq_len // tm, final_norm=final_norm)
    return pl.pallas_call(
        kern,
        grid=(m // tm, nj),
        in_specs=[pl.BlockSpec((tm, d), lambda i, j: (i, 0)),
                  pl.BlockSpec((hl, d), lambda i, j: (jnp.maximum(i * (tm // hl) - 1, 0), 0)),
                  pl.BlockSpec((1, d), lambda i, j: (0, 0)),
                  pl.BlockSpec((d, tn), lambda i, j: (0, j)),
                  pl.BlockSpec((d, tn), lambda i, j: (0, j + nj)),
                  pl.BlockSpec((FFN_CONV_K, tn), lambda i, j: (0, j)),
                  pl.BlockSpec((FFN_CONV_K, tn), lambda i, j: (0, j + nj)),
                  pl.BlockSpec((1, tn), lambda i, j: (0, j)),
                  pl.BlockSpec((1, tn), lambda i, j: (0, j + nj)),
                  pl.BlockSpec((tn, d), lambda i, j: (j, 0)),
                  pl.BlockSpec((1, d), lambda i, j: (0, 0))],
        out_specs=pl.BlockSpec((tm, d), lambda i, j: (i, 0)),
        out_shape=jax.ShapeDtypeStruct((m, d), F32),
        scratch_shapes=[pltpu.VMEM((hl + tm, d), BF16), pltpu.VMEM((tm, d), F32),
                        pltpu.VMEM((hl + tm, tn), F32), pltpu.VMEM((hl + tm, tn), F32)],
        compiler_params=_cparams("parallel", "arbitrary"),
        name="conv_ffn",
    )(x2, x2, gain, up, up, dw, dw, dwb, dwb, down, fin)


def _block_ones():
    i = np.arange(GROUP_WIDTH) // HEAD_DIM
    return jnp.asarray(i[:, None] == i[None, :], BF16)


def _diag_mask():
    v = np.arange(HEAD_DIM)
    l = np.arange(GROUP_WIDTH) % HEAD_DIM
    return jnp.asarray(v[:, None] == l[None, :], F32)


def _importance_map(n_cmp, n_blk):
    gt = np.zeros((n_blk, n_cmp), np.float32)
    per = SLC_BLOCK // CMP_STRIDE
    for c in range(n_blk * per):
        for i in (c - 1, c):
            if 0 <= i < n_cmp - 1:
                gt[c // per, i] += 1.0
    return jnp.asarray(gt, BF16)


def kernel(x, norm_mix, w_in, cmp_pos, cmp_k_w1, cmp_k_w2, cmp_v_w1, cmp_v_w2, beta_nsa, beta_dil, rwkv_mu, rwkv_w0, rwkv_w_up, rwkv_a0, rwkv_a_up, rwkv_g_up, rwkv_k_k, rwkv_k_a, rwkv_r_k, rwkv_ln_g, rwkv_ln_b, conv_dw, conv_dw_b, conv_ln_g, conv_ln_b, w_out, norm_ffn, ffn_up, ffn_dw, ffn_dw_b, ffn_down, norm_final):
    bsz, seq, d = x.shape
    depth = w_in.shape[0]
    gw = GROUP_WIDTH
    n_cmp = seq // CMP_STRIDE
    n_blk = seq // SLC_BLOCK
    half = CMP_STRIDE * HEAD_DIM

    cols, kv_cols = _proj_columns()
    w_in_p = w_in[:, :, cols].astype(BF16)
    w_kv_p = w_in[:, :, kv_cols].astype(BF16)
    w_out_b = w_out.astype(BF16)
    ffn_up_b = ffn_up.astype(BF16)
    ffn_down_b = ffn_down.astype(BF16)
    ones = _block_ones()
    diag = _diag_mask()
    gt = _importance_map(n_cmp, n_blk)
    row = lambda a: a.reshape(1, -1)

    x2 = x.reshape(bsz * seq, d)
    for l in range(depth):
        proj, kv = _proj_in(x2, row(norm_mix[l]), w_in_p[l], w_kv_p[l], tn=N_PROJ // 2)
        proj, kv = proj.reshape(bsz, seq, N_PROJ), kv.reshape(bsz, seq, N_KV)

        kh = kv[:, :, KV_KC:KV_KC + HEAD_DIM].reshape(bsz, n_cmp, half)
        vh = kv[:, :, KV_VC:KV_VC + HEAD_DIM].reshape(bsz, n_cmp, half)
        pos = cmp_pos[l].reshape(1, 2 * half)
        k_cmp, v_cmp = _compress(kh, vh, pos[:, :half], pos[:, half:],
                                 cmp_k_w1[l].astype(BF16), cmp_k_w2[l].astype(BF16),
                                 cmp_v_w1[l].astype(BF16), cmp_v_w2[l].astype(BF16))
        y_a = _nsa(proj, kv, k_cmp, jnp.swapaxes(v_cmp, 1, 2), gt, row(beta_nsa[l]))

        y_b = _dilated(proj, row(beta_dil[l]))

        rp = dict(mu=row(rwkv_mu[l]), w0=row(rwkv_w0[l]), w_up=rwkv_w_up[l].astype(BF16), a0=row(rwkv_a0[l]),
                  a_up=rwkv_a_up[l].astype(BF16), g_up=rwkv_g_up[l].astype(BF16), k_k=row(rwkv_k_k[l]),
                  k_a=row(rwkv_k_a[l]), r_k=row(rwkv_r_k[l]))
        kk, w, b, k2, yr, vkr, v, g, c = _rwkv_prep(proj, rp, ones)
        y_scan = _rwkv_scan(kk, w, b, k2, yr, vkr, v, ones, diag)

        y_d = _conformer(proj, conv_dw[l], row(conv_dw_b[l]), row(conv_ln_g[l]), row(conv_ln_b[l]))

        flat = lambda a: a.reshape(bsz * seq, gw)
        parts = [flat(a) for a in (y_a, y_b, y_scan, c, v, g, y_d)]
        vecs = [row(rwkv_ln_g[l]), row(rwkv_ln_b[l])]
        x2 = _out_proj(x2, parts, vecs, ones, w_out_b[l])

        x2 = _ffn(x2, row(norm_ffn[l]), ffn_up_b[l], ffn_dw[l], row(ffn_dw_b[l]), ffn_down_b[l],
                  row(norm_final), seq, final_norm=(l == depth - 1))
    return x2.reshape(bsz, seq, d)
```

```python
import functools

import jax
import jax.numpy as jnp
import numpy as np
from jax import lax
from jax.experimental import pallas as pl
from jax.experimental.pallas import tpu as pltpu

F32 = jnp.float32
BF16 = jnp.bfloat16

HEAD_DIM = 64
GROUP_HEADS = 4
GROUP_WIDTH = HEAD_DIM * GROUP_HEADS
CMP_STRIDE = 16
CMP_BLOCK = 32
SLC_BLOCK = 64
N_SEL_BLOCKS = 16
NSA_WINDOW = 512
DILATIONS = (1, 4, 16)
TILE = 128
CONV_K = 31
FFN_CONV_K = 3
RMS_EPS = 1e-6
LN_EPS = 1e-5
RWKV_GN_EPS = 64e-5
NEG = -1e30
FORCE_SCORE = 1e9
SCALE = HEAD_DIM ** -0.5
LOG2_E = 1.4426950408889634
SCAN_UNROLL = 8
SLC_CHUNK = 4
V_ROWS = HEAD_DIM + 16

C_PC, C_PD, C_QD, C_QN = 0, 1024, 1536, 2304
N_PROJ = 2560
KV_KC, KV_VC, KV_KS, KV_VS, KV_KW, KV_VW, KV_GATES = 0, 64, 128, 192, 256, 320, 384
N_KV = 512
DIL_SPAN = 2048

VMEM_LIMIT = 48 * 1024 * 1024


def _cparams(*sem):
    return pltpu.CompilerParams(dimension_semantics=sem, vmem_limit_bytes=VMEM_LIMIT)


def _proj_columns():
    gw, hd = GROUP_WIDTH, HEAD_DIM
    a0 = 0
    q = np.arange(a0, a0 + gw)
    kc, vc, ks, vs, kw, vw = [np.arange(a0 + gw + i * hd, a0 + gw + (i + 1) * hd) for i in range(6)]
    gate0 = a0 + gw + 6 * hd
    gates = np.arange(gate0, gate0 + 3 * GROUP_HEADS)
    b0 = gate0 + 3 * GROUP_HEADS
    qd, kd, vd = [np.arange(b0 + i * gw, b0 + (i + 1) * gw) for i in range(3)]
    c0 = b0 + 3 * gw
    pc = np.arange(c0, c0 + 4 * gw)
    d0 = c0 + 4 * gw
    pd = np.arange(d0, d0 + 2 * gw)
    cols = np.concatenate([pc, pd, qd, kd, vd, q])
    kv_cols = np.concatenate([kc, vc, ks, vs, kw, vw, gates, np.zeros(N_KV - KV_GATES - gates.shape[0], np.int64)])
    assert cols.shape[0] == N_PROJ and kv_cols.shape[0] == N_KV
    return cols, kv_cols


def _dot(a, b):
    return jnp.dot(a, b, preferred_element_type=F32)


def _seg_sum(x, ones_bf16):
    hi = x.astype(BF16)
    lo = (x - hi.astype(F32)).astype(BF16)
    return _dot(hi, ones_bf16) + _dot(lo, ones_bf16)


def _sigmoid(x):
    return 1.0 / (1.0 + jnp.exp(-x))


def _silu(x):
    return x * _sigmoid(x)


def _proj_in_kernel(x_ref, g_ref, w_ref, wkv_ref, o_ref, okv_ref, h_ref):
    @pl.when(pl.program_id(1) == 0)
    def _():
        x = x_ref[...]
        ms = jnp.mean(x * x, axis=-1, keepdims=True)
        h_ref[...] = (x * lax.rsqrt(ms + RMS_EPS) * g_ref[...]).astype(BF16)
        okv_ref[...] = _dot(h_ref[...], wkv_ref[...])

    o_ref[...] = _dot(h_ref[...], w_ref[...])


def _proj_in(x2, gain, w, w_kv, tn, tm=1024):
    m, d = x2.shape
    n, n_kv = w.shape[1], w_kv.shape[1]
    assert n % tn == 0
    return pl.pallas_call(
        _proj_in_kernel,
        grid=(m // tm, n // tn),
        in_specs=[pl.BlockSpec((tm, d), lambda i, j: (i, 0)),
                  pl.BlockSpec((1, d), lambda i, j: (0, 0)),
                  pl.BlockSpec((d, tn), lambda i, j: (0, j)),
                  pl.BlockSpec((d, n_kv), lambda i, j: (0, 0))],
        out_specs=[pl.BlockSpec((tm, tn), lambda i, j: (i, j)),
                   pl.BlockSpec((tm, n_kv), lambda i, j: (i, 0))],
        out_shape=[jax.ShapeDtypeStruct((m, n), F32), jax.ShapeDtypeStruct((m, n_kv), F32)],
        scratch_shapes=[pltpu.VMEM((tm, d), BF16)],
        compiler_params=_cparams("parallel", "arbitrary"),
        name="proj_in",
    )(x2, gain, w, w_kv)


def _compress_kernel(kh_ref, vh_ref, plo_ref, phi_ref, kw1_ref, kw2_ref, vw1_ref, vw2_ref, ko_ref, vo_ref):
    half = kh_ref.shape[2]
    n = kh_ref.shape[1]

    def mlp(xh, w1_ref, w2_ref):
        a = _dot((xh + plo_ref[...]).astype(BF16), w1_ref[0:half, :])
        bm = _dot((xh + phi_ref[...]).astype(BF16), w1_ref[half:2 * half, :])
        hid = a + pltpu.roll(bm, n - 1, axis=0)
        return _dot(_silu(hid).astype(BF16), w2_ref[...])

    ko_ref[0] = mlp(kh_ref[0], kw1_ref, kw2_ref)
    vo_ref[0] = mlp(vh_ref[0], vw1_ref, vw2_ref)


def _compress(kh, vh, pos_lo, pos_hi, kw1, kw2, vw1, vw2):
    b, n, half = kh.shape
    hid = kw1.shape[1]
    full = lambda shape: pl.BlockSpec(shape, lambda i: (0,) * len(shape))
    return pl.pallas_call(
        _compress_kernel,
        grid=(b,),
        in_specs=[pl.BlockSpec((1, n, half), lambda i: (i, 0, 0)),
                  pl.BlockSpec((1, n, half), lambda i: (i, 0, 0)),
                  full((1, half)), full((1, half)),
                  full((2 * half, hid)), full((hid, HEAD_DIM)),
                  full((2 * half, hid)), full((hid, HEAD_DIM))],
        out_specs=[pl.BlockSpec((1, n, HEAD_DIM), lambda i: (i, 0, 0))] * 2,
        out_shape=[jax.ShapeDtypeStruct((b, n, HEAD_DIM), F32)] * 2,
        compiler_params=_cparams("parallel"),
        name="nsa_compress",
    )(kh, vh, pos_lo, pos_hi, kw1, kw2, vw1, vw2)


def _cmp_select(qs_t, t0, kc_ref, vct_ref, gt_ref, score_ref):
    n_cmp = kc_ref.shape[1]
    n_blk = gt_ref.shape[0]
    heads = range(GROUP_HEADS)
    kc = kc_ref[0].astype(BF16)
    vc_t = vct_ref[0].astype(BF16)
    ci = lax.broadcasted_iota(jnp.int32, (n_cmp, TILE), 0)
    tq = t0 + lax.broadcasted_iota(jnp.int32, (n_cmp, TILE), 1)
    bias = jnp.where(jnp.where(ci < n_cmp - 1, CMP_STRIDE * ci + CMP_BLOCK - 1, 1 << 30) <= tq, 0.0, NEG)
    bias = jnp.concatenate([bias] * GROUP_HEADS, axis=1)
    s = _dot(kc, qs_t) + bias
    e = jnp.exp2(s - jnp.max(s, axis=0, keepdims=True))
    p = jnp.where(bias == 0.0, e / jnp.sum(e, axis=0, keepdims=True), 0.0)
    imp_t = functools.reduce(lambda a, b: a + b, [p[:, TILE * h:TILE * (h + 1)] for h in heads])
    o_t = _dot(vc_t, p.astype(BF16))

    hi = imp_t.astype(BF16)
    lo = (imp_t - hi.astype(F32)).astype(BF16)
    blk = _dot(gt_ref[...], hi) + _dot(gt_ref[...], lo)
    j = lax.broadcasted_iota(jnp.int32, (n_blk, TILE), 0)
    t = t0 + lax.broadcasted_iota(jnp.int32, (n_blk, TILE), 1)
    cur = lax.shift_right_logical(t, 6)
    forced = jnp.where(j == 0, 1, jnp.where(j == cur, 1, jnp.where(j == cur - 1, 1, 0)))
    score = jnp.where(j * SLC_BLOCK <= t, jnp.where(forced == 1, FORCE_SCORE, blk), NEG)
    score_ref[...] = score

    def count(pair, rank):
        for jp in (2 * pair, 2 * pair + 1):
            row = score_ref[pl.ds(jp, 1), :]
            rank = rank + jnp.where(row > score, 1, jnp.where(row == score, jnp.where(j > jp, 1, 0), 0))
        return rank

    n_valid = jnp.minimum((t0 + TILE) // SLC_BLOCK, n_blk)
    rank = lax.fori_loop(0, n_valid // 2, count, jnp.zeros((n_blk, TILE), jnp.int32))
    return o_t, jnp.where(rank < N_SEL_BLOCKS, 0.0, NEG)


def _nsa_kernel(q_ref, kc_ref, vct_ref, gt_ref, gl_ref, ge_ref, ks_ref, vs_ref, kw_ref, vw_ref, beta_ref,
                o_ref, sa_ref, sb_ref, sw_ref, score_ref):
    qi = pl.program_id(1)
    cols = GROUP_HEADS * TILE
    heads = range(GROUP_HEADS)
    q_t = (q_ref[0] * (SCALE * LOG2_E)).T
    qs_t = jnp.concatenate([q_t[HEAD_DIM * h:HEAD_DIM * (h + 1), :] for h in heads], axis=1).astype(BF16)
    q_pos = qi * TILE + lax.broadcasted_iota(jnp.int32, (1, TILE), 1)
    from_t = lambda x_t: jnp.concatenate([x_t[:, TILE * h:TILE * (h + 1)] for h in heads], axis=0).T

    def keys(ref, start, n):
        return ref[0, pl.ds(start, n)].reshape(n * TILE, 2 * HEAD_DIM)

    def values_t(ref, start, n):
        v = ref[0, pl.ds(start, n)]
        return jnp.concatenate([v[u] for u in range(n)], axis=1)

    def band_bias(start, n, window):
        key_pos = start * TILE + lax.broadcasted_iota(jnp.int32, (n * TILE, TILE), 0)
        dist = q_pos - key_pos
        ok = jnp.where(dist >= 0, 1, 0) if window is None else jnp.where(dist >= 0, jnp.where(dist <= window, 1, 0), 0)
        return jnp.concatenate([jnp.where(ok == 1, 0.0, NEG)] * GROUP_HEADS, axis=1)

    def col_max(s):
        return jnp.max(s, axis=0, keepdims=True)

    def finish(acc):
        return from_t(acc[0:HEAD_DIM, :] / acc[HEAD_DIM:HEAD_DIM + 1, :])

    n_win = NSA_WINDOW // TILE + 1
    first = jnp.maximum(qi - (n_win - 1), 0)
    w_win = jnp.concatenate([qs_t, jnp.zeros((HEAD_DIM, cols), BF16)], axis=0)
    sw_ref[...] = _dot(keys(kw_ref, first, n_win), w_win) + band_bias(first, n_win, NSA_WINDOW)
    p = jnp.exp2(sw_ref[...] - col_max(sw_ref[...])).astype(BF16)
    o_win = finish(_dot(values_t(vw_ref, first, n_win), p))

    ocmp_t, sel = _cmp_select(qs_t, qi * TILE, kc_ref, vct_ref, gt_ref, score_ref)
    w = jnp.concatenate([qs_t, jnp.concatenate([sel] * GROUP_HEADS, axis=1).astype(BF16)], axis=0)

    def prefetch(ref, c, may_hold_diagonal=True):
        s = _dot(keys(ks_ref, c * SLC_CHUNK, SLC_CHUNK), w)
        if may_hold_diagonal:
            s = s + band_bias(c * SLC_CHUNK, SLC_CHUNK, None)
        ref[...] = s
        return col_max(s)

    def consume(ref, c, m, cm, acc):
        m_new = jnp.maximum(m, cm)
        p = jnp.exp2(ref[...] - m_new).astype(BF16)
        return m_new, jnp.exp2(m - m_new) * acc + _dot(values_t(vs_ref, c * SLC_CHUNK, SLC_CHUNK), p)

    def two_chunks(j, carry):
        m, cm_a, acc = carry
        cm_b = prefetch(sb_ref, 2 * j + 1, may_hold_diagonal=False)
        m, acc = consume(sa_ref, 2 * j, m, cm_a, acc)
        cm_a = prefetch(sa_ref, 2 * j + 2)
        m, acc = consume(sb_ref, 2 * j + 1, m, cm_b, acc)
        return m, cm_a, acc

    n_full = qi // SLC_CHUNK
    n_pairs = n_full // 2
    m, cm_a, acc = lax.fori_loop(0, n_pairs, two_chunks,
                                 (jnp.full((1, cols), NEG, F32), prefetch(sa_ref, 0), jnp.zeros((V_ROWS, cols), F32)))
    odd = 2 * n_pairs + 1

    def last_two():
        cm_b = prefetch(sb_ref, odd)
        m1, acc1 = consume(sa_ref, 2 * n_pairs, m, cm_a, acc)
        return consume(sb_ref, odd, m1, cm_b, acc1)[1]

    def last_one():
        return consume(sa_ref, 2 * n_pairs, m, cm_a, acc)[1]

    o_slc = finish(lax.cond(odd <= n_full, last_two, last_one))

    sg = _sigmoid(gl_ref[0])
    sg_hi = sg.astype(BF16)
    sg_lo = (sg - sg_hi.astype(F32)).astype(BF16)
    gate = lambda j: _dot(sg_hi, ge_ref[j]) + _dot(sg_lo, ge_ref[j])
    y = gate(0) * from_t(ocmp_t) + gate(1) * o_slc + gate(2) * o_win
    ms = jnp.mean(y * y, axis=-1, keepdims=True)
    o_ref[0] = (y * lax.rsqrt(ms + RMS_EPS) * beta_ref[...]).astype(o_ref.dtype)


def _key_tiles(k, onehot):
    b, s, _ = k.shape
    aug = jnp.zeros((b, s, HEAD_DIM), BF16) if onehot is None else jnp.broadcast_to(onehot, (b, s, HEAD_DIM))
    return jnp.concatenate([k.astype(BF16), aug], axis=-1).reshape(b, s // TILE, TILE, 2 * HEAD_DIM)


def _value_tiles_t(v):
    b, s, _ = v.shape
    v_t = v.astype(BF16).reshape(b, s // TILE, TILE, HEAD_DIM).transpose(0, 1, 3, 2)
    pad = jnp.zeros((b, s // TILE, V_ROWS - HEAD_DIM, TILE), BF16).at[:, :, 0, :].set(1.0)
    return jnp.concatenate([v_t, pad], axis=2)


def _nsa(proj, kv, k_cmp, v_cmp_t, gt, beta):
    b, s, _ = proj.shape
    gw = GROUP_WIDTH
    n_cmp = k_cmp.shape[1]
    n_blk = gt.shape[0]
    assert n_blk == HEAD_DIM, "the selection rows share the 128-deep contraction with the 64 head dims"
    piece = lambda c: kv[:, :, c:c + HEAD_DIM]
    blk_of_key = np.arange(s)[:, None] // SLC_BLOCK == np.arange(n_blk)[None, :]
    ks = _key_tiles(piece(KV_KS), jnp.asarray(blk_of_key, BF16))
    kw = _key_tiles(piece(KV_KW), None)
    vs = _value_tiles_t(piece(KV_VS))
    vw = _value_tiles_t(piece(KV_VW))
    ge = np.zeros((3, TILE, gw), np.float32)
    for j in range(3):
        for h in range(GROUP_HEADS):
            ge[j, 3 * h + j, HEAD_DIM * h:HEAD_DIM * (h + 1)] = 1.0
    gate_expand = jnp.asarray(ge, BF16)
    nt = s // TILE
    col = lambda c: pl.BlockSpec((1, TILE, gw), lambda i, j: (i, j, c // gw))
    ktile = pl.BlockSpec((1, nt, TILE, 2 * HEAD_DIM), lambda i, j: (i, 0, 0, 0))
    vtile = pl.BlockSpec((1, nt, V_ROWS, TILE), lambda i, j: (i, 0, 0, 0))
    assert nt % SLC_CHUNK == 0 and nt >= NSA_WINDOW // TILE + 1
    return pl.pallas_call(
        _nsa_kernel,
        grid=(b, nt),
        in_specs=[col(C_QN),
                  pl.BlockSpec((1, n_cmp, HEAD_DIM), lambda i, j: (i, 0, 0)),
                  pl.BlockSpec((1, HEAD_DIM, n_cmp), lambda i, j: (i, 0, 0)),
                  pl.BlockSpec(gt.shape, lambda i, j: (0, 0)),
                  pl.BlockSpec((1, TILE, TILE), lambda i, j: (i, j, KV_GATES // TILE)),
                  pl.BlockSpec(gate_expand.shape, lambda i, j: (0, 0, 0)),
                  ktile, vtile, ktile, vtile,
                  pl.BlockSpec((1, gw), lambda i, j: (0, 0))],
        out_specs=pl.BlockSpec((1, TILE, gw), lambda i, j: (i, j, 0)),
        out_shape=jax.ShapeDtypeStruct((b, s, gw), BF16),
        scratch_shapes=[pltpu.VMEM((SLC_CHUNK * TILE, GROUP_HEADS * TILE), F32),
                        pltpu.VMEM((SLC_CHUNK * TILE, GROUP_HEADS * TILE), F32),
                        pltpu.VMEM((NSA_WINDOW + TILE, GROUP_HEADS * TILE), F32),
                        pltpu.VMEM((n_blk, TILE), F32)],
        compiler_params=_cparams("parallel", "arbitrary"),
        name="nsa",
    )(proj, k_cmp, v_cmp_t, gt, kv, gate_expand, ks, vs, kw, vw, beta)


def _dilated_kernel(q0_ref, q1_ref, k0_ref, k1_ref, v0_ref, v1_ref, beta_ref, o_ref, og_ref, lg_ref):
    gw = GROUP_WIDTH
    cols = GROUP_HEADS * TILE
    p0 = pl.program_id(1) * DIL_SPAN

    def load(refs, first, n, rows):
        window = pl.ds(pl.multiple_of(first, TILE), n)
        return jnp.concatenate([ref.at[0, window][rows, :] for ref in refs], axis=1)

    def store(ref, g, first, n, rows, val):
        window = pl.ds(pl.multiple_of(first, TILE), n)
        for u in range(2):
            ref.at[g, u, window][rows, :] = val[:, TILE * u:TILE * (u + 1)]

    row_head = lax.shift_right_logical(lax.broadcasted_iota(jnp.int32, (gw, TILE), 0), 6)
    rel = (lax.broadcasted_iota(jnp.int32, (2 * TILE, TILE), 1)
           - lax.broadcasted_iota(jnp.int32, (2 * TILE, TILE), 0))

    for g, dil in enumerate(DILATIONS):
        n_t = DIL_SPAN // (TILE * dil)

        def body(i, carry, r, g=g, dil=dil):
            sub0 = p0 // dil + i * TILE
            ksub0 = jnp.maximum(sub0 - TILE, 0)
            rows_q = pl.ds(r, TILE, stride=dil)
            rows_k = pl.ds(r, 2 * TILE, stride=dil)
            keys = functools.partial(load, first=dil * ksub0, n=2 * dil * TILE, rows=rows_k)
            q_t = (load((q0_ref, q1_ref), dil * sub0, dil * TILE, rows_q) * (SCALE * LOG2_E)).T
            wq = jnp.concatenate([jnp.where(row_head == h, q_t, 0.0) for h in range(GROUP_HEADS)],
                                 axis=1).astype(BF16)
            dist = rel + (sub0 - ksub0)
            bias = jnp.where(dist >= 0, jnp.where(dist <= TILE, 0.0, NEG), NEG)
            bias = jnp.concatenate([bias] * GROUP_HEADS, axis=1)
            s = _dot(keys((k0_ref, k1_ref)).astype(BF16), wq) + bias
            m = jnp.max(s, axis=0, keepdims=True)
            p = jnp.exp2(s - m)
            l = jnp.sum(p, axis=0, keepdims=True)
            p = p.astype(BF16)
            v_t = keys((v0_ref, v1_ref)).T.astype(BF16)
            lse = m + jnp.log2(l)
            o_t, l_t = [], []
            for h in range(GROUP_HEADS):
                qs = slice(TILE * h, TILE * (h + 1))
                o_t.append(_dot(v_t[HEAD_DIM * h:HEAD_DIM * (h + 1), :], p[:, qs]) / l[:, qs])
                l_t.append(jnp.broadcast_to(lse[:, qs], (HEAD_DIM, TILE)))
            store(og_ref, g, dil * i * TILE, dil * TILE, rows_q, jnp.concatenate(o_t, axis=0).T)
            store(lg_ref, g, dil * i * TILE, dil * TILE, rows_q, jnp.concatenate(l_t, axis=0).T)
            return carry

        for r in range(dil):
            if n_t == 1:
                body(0, 0, r)
            else:
                lax.fori_loop(0, n_t, functools.partial(body, r=r), 0, unroll=4)

    piece = 256

    def combine(c, carry):
        rows = pl.ds(pl.multiple_of(c * piece, piece), piece)
        both = lambda ref, g: jnp.concatenate([ref[g, 0, rows, :], ref[g, 1, rows, :]], axis=1)
        ls = [both(lg_ref, g) for g in range(len(DILATIONS))]
        mx = functools.reduce(jnp.maximum, ls)
        es = [jnp.exp2(x - mx) for x in ls]
        den = functools.reduce(lambda a, b: a + b, es)
        y = functools.reduce(lambda a, b: a + b, [(e / den) * both(og_ref, g) for g, e in enumerate(es)])
        y = y * lax.rsqrt(jnp.mean(y * y, axis=-1, keepdims=True) + RMS_EPS) * beta_ref[...]
        o_ref[0, rows, :] = y.astype(o_ref.dtype)
        return carry

    lax.fori_loop(0, DIL_SPAN // piece, combine, 0)


def _dilated(proj, beta):
    b, s, _ = proj.shape
    gw = GROUP_WIDTH
    assert s % DIL_SPAN == 0 and s >= 2 * TILE * max(DILATIONS)
    lane_block = lambda n: pl.BlockSpec((1, s, TILE), lambda i, j: (i, 0, C_QD // TILE + n))
    return pl.pallas_call(
        _dilated_kernel,
        grid=(b, s // DIL_SPAN),
        in_specs=[lane_block(n) for n in range(3 * gw // TILE)] + [pl.BlockSpec((1, gw), lambda i, j: (0, 0))],
        out_specs=pl.BlockSpec((1, DIL_SPAN, gw), lambda i, j: (i, j, 0)),
        out_shape=jax.ShapeDtypeStruct((b, s, gw), BF16),
        scratch_shapes=[pltpu.VMEM((len(DILATIONS), gw // TILE, DIL_SPAN, TILE), F32)] * 2,
        compiler_params=_cparams("parallel", "arbitrary"),
        name="dilated",
    )(*([proj] * (3 * gw // TILE)), beta)


def _rwkv_prep_kernel(cur_ref, halo_ref, mu_ref, w0_ref, wup_ref, a0_ref, aup_ref, gup_ref, kk_w_ref, ka_ref,
                      rk_ref, ones_ref,
                      kk_o, w_o, b_o, k_o, yr_o, vkr_o, v_o, g_o, c_o):
    gw = GROUP_WIDTH
    tm = cur_ref.shape[1]
    ones = ones_ref[...]
    x = cur_ref[0]
    last = halo_ref[0][7:8, :]
    last = jnp.where(pl.program_id(1) > 0, last, 0.0)
    rid = lax.broadcasted_iota(jnp.int32, (tm, 1), 0)
    xprev = jnp.where(rid == 0, last, pltpu.roll(x, 1, axis=0))
    xs = x + (xprev - x) * mu_ref[...]
    r, k, v = xs[:, 0:gw], xs[:, gw:2 * gw], xs[:, 2 * gw:3 * gw]
    wd, ad, gd = xs[:, 3 * gw:3 * gw + 64], xs[:, 3 * gw + 64:3 * gw + 128], xs[:, 3 * gw + 128:4 * gw]
    z = -(w0_ref[...] + _dot(jnp.tanh(wd).astype(BF16), wup_ref[...]))
    softplus = jnp.maximum(z, 0.0) + jnp.log1p(jnp.exp(-jnp.abs(z)))
    w = jnp.exp(-jnp.exp(-softplus - 0.5))
    a = _sigmoid(a0_ref[...] + _dot(ad.astype(BF16), aup_ref[...]))
    g = _dot(_sigmoid(gd).astype(BF16), gup_ref[...])
    kk = k * kk_w_ref[...]
    kk = kk * lax.rsqrt(_seg_sum(kk * kk, ones) + 1e-12)
    k2 = k * (1.0 + (a - 1.0) * ka_ref[...])
    b = kk * a
    br = _seg_sum(b * r, ones)
    kr = _seg_sum(k2 * r, ones)
    kk_o[0] = kk
    w_o[0] = w
    b_o[0] = b
    k_o[0] = k2
    yr_o[0] = w * r - kk * br
    vkr_o[0] = v * kr
    v_o[0] = v
    g_o[0] = g
    c_o[0] = _seg_sum(r * k2 * rk_ref[...], ones)


def _rwkv_prep(proj, p, ones, tm=512):
    b, s, _ = proj.shape
    gw = GROUP_WIDTH
    vec = lambda n: pl.BlockSpec((1, n), lambda i, j: (0, 0))
    mat = lambda a: pl.BlockSpec(a.shape, lambda i, j: (0, 0))
    out = pl.BlockSpec((1, tm, gw), lambda i, j: (i, j, 0))
    return pl.pallas_call(
        _rwkv_prep_kernel,
        grid=(b, s // tm),
        in_specs=[pl.BlockSpec((1, tm, 4 * gw), lambda i, j: (i, j, C_PC // (4 * gw))),
                  pl.BlockSpec((1, 8, 4 * gw), lambda i, j: (i, jnp.maximum(j * (tm // 8) - 1, 0), C_PC // (4 * gw))),
                  vec(4 * gw), vec(gw), mat(p["w_up"]), vec(gw), mat(p["a_up"]), mat(p["g_up"]),
                  vec(gw), vec(gw), vec(gw), mat(ones)],
        out_specs=[out] * 9,
        out_shape=[jax.ShapeDtypeStruct((b, s, gw), F32)] * 9,
        compiler_params=_cparams("parallel", "arbitrary"),
        name="rwkv_prep",
    )(proj, proj, p["mu"], p["w0"], p["w_up"], p["a0"], p["a_up"], p["g_up"], p["k_k"], p["k_a"], p["r_k"], ones)


def _rwkv_scan_kernel(kk_ref, w_ref, b_ref, k_ref, yr_ref, vkr_ref, v_ref, ones_ref, diag_ref,
                      y_ref, state_ref, yacc_ref):
    nb, tb = kk_ref.shape[0], kk_ref.shape[1]

    @pl.when(pl.program_id(0) == 0)
    def _():
        state_ref[...] = jnp.zeros(state_ref.shape, F32)
        yacc_ref[...] = jnp.zeros(yacc_ref.shape, F32)

    ones = ones_ref[...]
    diag = diag_ref[...].astype(BF16)
    lane = jnp.bitwise_and(lax.broadcasted_iota(jnp.int32, (HEAD_DIM, GROUP_WIDTH), 1), HEAD_DIM - 1)

    def group(c, carry):
        def step(tt, carry2):
            t = c * HEAD_DIM + tt
            here = lane == tt
            row = lambda ref, i: ref[i, pl.ds(t, 1), :]
            st = [state_ref[i] for i in range(nb)]
            sb = [st[i].astype(BF16) for i in range(nb)]
            rb = lambda ref, i: row(ref, i).astype(BF16)
            reduce = lambda parts: _dot(jnp.concatenate(parts, axis=0), ones)
            red_v = reduce([rb(v_ref, i) * diag for i in range(nb)])
            red_s = reduce([sb[i] * rb(kk_ref, i) for i in range(nb)])
            red_y = reduce([sb[i] * rb(yr_ref, i) + rb(vkr_ref, i) * diag for i in range(nb)])
            part = lambda red, n: red[n * HEAD_DIM:(n + 1) * HEAD_DIM, :]
            for i in range(nb):
                sa, vcol, ycol = part(red_s, i), part(red_v, i), part(red_y, i)
                state_ref[i] = st[i] * row(w_ref, i) - sa * row(b_ref, i) + vcol * row(k_ref, i)
                yacc_ref[i] = jnp.where(here, ycol, yacc_ref[i])
            return carry2

        lax.fori_loop(0, HEAD_DIM, step, 0, unroll=SCAN_UNROLL)
        rows = pl.ds(pl.multiple_of(c * HEAD_DIM, HEAD_DIM), HEAD_DIM)
        for i in range(nb):
            full_t = jnp.concatenate([yacc_ref[i], jnp.zeros((HEAD_DIM, GROUP_WIDTH), F32)], axis=0).T
            y_ref[i, rows, :] = jnp.concatenate(
                [full_t[HEAD_DIM * h:HEAD_DIM * (h + 1), 0:HEAD_DIM] for h in range(GROUP_HEADS)], axis=1)
        return carry

    lax.fori_loop(0, tb // HEAD_DIM, group, 0)


def _rwkv_scan(kk, w, b, k, yr, vkr, v, ones, diag, tb=128):
    nb, s, gw = kk.shape
    inp = pl.BlockSpec((nb, tb, gw), lambda j: (0, j, 0))
    const = lambda a: pl.BlockSpec(a.shape, lambda j: (0, 0))
    return pl.pallas_call(
        _rwkv_scan_kernel,
        grid=(s // tb,),
        in_specs=[inp] * 7 + [const(ones), const(diag)],
        out_specs=inp,
        out_shape=jax.ShapeDtypeStruct((nb, s, gw), F32),
        scratch_shapes=[pltpu.VMEM((nb, HEAD_DIM, gw), F32), pltpu.VMEM((nb, HEAD_DIM, gw), F32)],
        compiler_params=_cparams("arbitrary"),
        name="rwkv_scan",
    )(kk, w, b, k, yr, vkr, v, ones, diag)


def _conformer_kernel(cur_ref, halo_ref, dw_ref, dwb_ref, lng_ref, lnb_ref, o_ref, g_ref, sh_ref):
    gw = GROUP_WIDTH
    tm = cur_ref.shape[1]
    hl = halo_ref.shape[1]
    sub = 8
    glu = lambda x: x[:, 0:gw] * _sigmoid(x[:, gw:2 * gw])
    g_ref[0:hl, :] = jnp.where(pl.program_id(1) > 0, glu(halo_ref[0]), 0.0)
    g_ref[hl:hl + tm, :] = glu(cur_ref[0])
    for r in range(1, sub):
        sh_ref[r - 1] = g_ref[pl.ds(r, sh_ref.shape[1]), :]
    acc = jnp.zeros((tm, gw), F32)
    for j in range(CONV_K):
        off = hl - (CONV_K - 1) + j
        r, a = off % sub, off - off % sub
        tap = g_ref[pl.ds(a, tm), :] if r == 0 else sh_ref[r - 1, pl.ds(a, tm), :]
        acc = acc + dw_ref[j:j + 1, :] * tap
    h = acc + dwb_ref[...]
    mu = jnp.mean(h, axis=-1, keepdims=True)
    d = h - mu
    var = jnp.mean(d * d, axis=-1, keepdims=True)
    o_ref[0] = _silu(d * lax.rsqrt(var + LN_EPS) * lng_ref[...] + lnb_ref[...]).astype(o_ref.dtype)


def _conformer(proj, dw, dwb, lng, lnb, tm=1024, hl=32):
    b, s, _ = proj.shape
    gw = GROUP_WIDTH
    vec = pl.BlockSpec((1, gw), lambda i, j: (0, 0))
    return pl.pallas_call(
        _conformer_kernel,
        grid=(b, s // tm),
        in_specs=[pl.BlockSpec((1, tm, 2 * gw), lambda i, j: (i, j, C_PD // (2 * gw))),
                  pl.BlockSpec((1, hl, 2 * gw), lambda i, j: (i, jnp.maximum(j * (tm // hl) - 1, 0), C_PD // (2 * gw))),
                  pl.BlockSpec((CONV_K, gw), lambda i, j: (0, 0)), vec, vec, vec],
        out_specs=pl.BlockSpec((1, tm, gw), lambda i, j: (i, j, 0)),
        out_shape=jax.ShapeDtypeStruct((b, s, gw), BF16),
        scratch_shapes=[pltpu.VMEM((hl + tm, gw), F32), pltpu.VMEM((7, hl + tm - 8, gw), F32)],
        compiler_params=_cparams("parallel", "arbitrary"),
        name="conformer_conv",
    )(proj, proj, dw, dwb, lng, lnb)


def _out_proj_kernel(x_ref, ya_ref, yb_ref, ys_ref, c_ref, v_ref, g_ref, yd_ref, lng_ref, lnb_ref, ones_ref, w_ref,
                     o_ref):
    gw = GROUP_WIDTH
    ones = ones_ref[...]
    y = ys_ref[...] + c_ref[...] * v_ref[...]
    mu = _seg_sum(y, ones) * (1.0 / HEAD_DIM)
    d = y - mu
    var = _seg_sum(d * d, ones) * (1.0 / HEAD_DIM)
    yc = (d * lax.rsqrt(var + RWKV_GN_EPS) * lng_ref[...] + lnb_ref[...]) * g_ref[...]
    acc = x_ref[...]
    for n, part in enumerate((ya_ref[...], yb_ref[...], yc, yd_ref[...])):
        acc = acc + _dot(part.astype(BF16), w_ref[n * gw:(n + 1) * gw, :])
    o_ref[...] = acc


def _out_proj(x2, parts, vecs, ones, w, tm=1024):
    m, d = x2.shape
    gw = GROUP_WIDTH
    blk = pl.BlockSpec((tm, gw), lambda i: (i, 0))
    vec = pl.BlockSpec((1, gw), lambda i: (0, 0))
    return pl.pallas_call(
        _out_proj_kernel,
        grid=(m // tm,),
        in_specs=[pl.BlockSpec((tm, d), lambda i: (i, 0))] + [blk] * len(parts) + [vec] * len(vecs)
                 + [pl.BlockSpec(ones.shape, lambda i: (0, 0)), pl.BlockSpec(w.shape, lambda i: (0, 0))],
        out_specs=pl.BlockSpec((tm, d), lambda i: (i, 0)),
        out_shape=jax.ShapeDtypeStruct((m, d), F32),
        compiler_params=_cparams("parallel"),
        name="out_proj",
    )(x2, *parts, *vecs, ones, w)


def _ffn_kernel(x_ref, halo_ref, gain_ref, upg_ref, upv_ref, dwg_ref, dwv_ref, bg_ref, bv_ref, down_ref, fin_ref,
                o_ref, h_ref, acc_ref, ug_ref, uv_ref, *, tiles_per_seq, final_norm):
    j = pl.program_id(1)
    tm = x_ref.shape[0]
    hl = halo_ref.shape[0]

    def norm(x):
        return (x * lax.rsqrt(jnp.mean(x * x, axis=-1, keepdims=True) + RMS_EPS) * gain_ref[...]).astype(BF16)

    @pl.when(j == 0)
    def _():
        first = pl.program_id(0) % tiles_per_seq == 0
        h_ref[0:hl, :] = jnp.where(first, jnp.zeros((hl, x_ref.shape[1]), BF16), norm(halo_ref[...]))
        h_ref[hl:hl + tm, :] = norm(x_ref[...])
        acc_ref[...] = jnp.zeros(acc_ref.shape, F32)

    h = h_ref[...]

    def conv(up_ref, dw_ref, b_ref, u_ref):
        u_ref[...] = _dot(h, up_ref[...])
        out = b_ref[...]
        for tap in range(FFN_CONV_K):
            out = out + dw_ref[tap:tap + 1, :] * u_ref[pl.ds(hl - (FFN_CONV_K - 1) + tap, tm), :]
        return out

    act = _silu(conv(upg_ref, dwg_ref, bg_ref, ug_ref)) * conv(upv_ref, dwv_ref, bv_ref, uv_ref)
    acc_ref[...] += _dot(act.astype(BF16), down_ref[...])

    @pl.when(j == pl.num_programs(1) - 1)
    def _():
        y = x_ref[...] + acc_ref[...]
        if final_norm:
            y = y * lax.rsqrt(jnp.mean(y * y, axis=-1, keepdims=True) + RMS_EPS) * fin_ref[...]
        o_ref[...] = y


def _ffn(x2, gain, up, dw, dwb, down, fin, seq_len, final_norm, tm=512, tn=1408, hl=16):
    m, d = x2.shape
    dff = down.shape[0]
    nj = dff // tn
    kern = functools.partial(_ffn_kernel, tiles_per_seq=seq_len // tm, final_norm=final_norm)
    return pl.pallas_call(
        kern,
        grid=(m // tm, nj),
        in_specs=[pl.BlockSpec((tm, d), lambda i, j: (i, 0)),
                  pl.BlockSpec((hl, d), lambda i, j: (jnp.maximum(i * (tm // hl) - 1, 0), 0)),
                  pl.BlockSpec((1, d), lambda i, j: (0, 0)),
                  pl.BlockSpec((d, tn), lambda i, j: (0, j)),
                  pl.BlockSpec((d, tn), lambda i, j: (0, j + nj)),
                  pl.BlockSpec((FFN_CONV_K, tn), lambda i, j: (0, j)),
                  pl.BlockSpec((FFN_CONV_K, tn), lambda i, j: (0, j + nj)),
                  pl.BlockSpec((1, tn), lambda i, j: (0, j)),
                  pl.BlockSpec((1, tn), lambda i, j: (0, j + nj)),
                  pl.BlockSpec((tn, d), lambda i, j: (j, 0)),
                  pl.BlockSpec((1, d), lambda i, j: (0, 0))],
        out_specs=pl.BlockSpec((tm, d), lambda i, j: (i, 0)),
        out_shape=jax.ShapeDtypeStruct((m, d), F32),
        scratch_shapes=[pltpu.VMEM((hl + tm, d), BF16), pltpu.VMEM((tm, d), F32),
                        pltpu.VMEM((hl + tm, tn), F32), pltpu.VMEM((hl + tm, tn), F32)],
        compiler_params=_cparams("parallel", "arbitrary"),
        name="conv_ffn",
    )(x2, x2, gain, up, up, dw, dw, dwb, dwb, down, fin)


def _block_ones():
    i = np.arange(GROUP_WIDTH) // HEAD_DIM
    return jnp.asarray(i[:, None] == i[None, :], BF16)


def _diag_mask():
    v = np.arange(HEAD_DIM)
    l = np.arange(GROUP_WIDTH) % HEAD_DIM
    return jnp.asarray(v[:, None] == l[None, :], F32)


def _importance_map(n_cmp, n_blk):
    gt = np.zeros((n_blk, n_cmp), np.float32)
    per = SLC_BLOCK // CMP_STRIDE
    for c in range(n_blk * per):
        for i in (c - 1, c):
            if 0 <= i < n_cmp - 1:
                gt[c // per, i] += 1.0
    return jnp.asarray(gt, BF16)


def kernel(x, norm_mix, w_in, cmp_pos, cmp_k_w1, cmp_k_w2, cmp_v_w1, cmp_v_w2, beta_nsa, beta_dil, rwkv_mu, rwkv_w0, rwkv_w_up, rwkv_a0, rwkv_a_up, rwkv_g_up, rwkv_k_k, rwkv_k_a, rwkv_r_k, rwkv_ln_g, rwkv_ln_b, conv_dw, conv_dw_b, conv_ln_g, conv_ln_b, w_out, norm_ffn, ffn_up, ffn_dw, ffn_dw_b, ffn_down, norm_final):
    bsz, seq, d = x.shape
    depth = w_in.shape[0]
    gw = GROUP_WIDTH
    n_cmp = seq // CMP_STRIDE
    n_blk = seq // SLC_BLOCK
    half = CMP_STRIDE * HEAD_DIM

    cols, kv_cols = _proj_columns()
    w_in_p = w_in[:, :, cols].astype(BF16)
    w_kv_p = w_in[:, :, kv_cols].astype(BF16)
    w_out_b = w_out.astype(BF16)
    ffn_up_b = ffn_up.astype(BF16)
    ffn_down_b = ffn_down.astype(BF16)
    ones = _block_ones()
    diag = _diag_mask()
    gt = _importance_map(n_cmp, n_blk)
    row = lambda a: a.reshape(1, -1)

    x2 = x.reshape(bsz * seq, d)
    for l in range(depth):
        proj, kv = _proj_in(x2, row(norm_mix[l]), w_in_p[l], w_kv_p[l], tn=N_PROJ // 2)
        proj, kv = proj.reshape(bsz, seq, N_PROJ), kv.reshape(bsz, seq, N_KV)

        kh = kv[:, :, KV_KC:KV_KC + HEAD_DIM].reshape(bsz, n_cmp, half)
        vh = kv[:, :, KV_VC:KV_VC + HEAD_DIM].reshape(bsz, n_cmp, half)
        pos = cmp_pos[l].reshape(1, 2 * half)
        k_cmp, v_cmp = _compress(kh, vh, pos[:, :half], pos[:, half:],
                                 cmp_k_w1[l].astype(BF16), cmp_k_w2[l].astype(BF16),
                                 cmp_v_w1[l].astype(BF16), cmp_v_w2[l].astype(BF16))
        y_a = _nsa(proj, kv, k_cmp, jnp.swapaxes(v_cmp, 1, 2), gt, row(beta_nsa[l]))

        y_b = _dilated(proj, row(beta_dil[l]))

        rp = dict(mu=row(rwkv_mu[l]), w0=row(rwkv_w0[l]), w_up=rwkv_w_up[l].astype(BF16), a0=row(rwkv_a0[l]),
                  a_up=rwkv_a_up[l].astype(BF16), g_up=rwkv_g_up[l].astype(BF16), k_k=row(rwkv_k_k[l]),
                  k_a=row(rwkv_k_a[l]), r_k=row(rwkv_r_k[l]))
        kk, w, b, k2, yr, vkr, v, g, c = _rwkv_prep(proj, rp, ones)
        y_scan = _rwkv_scan(kk, w, b, k2, yr, vkr, v, ones, diag)

        y_d = _conformer(proj, conv_dw[l], row(conv_dw_b[l]), row(conv_ln_g[l]), row(conv_ln_b[l]))

        flat = lambda a: a.reshape(bsz * seq, gw)
        parts = [flat(a) for a in (y_a, y_b, y_scan, c, v, g, y_d)]
        vecs = [row(rwkv_ln_g[l]), row(rwkv_ln_b[l])]
        x2 = _out_proj(x2, parts, vecs, ones, w_out_b[l])

        x2 = _ffn(x2, row(norm_ffn[l]), ffn_up_b[l], ffn_dw[l], row(ffn_dw_b[l]), ffn_down_b[l],
                  row(norm_final), seq, final_norm=(l == depth - 1))
    return x2.reshape(bsz, seq, d)
```
